```python
import jax, jax.numpy as jnp
from jax import lax
import numpy as np

D_MODEL = 1024
BATCH = 2
SEQ = 8192
DEPTH = 1

N_META = 16
D_MIX = D_MODEL
D_CONV = D_MIX // 2
CONV_HEADS = 8
CONV_WIDTH = 3
D_RET = D_MIX - D_CONV
RET_HEADS = 4
RET_HEAD_DIM = D_RET // RET_HEADS
CHUNK = 128
ROPE_BASE = 10000.0
EPS = 1e-6
N_PROJ = 8

kernel_name = "hymba_conv_retention_hybrid"


def rms_norm(x, g):
    xf = x.astype(jnp.float32)
    y = xf * lax.rsqrt(jnp.mean(xf * xf, axis=-1, keepdims=True) + EPS)
    return (y * g.astype(jnp.float32)).astype(x.dtype)


def rotary(t, pos):
    half = t.shape[-1] // 2
    freqs = 1.0 / (ROPE_BASE ** (jnp.arange(half, dtype=jnp.float32) / half))
    ang = pos.astype(jnp.float32)[:, None] * freqs[None, :]
    cos = jnp.cos(ang)[None, :, None, :]
    sin = jnp.sin(ang)[None, :, None, :]
    t1, t2 = t[..., :half], t[..., half:]
    return jnp.concatenate([t1 * cos - t2 * sin, t1 * sin + t2 * cos], axis=-1)


def short_conv_branch(h, b_gate, c_gate, conv_w):
    u = c_gate * h
    kern = conv_w.reshape(CONV_WIDTH, 1, D_CONV).astype(u.dtype)
    conv = lax.conv_general_dilated(
        u, kern, window_strides=(1,), padding=[(CONV_WIDTH - 1, 0)],
        dimension_numbers=('NWC', 'WIO', 'NWC'), feature_group_count=D_CONV)
    return b_gate * conv


def retention_chunkwise(q, k, v):
    bsz, L, H, d = q.shape
    pad = CHUNK - N_META
    P = L + pad
    n_chunks = P // CHUNK

    def to_chunks(t):
        t = jnp.pad(t, ((0, 0), (pad, 0), (0, 0), (0, 0)))
        return t.reshape(bsz, n_chunks, CHUNK, H, d).transpose(0, 3, 1, 2, 4)

    qc, kc, vc = to_chunks(q), to_chunks(k), to_chunks(v)
    log_g = jnp.log(1.0 - 2.0 ** (-5.0 - jnp.arange(H, dtype=jnp.float32)))
    idx = jnp.arange(CHUNK, dtype=jnp.float32)
    diff = idx[:, None] - idx[None, :]
    decay = jnp.where(diff[None] >= 0, jnp.exp(diff[None] * log_g[:, None, None]), 0.0)

    scores = jnp.einsum('bhnid,bhnjd->bhnij', qc, kc) * decay[None, :, None]
    inner = jnp.einsum('bhnij,bhnje->bhnie', scores, vc)

    zeta = jnp.exp((CHUNK - 1 - idx)[None, :] * log_g[:, None])
    upd = jnp.einsum('bhnjd,bhnje->nbhde', kc * zeta[None, :, None, :, None], vc)
    chunk_decay = jnp.exp(CHUNK * log_g)[None, :, None, None]

    def step(state, u):
        return chunk_decay * state + u, state

    init = jnp.zeros((bsz, H, d, d), jnp.float32)
    _, states = lax.scan(step, init, upd)
    xi = jnp.exp((idx + 1.0)[None, :] * log_g[:, None])
    cross = jnp.einsum('bhnid,nbhde->bhnie', qc * xi[None, :, None, :, None], states)

    out = (inner + cross).transpose(0, 2, 3, 1, 4).reshape(bsz, P, H, d)
    return out[:, pad:]


def head_group_norm(o, g):
    mu = jnp.mean(o, axis=-1, keepdims=True)
    var = jnp.mean(jnp.square(o - mu), axis=-1, keepdims=True)
    y = (o - mu) * lax.rsqrt(var + EPS)
    bsz, L = o.shape[:2]
    return y.reshape(bsz, L, D_RET) * g.astype(jnp.float32)


def mixer_layer(h, norm_g, w_in, conv_w, ret_norm_g, w_out, pos):
    bsz, L, _ = h.shape
    hn = rms_norm(h, norm_g)
    proj = jnp.einsum('bld,de->ble', hn, w_in)
    cx, cb, cc, cg, q, k, v, rg = jnp.split(proj, N_PROJ, axis=-1)

    conv_out = short_conv_branch(cx, cb, cc, conv_w) * jax.nn.silu(cg)

    shp = (bsz, L, RET_HEADS, RET_HEAD_DIM)
    qf = rotary(q.reshape(shp).astype(jnp.float32), pos) * (RET_HEAD_DIM ** -0.5)
    kf = rotary(k.reshape(shp).astype(jnp.float32), pos)
    vf = v.reshape(shp).astype(jnp.float32)
    ret = head_group_norm(retention_chunkwise(qf, kf, vf), ret_norm_g)
    ret_out = ret.astype(h.dtype) * jax.nn.silu(rg)

    mixed = jnp.concatenate([conv_out, ret_out], axis=-1)
    return h + jnp.einsum('ble,ed->bld', mixed, w_out)


def setup_inputs(seed: int = 0) -> dict:
    key = jax.random.key(seed)
    ks = jax.random.split(key, 8)
    x = jax.random.normal(ks[0], (BATCH, SEQ, D_MODEL), jnp.float32)
    meta = jax.random.normal(ks[1], (N_META, D_MODEL), jnp.float32)
    norm1_g = 1.0 + 0.02 * jax.random.normal(ks[2], (D_MODEL,), jnp.float32)
    w_in = jax.random.normal(ks[3], (D_MODEL, N_PROJ * D_CONV), jnp.float32) * D_MODEL ** -0.5
    conv_w = jax.random.normal(ks[4], (CONV_WIDTH, D_CONV), jnp.float32) * CONV_WIDTH ** -0.5
    ret_norm_g = 1.0 + 0.02 * jax.random.normal(ks[5], (D_RET,), jnp.float32)
    w_out = jax.random.normal(ks[6], (D_MIX, D_MODEL), jnp.float32) * D_MIX ** -0.5
    final_g = 1.0 + 0.02 * jax.random.normal(ks[7], (D_MODEL,), jnp.float32)
    return {"x": x, "meta": meta, "norm1_g": norm1_g, "w_in": w_in, "conv_w": conv_w,
            "ret_norm_g": ret_norm_g, "w_out": w_out, "final_g": final_g}


def reference(x, meta, norm1_g, w_in, conv_w, ret_norm_g, w_out, final_g):
    bsz = x.shape[0]
    meta_b = jnp.broadcast_to(meta.astype(x.dtype)[None], (bsz, N_META, D_MODEL))
    h = jnp.concatenate([meta_b, x], axis=1)
    pos = jnp.arange(h.shape[1], dtype=jnp.int32)
    for _ in range(DEPTH):
        h = mixer_layer(h, norm1_g, w_in, conv_w, ret_norm_g, w_out, pos)
    h = rms_norm(h, final_g)
    return h[:, N_META:]
```

```python
import functools

import numpy as np
import jax
import jax.numpy as jnp
from jax import lax
from jax.experimental import pallas as pl
from jax.experimental.pallas import tpu as pltpu

D_MODEL = 1024
N_META = 16
D_CONV = 512
D_RET = 512
RET_HEADS = 4
HEAD_DIM = 128
HALF = HEAD_DIM // 2
CONV_WIDTH = 3
ROPE_BASE = 10000.0
EPS = 1e-6

SEC = 512
CX, CB, CC, CG, Q, K, V, RG = range(8)

TOKENS_PER_STEP = 512
RET_CHUNK = 128
HIST_ROWS = 8
VMEM_LIMIT_BYTES = 56 * 1024 * 1024


def _gammas():
    return 1.0 - 2.0 ** (-5.0 - np.arange(RET_HEADS, dtype=np.float64))


def _rope_tables(n_pos):
    freqs = 1.0 / (ROPE_BASE ** (np.arange(HALF, dtype=np.float64) / HALF))
    ang = np.arange(n_pos, dtype=np.float64)[:, None] * freqs[None, :]
    cos = np.concatenate([np.cos(ang), np.cos(ang)], axis=1)
    sin = np.concatenate([-np.sin(ang), np.sin(ang)], axis=1)
    return cos.astype(np.float32), sin.astype(np.float32)


def _decay_tables(chunk):
    g = _gammas()
    idx = np.arange(chunk, dtype=np.float64)
    diff = idx[:, None] - idx[None, :]
    scale = HEAD_DIM ** -0.5
    decay = np.where(diff[None] >= 0, g[:, None, None] ** np.maximum(diff[None], 0.0), 0.0) * scale
    xi = (g[:, None] ** (idx[None, :] + 1.0)) * scale
    zeta = g[:, None] ** (chunk - 1.0 - idx[None, :])
    xi = np.broadcast_to(xi[:, :, None], (RET_HEADS, chunk, HEAD_DIM))
    zeta = np.broadcast_to(zeta[:, :, None], (RET_HEADS, chunk, HEAD_DIM))
    return decay.astype(np.float32), xi.astype(np.float32), zeta.astype(np.float32)


def _meta_zeta():
    g = _gammas()
    j = np.arange(N_META, dtype=np.float64)
    z = g[:, None] ** (N_META - 1.0 - j[None, :])
    return np.broadcast_to(z[:, :, None], (RET_HEADS, N_META, HEAD_DIM)).astype(np.float32)


def _rms_norm(x, g):
    return x * lax.rsqrt(jnp.mean(x * x, axis=-1, keepdims=True) + EPS) * g


def _silu(x):
    return x * (1.0 / (1.0 + jnp.exp(-x)))


def _rotary(t, cos, sin):
    return t * cos + pltpu.roll(t, HALF, axis=1) * sin


def _project(hn_bf16, win_ref, sec):
    return jnp.dot(hn_bf16, win_ref[:, sec * SEC:(sec + 1) * SEC], preferred_element_type=jnp.float32)


def _head(t, h):
    return t[:, h * HEAD_DIM:(h + 1) * HEAD_DIM]


def _mixer_kernel(x_ref, meta_ref, g1_ref, win_ref, convw_ref, retg_ref, wout_ref, gf_ref,
                  cos_ref, sin_ref, mcos_ref, msin_ref, decay_ref, xi_ref, zeta_ref, mzeta_ref,
                  out_ref,
                  state_ref, state0_ref, u_ref, hist0_ref, qkv_ref, mixed_ref,
                  *, tokens, chunk, chunk_decay):
    b = pl.program_id(0)
    j = pl.program_id(1)
    bf16 = jnp.bfloat16
    f32 = jnp.float32

    @pl.when(jnp.logical_and(b == 0, j == 0))
    def _meta_prefix():
        hm = _rms_norm(meta_ref[...], g1_ref[...]).astype(bf16)
        u_m = _project(hm, win_ref, CC) * _project(hm, win_ref, CX)
        hist0_ref[...] = u_m[N_META - HIST_ROWS:, :]
        k_m = _project(hm, win_ref, K)
        v_m = _project(hm, win_ref, V).astype(bf16)
        for h in range(RET_HEADS):
            kr = _rotary(_head(k_m, h), mcos_ref[...], msin_ref[...]) * mzeta_ref[h]
            state0_ref[h] = lax.dot_general(kr.astype(bf16), _head(v_m, h), (((0,), (0,)), ((), ())),
                                            preferred_element_type=f32)

    @pl.when(j == 0)
    def _start_sequence():
        state_ref[...] = state0_ref[...]
        u_ref[0:HIST_ROWS, :] = hist0_ref[...]

    x = x_ref[0]
    hn = _rms_norm(x, g1_ref[...]).astype(bf16)

    u = _project(hn, win_ref, CC) * _project(hn, win_ref, CX)
    u_ref[HIST_ROWS:HIST_ROWS + tokens, :] = u
    u1 = u_ref[HIST_ROWS - 1:HIST_ROWS - 1 + tokens, :]
    u2 = u_ref[HIST_ROWS - 2:HIST_ROWS - 2 + tokens, :]
    conv = convw_ref[0:1, :] * u2 + convw_ref[1:2, :] * u1 + convw_ref[2:3, :] * u
    conv_out = _project(hn, win_ref, CB) * conv * _silu(_project(hn, win_ref, CG))
    mixed_ref[:, 0:D_CONV] = conv_out.astype(bf16)
    u_ref[0:HIST_ROWS, :] = u_ref[tokens:tokens + HIST_ROWS, :]

    cos = cos_ref[...]
    sin = sin_ref[...]
    q = _project(hn, win_ref, Q)
    k = _project(hn, win_ref, K)
    for h in range(RET_HEADS):
        qkv_ref[0, :, h * HEAD_DIM:(h + 1) * HEAD_DIM] = _rotary(_head(q, h), cos, sin)
        qkv_ref[1, :, h * HEAD_DIM:(h + 1) * HEAD_DIM] = _rotary(_head(k, h), cos, sin)
    qkv_ref[2] = _project(hn, win_ref, V)
    gate = _silu(_project(hn, win_ref, RG))

    for c in range(tokens // chunk):
        rows = slice(c * chunk, (c + 1) * chunk)
        for h in range(RET_HEADS):
            cols = slice(h * HEAD_DIM, (h + 1) * HEAD_DIM)
            qh = qkv_ref[0, rows, cols].astype(bf16)
            kh = qkv_ref[1, rows, cols]
            vh = qkv_ref[2, rows, cols].astype(bf16)
            scores = lax.dot_general(qh, kh.astype(bf16), (((1,), (1,)), ((), ())),
                                     preferred_element_type=f32) * decay_ref[h]
            inner = jnp.dot(scores.astype(bf16), vh, preferred_element_type=f32)
            state = state_ref[h]
            cross = jnp.dot(qh, state.astype(bf16), preferred_element_type=f32) * xi_ref[h]
            o = inner + cross
            upd = lax.dot_general((kh * zeta_ref[h]).astype(bf16), vh, (((0,), (0,)), ((), ())),
                                  preferred_element_type=f32)
            state_ref[h] = chunk_decay[h] * state + upd

            mu = jnp.mean(o, axis=-1, keepdims=True)
            d = o - mu
            var = jnp.mean(d * d, axis=-1, keepdims=True)
            y = d * lax.rsqrt(var + EPS) * retg_ref[:, cols]
            mixed_ref[rows, D_CONV + h * HEAD_DIM:D_CONV + (h + 1) * HEAD_DIM] = (
                y * gate[rows, cols]).astype(bf16)

    y = jnp.dot(mixed_ref[...], wout_ref[...], preferred_element_type=f32) + x
    out_ref[0] = _rms_norm(y, gf_ref[...])


def kernel(x, meta, norm1_g, w_in, conv_w, ret_norm_g, w_out, final_g):
    bsz, seq, d_model = x.shape
    assert d_model == D_MODEL and meta.shape == (N_META, D_MODEL)
    tokens, chunk = TOKENS_PER_STEP, RET_CHUNK
    assert seq % tokens == 0 and tokens % chunk == 0

    cos, sin = _rope_tables(N_META + seq)
    decay, xi, zeta = _decay_tables(chunk)
    chunk_decay = tuple(float(v) for v in _gammas() ** chunk)

    const = lambda *shape: pl.BlockSpec(shape, lambda b, j: (0,) * len(shape))
    grid = (bsz, seq // tokens)
    body = functools.partial(_mixer_kernel, tokens=tokens, chunk=chunk, chunk_decay=chunk_decay)
    return pl.pallas_call(
        body,
        grid=grid,
        in_specs=[
            pl.BlockSpec((1, tokens, D_MODEL), lambda b, j: (b, j, 0)),
            const(N_META, D_MODEL),
            const(1, D_MODEL),
            const(D_MODEL, 8 * SEC),
            const(CONV_WIDTH, D_CONV),
            const(1, D_RET),
            const(D_MODEL, D_MODEL),
            const(1, D_MODEL),
            pl.BlockSpec((tokens, HEAD_DIM), lambda b, j: (j, 0)),
            pl.BlockSpec((tokens, HEAD_DIM), lambda b, j: (j, 0)),
            const(N_META, HEAD_DIM),
            const(N_META, HEAD_DIM),
            const(RET_HEADS, chunk, chunk),
            const(RET_HEADS, chunk, HEAD_DIM),
            const(RET_HEADS, chunk, HEAD_DIM),
            const(RET_HEADS, N_META, HEAD_DIM),
        ],
        out_specs=pl.BlockSpec((1, tokens, D_MODEL), lambda b, j: (b, j, 0)),
        out_shape=jax.ShapeDtypeStruct((bsz, seq, D_MODEL), x.dtype),
        scratch_shapes=[
            pltpu.VMEM((RET_HEADS, HEAD_DIM, HEAD_DIM), jnp.float32),
            pltpu.VMEM((RET_HEADS, HEAD_DIM, HEAD_DIM), jnp.float32),
            pltpu.VMEM((HIST_ROWS + tokens, D_CONV), jnp.float32),
            pltpu.VMEM((HIST_ROWS, D_CONV), jnp.float32),
            pltpu.VMEM((3, tokens, D_RET), jnp.float32),
            pltpu.VMEM((tokens, D_MODEL), jnp.bfloat16),
        ],
        compiler_params=pltpu.CompilerParams(
            dimension_semantics=("arbitrary", "arbitrary"),
            vmem_limit_bytes=VMEM_LIMIT_BYTES),
        name="hymba_mixer",
    )(x, meta, norm1_g.reshape(1, -1), w_in.astype(jnp.bfloat16), conv_w,
      ret_norm_g.reshape(1, -1), w_out.astype(jnp.bfloat16), final_g.reshape(1, -1),
      jnp.asarray(cos[N_META:]), jnp.asarray(sin[N_META:]),
      jnp.asarray(cos[:N_META]), jnp.asarray(sin[:N_META]),
      jnp.asarray(decay), jnp.asarray(xi), jnp.asarray(zeta), jnp.asarray(_meta_zeta()))
```

```python
import functools

import numpy as np
import jax
import jax.numpy as jnp
from jax import lax
from jax.experimental import pallas as pl
from jax.experimental.pallas import tpu as pltpu

D_MODEL = 1024
N_META = 16
D_CONV = 512
D_RET = 512
RET_HEADS = 4
HEAD_DIM = 128
HALF = HEAD_DIM // 2
CONV_WIDTH = 3
ROPE_BASE = 10000.0
EPS = 1e-6

SEC = 512
CX, CB, CC, CG, Q, K, V, RG = range(8)

TOKENS_PER_STEP = 512
RET_CHUNK = 128
HIST_ROWS = 8
VMEM_LIMIT_BYTES = 56 * 1024 * 1024


def _gammas():
    return 1.0 - 2.0 ** (-5.0 - np.arange(RET_HEADS, dtype=np.float64))


def _rope_tables(n_pos):
    freqs = 1.0 / (ROPE_BASE ** (np.arange(HALF, dtype=np.float64) / HALF))
    ang = np.arange(n_pos, dtype=np.float64)[:, None] * freqs[None, :]
    cos = np.concatenate([np.cos(ang), np.cos(ang)], axis=1)
    sin = np.concatenate([-np.sin(ang), np.sin(ang)], axis=1)
    return cos.astype(np.float32), sin.astype(np.float32)


def _decay_tables(chunk):
    g = _gammas()
    idx = np.arange(chunk, dtype=np.float64)
    diff = idx[:, None] - idx[None, :]
    scale = HEAD_DIM ** -0.5
    decay = np.where(diff[None] >= 0, g[:, None, None] ** np.maximum(diff[None], 0.0), 0.0) * scale
    xi = (g[:, None] ** (idx[None, :] + 1.0)) * scale
    zeta = g[:, None] ** (chunk - 1.0 - idx[None, :])
    xi = np.broadcast_to(xi[:, :, None], (RET_HEADS, chunk, HEAD_DIM))
    zeta = np.broadcast_to(zeta[:, :, None], (RET_HEADS, chunk, HEAD_DIM))
    return decay.astype(np.float32), xi.astype(np.float32), zeta.astype(np.float32)


def _meta_zeta():
    g = _gammas()
    j = np.arange(N_META, dtype=np.float64)
    z = g[:, None] ** (N_META - 1.0 - j[None, :])
    return np.broadcast_to(z[:, :, None], (RET_HEADS, N_META, HEAD_DIM)).astype(np.float32)


def _rms_norm(x, g):
    return x * lax.rsqrt(jnp.mean(x * x, axis=-1, keepdims=True) + EPS) * g


def _silu(x):
    return x * (1.0 / (1.0 + jnp.exp(-x)))


def _rotary(t, cos, sin):
    return t * cos + pltpu.roll(t, HALF, axis=1) * sin


def _project(hn_bf16, win_ref, sec):
    return jnp.dot(hn_bf16, win_ref[:, sec * SEC:(sec + 1) * SEC], preferred_element_type=jnp.float32)


def _head(t, h):
    return t[:, h * HEAD_DIM:(h + 1) * HEAD_DIM]


def _mixer_kernel(x_ref, meta_ref, g1_ref, win_ref, convw_ref, retg_ref, wout_ref, gf_ref,
                  cos_ref, sin_ref, mcos_ref, msin_ref, decay_ref, xi_ref, zeta_ref, mzeta_ref,
                  out_ref,
                  state_ref, state0_ref, u_ref, hist0_ref, q_ref, qx_ref, k_ref, kz_ref, v_ref,
                  p_ref, sbf_ref, mixed_ref,
                  *, tokens, chunk, chunk_decay):
    b = pl.program_id(0)
    j = pl.program_id(1)
    bf16 = jnp.bfloat16
    f32 = jnp.float32

    @pl.when(jnp.logical_and(b == 0, j == 0))
    def _meta_prefix():
        hm = _rms_norm(meta_ref[...], g1_ref[...]).astype(bf16)
        u_m = _project(hm, win_ref, CC) * _project(hm, win_ref, CX)
        hist0_ref[...] = u_m[N_META - HIST_ROWS:, :]
        k_m = _project(hm, win_ref, K)
        v_m = _project(hm, win_ref, V).astype(bf16)
        for h in range(RET_HEADS):
            kr = _rotary(_head(k_m, h), mcos_ref[...], msin_ref[...]) * mzeta_ref[h]
            state0_ref[h] = lax.dot_general(kr.astype(bf16), _head(v_m, h), (((0,), (0,)), ((), ())),
                                            preferred_element_type=f32)

    @pl.when(j == 0)
    def _start_sequence():
        state_ref[...] = state0_ref[...]
        u_ref[0:HIST_ROWS, :] = hist0_ref[...]

    x = x_ref[0]
    hn = _rms_norm(x, g1_ref[...]).astype(bf16)

    n_chunks = tokens // chunk

    cos = cos_ref[...]
    sin = sin_ref[...]
    q = _project(hn, win_ref, Q)
    k = _project(hn, win_ref, K)
    for h in range(RET_HEADS):
        cols = slice(h * HEAD_DIM, (h + 1) * HEAD_DIM)
        qr = _rotary(_head(q, h), cos, sin)
        kr = _rotary(_head(k, h), cos, sin)
        q_ref[:, cols] = qr.astype(bf16)
        k_ref[:, cols] = kr.astype(bf16)
        for c in range(n_chunks):
            rows = slice(c * chunk, (c + 1) * chunk)
            qx_ref[rows, cols] = (qr[rows] * xi_ref[h]).astype(bf16)
            kz_ref[rows, cols] = (kr[rows] * zeta_ref[h]).astype(bf16)
    v_ref[...] = _project(hn, win_ref, V).astype(bf16)

    upd = {}
    for c in range(n_chunks):
        rows = slice(c * chunk, (c + 1) * chunk)
        for h in range(RET_HEADS):
            cols = slice(h * HEAD_DIM, (h + 1) * HEAD_DIM)
            scores = lax.dot_general(q_ref[rows, cols], k_ref[rows, cols], (((1,), (1,)), ((), ())),
                                     preferred_element_type=f32)
            p_ref[c * RET_HEADS + h] = (scores * decay_ref[h]).astype(bf16)
            upd[c, h] = lax.dot_general(kz_ref[rows, cols], v_ref[rows, cols], (((0,), (0,)), ((), ())),
                                        preferred_element_type=f32)
    for h in range(RET_HEADS):
        state = state_ref[h]
        for c in range(n_chunks):
            sbf_ref[c * RET_HEADS + h] = state.astype(bf16)
            state = chunk_decay[h] * state + upd[c, h]
        state_ref[h] = state

    u = _project(hn, win_ref, CC) * _project(hn, win_ref, CX)
    u_ref[HIST_ROWS:HIST_ROWS + tokens, :] = u
    u1 = u_ref[HIST_ROWS - 1:HIST_ROWS - 1 + tokens, :]
    u2 = u_ref[HIST_ROWS - 2:HIST_ROWS - 2 + tokens, :]
    conv = convw_ref[0:1, :] * u2 + convw_ref[1:2, :] * u1 + convw_ref[2:3, :] * u
    conv_out = _project(hn, win_ref, CB) * conv * _silu(_project(hn, win_ref, CG))
    mixed_ref[:, 0:D_CONV] = conv_out.astype(bf16)
    u_ref[0:HIST_ROWS, :] = u_ref[tokens:tokens + HIST_ROWS, :]

    gate = _silu(_project(hn, win_ref, RG))
    for c in range(n_chunks):
        rows = slice(c * chunk, (c + 1) * chunk)
        for h in range(RET_HEADS):
            cols = slice(h * HEAD_DIM, (h + 1) * HEAD_DIM)
            hc = c * RET_HEADS + h
            lhs = jnp.concatenate([p_ref[hc], qx_ref[rows, cols]], axis=1)
            rhs = jnp.concatenate([v_ref[rows, cols], sbf_ref[hc]], axis=0)
            o = jnp.dot(lhs, rhs, preferred_element_type=f32)
            mu = jnp.mean(o, axis=-1, keepdims=True)
            d = o - mu
            var = jnp.mean(d * d, axis=-1, keepdims=True)
            y = d * lax.rsqrt(var + EPS) * retg_ref[:, cols]
            mixed_ref[rows, D_CONV + h * HEAD_DIM:D_CONV + (h + 1) * HEAD_DIM] = (
                y * gate[rows, cols]).astype(bf16)

    y = jnp.dot(mixed_ref[...], wout_ref[...], preferred_element_type=f32) + x
    out_ref[0] = _rms_norm(y, gf_ref[...])


def kernel(x, meta, norm1_g, w_in, conv_w, ret_norm_g, w_out, final_g):
    bsz, seq, d_model = x.shape
    assert d_model == D_MODEL and meta.shape == (N_META, D_MODEL)
    tokens, chunk = TOKENS_PER_STEP, RET_CHUNK
    assert seq % tokens == 0 and tokens % chunk == 0

    cos, sin = _rope_tables(N_META + seq)
    decay, xi, zeta = _decay_tables(chunk)
    chunk_decay = tuple(float(v) for v in _gammas() ** chunk)
    n_hc = (tokens // chunk) * RET_HEADS

    const = lambda *shape: pl.BlockSpec(shape, lambda b, j: (0,) * len(shape))
    grid = (bsz, seq // tokens)
    body = functools.partial(_mixer_kernel, tokens=tokens, chunk=chunk, chunk_decay=chunk_decay)
    return pl.pallas_call(
        body,
        grid=grid,
        in_specs=[
            pl.BlockSpec((1, tokens, D_MODEL), lambda b, j: (b, j, 0)),
            const(N_META, D_MODEL),
            const(1, D_MODEL),
            const(D_MODEL, 8 * SEC),
            const(CONV_WIDTH, D_CONV),
            const(1, D_RET),
            const(D_MODEL, D_MODEL),
            const(1, D_MODEL),
            pl.BlockSpec((tokens, HEAD_DIM), lambda b, j: (j, 0)),
            pl.BlockSpec((tokens, HEAD_DIM), lambda b, j: (j, 0)),
            const(N_META, HEAD_DIM),
            const(N_META, HEAD_DIM),
            const(RET_HEADS, chunk, chunk),
            const(RET_HEADS, chunk, HEAD_DIM),
            const(RET_HEADS, chunk, HEAD_DIM),
            const(RET_HEADS, N_META, HEAD_DIM),
        ],
        out_specs=pl.BlockSpec((1, tokens, D_MODEL), lambda b, j: (b, j, 0)),
        out_shape=jax.ShapeDtypeStruct((bsz, seq, D_MODEL), x.dtype),
        scratch_shapes=[
            pltpu.VMEM((RET_HEADS, HEAD_DIM, HEAD_DIM), jnp.float32),
            pltpu.VMEM((RET_HEADS, HEAD_DIM, HEAD_DIM), jnp.float32),
            pltpu.VMEM((HIST_ROWS + tokens, D_CONV), jnp.float32),
            pltpu.VMEM((HIST_ROWS, D_CONV), jnp.float32),
            pltpu.VMEM((tokens, D_RET), jnp.bfloat16),
            pltpu.VMEM((tokens, D_RET), jnp.bfloat16),
            pltpu.VMEM((tokens, D_RET), jnp.bfloat16),
            pltpu.VMEM((tokens, D_RET), jnp.bfloat16),
            pltpu.VMEM((tokens, D_RET), jnp.bfloat16),
            pltpu.VMEM((n_hc, chunk, chunk), jnp.bfloat16),
            pltpu.VMEM((n_hc, HEAD_DIM, HEAD_DIM), jnp.bfloat16),
            pltpu.VMEM((tokens, D_MODEL), jnp.bfloat16),
        ],
        compiler_params=pltpu.CompilerParams(
            dimension_semantics=("arbitrary", "arbitrary"),
            vmem_limit_bytes=VMEM_LIMIT_BYTES),
        name="hymba_mixer",
    )(x, meta, norm1_g.reshape(1, -1), w_in.astype(jnp.bfloat16), conv_w,
      ret_norm_g.reshape(1, -1), w_out.astype(jnp.bfloat16), final_g.reshape(1, -1),
      jnp.asarray(cos[N_META:]), jnp.asarray(sin[N_META:]),
      jnp.asarray(cos[:N_META]), jnp.asarray(sin[:N_META]),
      jnp.asarray(decay), jnp.asarray(xi), jnp.asarray(zeta), jnp.asarray(_meta_zeta()))
```

```python
import functools

import numpy as np
import jax
import jax.numpy as jnp
from jax import lax
from jax.experimental import pallas as pl
from jax.experimental.pallas import tpu as pltpu

D_MODEL = 1024
N_META = 16
D_CONV = 512
D_RET = 512
RET_HEADS = 4
HEAD_DIM = 128
HALF = HEAD_DIM // 2
CONV_WIDTH = 3
ROPE_BASE = 10000.0
EPS = 1e-6

SEC = 512
CX, CB, CC, CG, Q, K, V, RG = range(8)

TOKENS_PER_STEP = 1024
RET_CHUNK = 128
HIST_ROWS = 8
VMEM_LIMIT_BYTES = 56 * 1024 * 1024


def _gammas():
    return 1.0 - 2.0 ** (-5.0 - np.arange(RET_HEADS, dtype=np.float64))


def _rope_tables(n_pos):
    freqs = 1.0 / (ROPE_BASE ** (np.arange(HALF, dtype=np.float64) / HALF))
    ang = np.arange(n_pos, dtype=np.float64)[:, None] * freqs[None, :]
    cos = np.concatenate([np.cos(ang), np.cos(ang)], axis=1)
    sin = np.concatenate([-np.sin(ang), np.sin(ang)], axis=1)
    return cos.astype(np.float32), sin.astype(np.float32)


def _decay_tables(chunk):
    g = _gammas()
    idx = np.arange(chunk, dtype=np.float64)
    diff = idx[:, None] - idx[None, :]
    scale = HEAD_DIM ** -0.5
    decay = np.where(diff[None] >= 0, g[:, None, None] ** np.maximum(diff[None], 0.0), 0.0) * scale
    xi = (g[:, None] ** (idx[None, :] + 1.0)) * scale
    zeta = g[:, None] ** (chunk - 1.0 - idx[None, :])
    xi = np.broadcast_to(xi[:, :, None], (RET_HEADS, chunk, HEAD_DIM))
    zeta = np.broadcast_to(zeta[:, :, None], (RET_HEADS, chunk, HEAD_DIM))
    return decay.astype(np.float32), xi.astype(np.float32), zeta.astype(np.float32)


def _meta_zeta():
    g = _gammas()
    j = np.arange(N_META, dtype=np.float64)
    z = g[:, None] ** (N_META - 1.0 - j[None, :])
    return np.broadcast_to(z[:, :, None], (RET_HEADS, N_META, HEAD_DIM)).astype(np.float32)


def _rms_norm(x, g):
    return x * lax.rsqrt(jnp.mean(x * x, axis=-1, keepdims=True) + EPS) * g


def _silu(x):
    return x * (1.0 / (1.0 + jnp.exp(-x)))


def _rotary(t, cos, sin):
    return t * cos + pltpu.roll(t, HALF, axis=1) * sin


def _project(hn_bf16, win_ref, sec):
    return jnp.dot(hn_bf16, win_ref[:, sec * SEC:(sec + 1) * SEC], preferred_element_type=jnp.float32)


def _head(t, h):
    return t[:, h * HEAD_DIM:(h + 1) * HEAD_DIM]


def _mixer_kernel(x_ref, meta_ref, g1_ref, win_ref, convw_ref, retg_ref, wout_ref, gf_ref,
                  cos_ref, sin_ref, mcos_ref, msin_ref, decay_ref, xi_ref, zeta_ref, mzeta_ref,
                  out_ref,
                  state_ref, state0_ref, u_ref, hist0_ref, q_ref, qx_ref, k_ref, kz_ref, v_ref,
                  p_ref, sbf_ref, mixed_ref,
                  *, tokens, chunk, chunk_decay):
    b = pl.program_id(0)
    j = pl.program_id(1)
    bf16 = jnp.bfloat16
    f32 = jnp.float32

    @pl.when(jnp.logical_and(b == 0, j == 0))
    def _meta_prefix():
        hm = _rms_norm(meta_ref[...], g1_ref[...]).astype(bf16)
        u_m = _project(hm, win_ref, CC) * _project(hm, win_ref, CX)
        hist0_ref[...] = u_m[N_META - HIST_ROWS:, :]
        k_m = _project(hm, win_ref, K)
        v_m = _project(hm, win_ref, V).astype(bf16)
        for h in range(RET_HEADS):
            kr = _rotary(_head(k_m, h), mcos_ref[...], msin_ref[...]) * mzeta_ref[h]
            state0_ref[h] = lax.dot_general(kr.astype(bf16), _head(v_m, h), (((0,), (0,)), ((), ())),
                                            preferred_element_type=f32)

    @pl.when(j == 0)
    def _start_sequence():
        state_ref[...] = state0_ref[...]
        u_ref[0:HIST_ROWS, :] = hist0_ref[...]

    x = x_ref[0]
    hn = _rms_norm(x, g1_ref[...]).astype(bf16)

    n_chunks = tokens // chunk

    cos = cos_ref[...]
    sin = sin_ref[...]
    q = _project(hn, win_ref, Q)
    k = _project(hn, win_ref, K)
    for h in range(RET_HEADS):
        cols = slice(h * HEAD_DIM, (h + 1) * HEAD_DIM)
        qr = _rotary(_head(q, h), cos, sin)
        kr = _rotary(_head(k, h), cos, sin)
        q_ref[:, cols] = qr.astype(bf16)
        k_ref[:, cols] = kr.astype(bf16)
        for c in range(n_chunks):
            rows = slice(c * chunk, (c + 1) * chunk)
            qx_ref[rows, cols] = (qr[rows] * xi_ref[h]).astype(bf16)
            kz_ref[rows, cols] = (kr[rows] * zeta_ref[h]).astype(bf16)
    v_ref[...] = _project(hn, win_ref, V).astype(bf16)

    upd = {}
    for c in range(n_chunks):
        rows = slice(c * chunk, (c + 1) * chunk)
        for h in range(RET_HEADS):
            cols = slice(h * HEAD_DIM, (h + 1) * HEAD_DIM)
            scores = lax.dot_general(q_ref[rows, cols], k_ref[rows, cols], (((1,), (1,)), ((), ())),
                                     preferred_element_type=f32)
            p_ref[c * RET_HEADS + h] = (scores * decay_ref[h]).astype(bf16)
            upd[c, h] = lax.dot_general(kz_ref[rows, cols], v_ref[rows, cols], (((0,), (0,)), ((), ())),
                                        preferred_element_type=f32)
    for h in range(RET_HEADS):
        state = state_ref[h]
        for c in range(n_chunks):
            sbf_ref[c * RET_HEADS + h] = state.astype(bf16)
            state = chunk_decay[h] * state + upd[c, h]
        state_ref[h] = state

    u = _project(hn, win_ref, CC) * _project(hn, win_ref, CX)
    u_ref[HIST_ROWS:HIST_ROWS + tokens, :] = u
    u1 = u_ref[HIST_ROWS - 1:HIST_ROWS - 1 + tokens, :]
    u2 = u_ref[HIST_ROWS - 2:HIST_ROWS - 2 + tokens, :]
    conv = convw_ref[0:1, :] * u2 + convw_ref[1:2, :] * u1 + convw_ref[2:3, :] * u
    conv_out = _project(hn, win_ref, CB) * conv * _silu(_project(hn, win_ref, CG))
    mixed_ref[:, 0:D_CONV] = conv_out.astype(bf16)
    u_ref[0:HIST_ROWS, :] = u_ref[tokens:tokens + HIST_ROWS, :]

    gate = _silu(_project(hn, win_ref, RG))
    for c in range(n_chunks):
        rows = slice(c * chunk, (c + 1) * chunk)
        for h in range(RET_HEADS):
            cols = slice(h * HEAD_DIM, (h + 1) * HEAD_DIM)
            hc = c * RET_HEADS + h
            lhs = jnp.concatenate([p_ref[hc], qx_ref[rows, cols]], axis=1)
            rhs = jnp.concatenate([v_ref[rows, cols], sbf_ref[hc]], axis=0)
            o = jnp.dot(lhs, rhs, preferred_element_type=f32)
            mu = jnp.mean(o, axis=-1, keepdims=True)
            d = o - mu
            var = jnp.mean(d * d, axis=-1, keepdims=True)
            y = d * lax.rsqrt(var + EPS) * retg_ref[:, cols]
            mixed_ref[rows, D_CONV + h * HEAD_DIM:D_CONV + (h + 1) * HEAD_DIM] = (
                y * gate[rows, cols]).astype(bf16)

    y = jnp.dot(mixed_ref[...], wout_ref[...], preferred_element_type=f32) + x
    out_ref[0] = _rms_norm(y, gf_ref[...])


def kernel(x, meta, norm1_g, w_in, conv_w, ret_norm_g, w_out, final_g):
    bsz, seq, d_model = x.shape
    assert d_model == D_MODEL and meta.shape == (N_META, D_MODEL)
    tokens, chunk = TOKENS_PER_STEP, RET_CHUNK
    assert seq % tokens == 0 and tokens % chunk == 0

    cos, sin = _rope_tables(N_META + seq)
    decay, xi, zeta = _decay_tables(chunk)
    chunk_decay = tuple(float(v) for v in _gammas() ** chunk)
    n_hc = (tokens // chunk) * RET_HEADS

    const = lambda *shape: pl.BlockSpec(shape, lambda b, j: (0,) * len(shape))
    grid = (bsz, seq // tokens)
    body = functools.partial(_mixer_kernel, tokens=tokens, chunk=chunk, chunk_decay=chunk_decay)
    return pl.pallas_call(
        body,
        grid=grid,
        in_specs=[
            pl.BlockSpec((1, tokens, D_MODEL), lambda b, j: (b, j, 0)),
            const(N_META, D_MODEL),
            const(1, D_MODEL),
            const(D_MODEL, 8 * SEC),
            const(CONV_WIDTH, D_CONV),
            const(1, D_RET),
            const(D_MODEL, D_MODEL),
            const(1, D_MODEL),
            pl.BlockSpec((tokens, HEAD_DIM), lambda b, j: (j, 0)),
            pl.BlockSpec((tokens, HEAD_DIM), lambda b, j: (j, 0)),
            const(N_META, HEAD_DIM),
            const(N_META, HEAD_DIM),
            const(RET_HEADS, chunk, chunk),
            const(RET_HEADS, chunk, HEAD_DIM),
            const(RET_HEADS, chunk, HEAD_DIM),
            const(RET_HEADS, N_META, HEAD_DIM),
        ],
        out_specs=pl.BlockSpec((1, tokens, D_MODEL), lambda b, j: (b, j, 0)),
        out_shape=jax.ShapeDtypeStruct((bsz, seq, D_MODEL), x.dtype),
        scratch_shapes=[
            pltpu.VMEM((RET_HEADS, HEAD_DIM, HEAD_DIM), jnp.float32),
            pltpu.VMEM((RET_HEADS, HEAD_DIM, HEAD_DIM), jnp.float32),
            pltpu.VMEM((HIST_ROWS + tokens, D_CONV), jnp.float32),
            pltpu.VMEM((HIST_ROWS, D_CONV), jnp.float32),
            pltpu.VMEM((tokens, D_RET), jnp.bfloat16),
            pltpu.VMEM((tokens, D_RET), jnp.bfloat16),
            pltpu.VMEM((tokens, D_RET), jnp.bfloat16),
            pltpu.VMEM((tokens, D_RET), jnp.bfloat16),
            pltpu.VMEM((tokens, D_RET), jnp.bfloat16),
            pltpu.VMEM((n_hc, chunk, chunk), jnp.bfloat16),
            pltpu.VMEM((n_hc, HEAD_DIM, HEAD_DIM), jnp.bfloat16),
            pltpu.VMEM((tokens, D_MODEL), jnp.bfloat16),
        ],
        compiler_params=pltpu.CompilerParams(
            dimension_semantics=("arbitrary", "arbitrary"),
            vmem_limit_bytes=VMEM_LIMIT_BYTES),
        name="hymba_mixer",
    )(x, meta, norm1_g.reshape(1, -1), w_in.astype(jnp.bfloat16), conv_w,
      ret_norm_g.reshape(1, -1), w_out.astype(jnp.bfloat16), final_g.reshape(1, -1),
      jnp.asarray(cos[N_META:]), jnp.asarray(sin[N_META:]),
      jnp.asarray(cos[:N_META]), jnp.asarray(sin[:N_META]),
      jnp.asarray(decay), jnp.asarray(xi), jnp.asarray(zeta), jnp.asarray(_meta_zeta()))
```

```python
import functools

import numpy as np
import jax
import jax.numpy as jnp
from jax import lax
from jax.experimental import pallas as pl
from jax.experimental.pallas import tpu as pltpu

D_MODEL = 1024
N_META = 16
D_CONV = 512
D_RET = 512
RET_HEADS = 4
HEAD_DIM = 128
HALF = HEAD_DIM // 2
CONV_WIDTH = 3
ROPE_BASE = 10000.0
EPS = 1e-6

SEC = 512
CX, CB, CC, CG, Q, K, V, RG = range(8)

TOKENS_PER_STEP = 1024
RET_CHUNK = 128
HIST_ROWS = 8
STAGE_ROWS = 64
STAGE_SLOTS = 4
VMEM_LIMIT_BYTES = 56 * 1024 * 1024


def _gammas():
    return 1.0 - 2.0 ** (-5.0 - np.arange(RET_HEADS, dtype=np.float64))


def _rope_tables(n_pos):
    freqs = 1.0 / (ROPE_BASE ** (np.arange(HALF, dtype=np.float64) / HALF))
    ang = np.arange(n_pos, dtype=np.float64)[:, None] * freqs[None, :]
    cos = np.concatenate([np.cos(ang), np.cos(ang)], axis=1)
    sin = np.concatenate([-np.sin(ang), np.sin(ang)], axis=1)
    return cos.astype(np.float32), sin.astype(np.float32)


def _decay_tables(chunk):
    g = _gammas()
    idx = np.arange(chunk, dtype=np.float64)
    diff = idx[:, None] - idx[None, :]
    scale = HEAD_DIM ** -0.5
    decay = np.where(diff[None] >= 0, g[:, None, None] ** np.maximum(diff[None], 0.0), 0.0) * scale
    xi = (g[:, None] ** (idx[None, :] + 1.0)) * scale
    zeta = g[:, None] ** (chunk - 1.0 - idx[None, :])
    xi = np.broadcast_to(xi[:, :, None], (RET_HEADS, chunk, HEAD_DIM))
    zeta = np.broadcast_to(zeta[:, :, None], (RET_HEADS, chunk, HEAD_DIM))
    return decay.astype(np.float32), xi.astype(np.float32), zeta.astype(np.float32)


def _meta_zeta():
    g = _gammas()
    j = np.arange(N_META, dtype=np.float64)
    z = g[:, None] ** (N_META - 1.0 - j[None, :])
    return np.broadcast_to(z[:, :, None], (RET_HEADS, N_META, HEAD_DIM)).astype(np.float32)


def _rms_norm(x, g):
    return x * lax.rsqrt(jnp.mean(x * x, axis=-1, keepdims=True) + EPS) * g


def _silu(x):
    return x * (1.0 / (1.0 + jnp.exp(-x)))


def _rotary(t, cos, sin):
    return t * cos + pltpu.roll(t, HALF, axis=1) * sin


def _project(hn_bf16, win_ref, sec):
    return jnp.dot(hn_bf16, win_ref[:, sec * SEC:(sec + 1) * SEC], preferred_element_type=jnp.float32)


def _head(t, h):
    return t[:, h * HEAD_DIM:(h + 1) * HEAD_DIM]


def _stage_copy(w_hbm, stage_ref, sem_ref, chunk_idx):
    slot = chunk_idx % STAGE_SLOTS
    rows = pl.ds(chunk_idx * STAGE_ROWS, STAGE_ROWS)
    return pltpu.make_async_copy(w_hbm.at[rows, :], stage_ref.at[slot, :, 0:w_hbm.shape[1]], sem_ref.at[slot])


def _load_weight_as_bf16(w_hbm, w_bf16_ref, stage_ref, sem_ref):
    n_rows, n_cols = w_hbm.shape
    n_chunks = n_rows // STAGE_ROWS
    for c in range(min(STAGE_SLOTS, n_chunks)):
        _stage_copy(w_hbm, stage_ref, sem_ref, c).start()
    for c in range(n_chunks):
        _stage_copy(w_hbm, stage_ref, sem_ref, c).wait()
        w_bf16_ref[c * STAGE_ROWS:(c + 1) * STAGE_ROWS, :] = (
            stage_ref[c % STAGE_SLOTS, :, 0:n_cols].astype(jnp.bfloat16))
        if c + STAGE_SLOTS < n_chunks:
            _stage_copy(w_hbm, stage_ref, sem_ref, c + STAGE_SLOTS).start()


def _mixer_kernel(x_ref, meta_ref, g1_ref, win_hbm, convw_ref, retg_ref, wout_hbm, gf_ref,
                  cos_ref, sin_ref, mcos_ref, msin_ref, decay_ref, xi_ref, zeta_ref, mzeta_ref,
                  out_ref,
                  win_ref, wout_ref, stage_ref, stage_sem,
                  state_ref, state0_ref, u_ref, hist0_ref, q_ref, qx_ref, k_ref, kz_ref, v_ref,
                  p_ref, sbf_ref, mixed_ref,
                  *, tokens, chunk, chunk_decay):
    b = pl.program_id(0)
    j = pl.program_id(1)
    bf16 = jnp.bfloat16
    f32 = jnp.float32

    @pl.when(jnp.logical_and(b == 0, j == 0))
    def _first_step():
        _load_weight_as_bf16(win_hbm, win_ref, stage_ref, stage_sem)
        _load_weight_as_bf16(wout_hbm, wout_ref, stage_ref, stage_sem)
        hm = _rms_norm(meta_ref[...], g1_ref[...]).astype(bf16)
        u_m = _project(hm, win_ref, CC) * _project(hm, win_ref, CX)
        hist0_ref[...] = u_m[N_META - HIST_ROWS:, :]
        k_m = _project(hm, win_ref, K)
        v_m = _project(hm, win_ref, V).astype(bf16)
        for h in range(RET_HEADS):
            kr = _rotary(_head(k_m, h), mcos_ref[...], msin_ref[...]) * mzeta_ref[h]
            state0_ref[h] = lax.dot_general(kr.astype(bf16), _head(v_m, h), (((0,), (0,)), ((), ())),
                                            preferred_element_type=f32)

    @pl.when(j == 0)
    def _start_sequence():
        state_ref[...] = state0_ref[...]
        u_ref[0:HIST_ROWS, :] = hist0_ref[...]

    x = x_ref[0]
    hn = _rms_norm(x, g1_ref[...]).astype(bf16)

    n_chunks = tokens // chunk

    cos = cos_ref[...]
    sin = sin_ref[...]
    q = _project(hn, win_ref, Q)
    k = _project(hn, win_ref, K)
    for h in range(RET_HEADS):
        cols = slice(h * HEAD_DIM, (h + 1) * HEAD_DIM)
        qr = _rotary(_head(q, h), cos, sin)
        kr = _rotary(_head(k, h), cos, sin)
        q_ref[:, cols] = qr.astype(bf16)
        k_ref[:, cols] = kr.astype(bf16)
        for c in range(n_chunks):
            rows = slice(c * chunk, (c + 1) * chunk)
            qx_ref[rows, cols] = (qr[rows] * xi_ref[h]).astype(bf16)
            kz_ref[rows, cols] = (kr[rows] * zeta_ref[h]).astype(bf16)
    v_ref[...] = _project(hn, win_ref, V).astype(bf16)

    upd = {}
    for c in range(n_chunks):
        rows = slice(c * chunk, (c + 1) * chunk)
        for h in range(RET_HEADS):
            cols = slice(h * HEAD_DIM, (h + 1) * HEAD_DIM)
            scores = lax.dot_general(q_ref[rows, cols], k_ref[rows, cols], (((1,), (1,)), ((), ())),
                                     preferred_element_type=f32)
            p_ref[c * RET_HEADS + h] = (scores * decay_ref[h]).astype(bf16)
            upd[c, h] = lax.dot_general(kz_ref[rows, cols], v_ref[rows, cols], (((0,), (0,)), ((), ())),
                                        preferred_element_type=f32)
    for h in range(RET_HEADS):
        state = state_ref[h]
        for c in range(n_chunks):
            sbf_ref[c * RET_HEADS + h] = state.astype(bf16)
            state = chunk_decay[h] * state + upd[c, h]
        state_ref[h] = state

    u = _project(hn, win_ref, CC) * _project(hn, win_ref, CX)
    u_ref[HIST_ROWS:HIST_ROWS + tokens, :] = u
    u1 = u_ref[HIST_ROWS - 1:HIST_ROWS - 1 + tokens, :]
    u2 = u_ref[HIST_ROWS - 2:HIST_ROWS - 2 + tokens, :]
    conv = convw_ref[0:1, :] * u2 + convw_ref[1:2, :] * u1 + convw_ref[2:3, :] * u
    conv_out = _project(hn, win_ref, CB) * conv * _silu(_project(hn, win_ref, CG))
    mixed_ref[:, 0:D_CONV] = conv_out.astype(bf16)
    u_ref[0:HIST_ROWS, :] = u_ref[tokens:tokens + HIST_ROWS, :]

    gate = _silu(_project(hn, win_ref, RG))
    for c in range(n_chunks):
        rows = slice(c * chunk, (c + 1) * chunk)
        for h in range(RET_HEADS):
            cols = slice(h * HEAD_DIM, (h + 1) * HEAD_DIM)
            hc = c * RET_HEADS + h
            lhs = jnp.concatenate([p_ref[hc], qx_ref[rows, cols]], axis=1)
            rhs = jnp.concatenate([v_ref[rows, cols], sbf_ref[hc]], axis=0)
            o = jnp.dot(lhs, rhs, preferred_element_type=f32)
            mu = jnp.mean(o, axis=-1, keepdims=True)
            d = o - mu
            var = jnp.mean(d * d, axis=-1, keepdims=True)
            y = d * lax.rsqrt(var + EPS) * retg_ref[:, cols]
            mixed_ref[rows, D_CONV + h * HEAD_DIM:D_CONV + (h + 1) * HEAD_DIM] = (
                y * gate[rows, cols]).astype(bf16)

    y = jnp.dot(mixed_ref[...], wout_ref[...], preferred_element_type=f32) + x
    out_ref[0] = _rms_norm(y, gf_ref[...])


def kernel(x, meta, norm1_g, w_in, conv_w, ret_norm_g, w_out, final_g):
    bsz, seq, d_model = x.shape
    assert d_model == D_MODEL and meta.shape == (N_META, D_MODEL)
    tokens, chunk = TOKENS_PER_STEP, RET_CHUNK
    assert seq % tokens == 0 and tokens % chunk == 0

    cos, sin = _rope_tables(N_META + seq)
    decay, xi, zeta = _decay_tables(chunk)
    chunk_decay = tuple(float(v) for v in _gammas() ** chunk)
    n_hc = (tokens // chunk) * RET_HEADS

    const = lambda *shape: pl.BlockSpec(shape, lambda b, j: (0,) * len(shape))
    grid = (bsz, seq // tokens)
    body = functools.partial(_mixer_kernel, tokens=tokens, chunk=chunk, chunk_decay=chunk_decay)
    return pl.pallas_call(
        body,
        grid=grid,
        in_specs=[
            pl.BlockSpec((1, tokens, D_MODEL), lambda b, j: (b, j, 0)),
            const(N_META, D_MODEL),
            const(1, D_MODEL),
            pl.BlockSpec(memory_space=pl.ANY),
            const(CONV_WIDTH, D_CONV),
            const(1, D_RET),
            pl.BlockSpec(memory_space=pl.ANY),
            const(1, D_MODEL),
            pl.BlockSpec((tokens, HEAD_DIM), lambda b, j: (j, 0)),
            pl.BlockSpec((tokens, HEAD_DIM), lambda b, j: (j, 0)),
            const(N_META, HEAD_DIM),
            const(N_META, HEAD_DIM),
            const(RET_HEADS, chunk, chunk),
            const(RET_HEADS, chunk, HEAD_DIM),
            const(RET_HEADS, chunk, HEAD_DIM),
            const(RET_HEADS, N_META, HEAD_DIM),
        ],
        out_specs=pl.BlockSpec((1, tokens, D_MODEL), lambda b, j: (b, j, 0)),
        out_shape=jax.ShapeDtypeStruct((bsz, seq, D_MODEL), x.dtype),
        scratch_shapes=[
            pltpu.VMEM((D_MODEL, 8 * SEC), jnp.bfloat16),
            pltpu.VMEM((D_MODEL, D_MODEL), jnp.bfloat16),
            pltpu.VMEM((STAGE_SLOTS, STAGE_ROWS, 8 * SEC), jnp.float32),
            pltpu.SemaphoreType.DMA((STAGE_SLOTS,)),
            pltpu.VMEM((RET_HEADS, HEAD_DIM, HEAD_DIM), jnp.float32),
            pltpu.VMEM((RET_HEADS, HEAD_DIM, HEAD_DIM), jnp.float32),
            pltpu.VMEM((HIST_ROWS + tokens, D_CONV), jnp.float32),
            pltpu.VMEM((HIST_ROWS, D_CONV), jnp.float32),
            pltpu.VMEM((tokens, D_RET), jnp.bfloat16),
            pltpu.VMEM((tokens, D_RET), jnp.bfloat16),
            pltpu.VMEM((tokens, D_RET), jnp.bfloat16),
            pltpu.VMEM((tokens, D_RET), jnp.bfloat16),
            pltpu.VMEM((tokens, D_RET), jnp.bfloat16),
            pltpu.VMEM((n_hc, chunk, chunk), jnp.bfloat16),
            pltpu.VMEM((n_hc, HEAD_DIM, HEAD_DIM), jnp.bfloat16),
            pltpu.VMEM((tokens, D_MODEL), jnp.bfloat16),
        ],
        compiler_params=pltpu.CompilerParams(
            dimension_semantics=("arbitrary", "arbitrary"),
            vmem_limit_bytes=VMEM_LIMIT_BYTES),
        name="hymba_mixer",
    )(x, meta, norm1_g.reshape(1, -1), w_in, conv_w,
      ret_norm_g.reshape(1, -1), w_out, final_g.reshape(1, -1),
      jnp.asarray(cos[N_META:]), jnp.asarray(sin[N_META:]),
      jnp.asarray(cos[:N_META]), jnp.asarray(sin[:N_META]),
      jnp.asarray(decay), jnp.asarray(xi), jnp.asarray(zeta), jnp.asarray(_meta_zeta()))
```

```python
import functools

import numpy as np
import jax
import jax.numpy as jnp
from jax import lax
from jax.experimental import pallas as pl
from jax.experimental.pallas import tpu as pltpu

D_MODEL = 1024
N_META = 16
D_CONV = 512
D_RET = 512
RET_HEADS = 4
HEAD_DIM = 128
HALF = HEAD_DIM // 2
CONV_WIDTH = 3
ROPE_BASE = 10000.0
EPS = 1e-6

SEC = 512
CX, CB, CC, CG, Q, K, V, RG = range(8)

TOKENS_PER_STEP = 1024
RET_CHUNK = 128
HIST_ROWS = 8
STAGE_ROWS = 64
STAGE_SLOTS = 4
FINISH_GROUPS = 4
VMEM_LIMIT_BYTES = 58 * 1024 * 1024


def _gammas():
    return 1.0 - 2.0 ** (-5.0 - np.arange(RET_HEADS, dtype=np.float64))


def _rope_tables(n_pos):
    freqs = 1.0 / (ROPE_BASE ** (np.arange(HALF, dtype=np.float64) / HALF))
    ang = np.arange(n_pos, dtype=np.float64)[:, None] * freqs[None, :]
    cos = np.concatenate([np.cos(ang), np.cos(ang)], axis=1)
    sin = np.concatenate([-np.sin(ang), np.sin(ang)], axis=1)
    return cos.astype(np.float32), sin.astype(np.float32)


def _decay_tables(chunk):
    g = _gammas()
    idx = np.arange(chunk, dtype=np.float64)
    diff = idx[:, None] - idx[None, :]
    scale = HEAD_DIM ** -0.5
    decay = np.where(diff[None] >= 0, g[:, None, None] ** np.maximum(diff[None], 0.0), 0.0) * scale
    xi = (g[:, None] ** (idx[None, :] + 1.0)) * scale
    zeta = g[:, None] ** (chunk - 1.0 - idx[None, :])
    xi = np.broadcast_to(xi[:, :, None], (RET_HEADS, chunk, HEAD_DIM))
    zeta = np.broadcast_to(zeta[:, :, None], (RET_HEADS, chunk, HEAD_DIM))
    return decay.astype(np.float32), xi.astype(np.float32), zeta.astype(np.float32)


def _meta_zeta():
    g = _gammas()
    j = np.arange(N_META, dtype=np.float64)
    z = g[:, None] ** (N_META - 1.0 - j[None, :])
    return np.broadcast_to(z[:, :, None], (RET_HEADS, N_META, HEAD_DIM)).astype(np.float32)


def _rms_norm(x, g):
    return x * lax.rsqrt(jnp.mean(x * x, axis=-1, keepdims=True) + EPS) * g


def _silu(x):
    return x * (1.0 / (1.0 + jnp.exp(-x)))


def _rotary(t, cos, sin):
    return t * cos + pltpu.roll(t, HALF, axis=1) * sin


def _project(hn_bf16, win_ref, sec):
    return jnp.dot(hn_bf16, win_ref[:, sec * SEC:(sec + 1) * SEC], preferred_element_type=jnp.float32)


def _head(t, h):
    return t[:, h * HEAD_DIM:(h + 1) * HEAD_DIM]


def _zero_tile_after(stored_f32):
    bits = pltpu.bitcast(stored_f32, jnp.uint32)
    rows, cols = bits.shape
    acc = bits[0:8]
    for r in range(8, rows, 8):
        acc = acc | bits[r:r + 8]
    word = acc[:, 0:128]
    for c in range(128, cols, 128):
        word = word | acc[:, c:c + 128]
    return pltpu.bitcast((word >> 16) >> 16, jnp.bfloat16)


def _after(lhs_bf16, zero_tile):
    top = jnp.concatenate([lhs_bf16[0:16, 0:128] + zero_tile, lhs_bf16[0:16, 128:]], axis=1)
    return jnp.concatenate([top, lhs_bf16[16:]], axis=0)


def _stage_copy(w_hbm, stage_ref, sem_ref, chunk_idx):
    slot = chunk_idx % STAGE_SLOTS
    rows = pl.ds(chunk_idx * STAGE_ROWS, STAGE_ROWS)
    return pltpu.make_async_copy(w_hbm.at[rows, :], stage_ref.at[slot, :, 0:w_hbm.shape[1]], sem_ref.at[slot])


def _load_weight_as_bf16(w_hbm, w_bf16_ref, stage_ref, sem_ref):
    n_rows, n_cols = w_hbm.shape
    n_chunks = n_rows // STAGE_ROWS
    for c in range(min(STAGE_SLOTS, n_chunks)):
        _stage_copy(w_hbm, stage_ref, sem_ref, c).start()
    for c in range(n_chunks):
        _stage_copy(w_hbm, stage_ref, sem_ref, c).wait()
        w_bf16_ref[c * STAGE_ROWS:(c + 1) * STAGE_ROWS, :] = (
            stage_ref[c % STAGE_SLOTS, :, 0:n_cols].astype(jnp.bfloat16))
        if c + STAGE_SLOTS < n_chunks:
            _stage_copy(w_hbm, stage_ref, sem_ref, c + STAGE_SLOTS).start()


def _mixer_kernel(x_ref, meta_ref, g1_ref, win_hbm, convw_ref, retg_ref, wout_hbm, gf_ref,
                  cos_ref, sin_ref, mcos_ref, msin_ref, decay_ref, xi_ref, zeta_ref, mzeta_ref,
                  out_ref,
                  win_ref, wout_ref, stage_ref, stage_sem,
                  state_ref, state0_ref, u_ref, hist0_ref, q_ref, qx_ref, k_ref, kz_ref, v_ref,
                  p_ref, sbf_ref, mixed_ref, y_ref,
                  *, tokens, chunk, chunk_decay, blocks_per_seq, n_blocks):
    s = pl.program_id(0)
    bf16 = jnp.bfloat16
    f32 = jnp.float32

    def _finish_rows(rows):
        out_ref[0, rows, :] = _rms_norm(y_ref[rows, :], gf_ref[...])

    def _finish_previous_block():
        _finish_rows(slice(0, tokens))

    @pl.when(s == 0)
    def _first_step():
        y_ref[...] = jnp.zeros_like(y_ref)
        _load_weight_as_bf16(win_hbm, win_ref, stage_ref, stage_sem)
        _load_weight_as_bf16(wout_hbm, wout_ref, stage_ref, stage_sem)
        hm = _rms_norm(meta_ref[...], g1_ref[...]).astype(bf16)
        u_m = _project(hm, win_ref, CC) * _project(hm, win_ref, CX)
        hist0_ref[...] = u_m[N_META - HIST_ROWS:, :]
        k_m = _project(hm, win_ref, K)
        v_m = _project(hm, win_ref, V).astype(bf16)
        for h in range(RET_HEADS):
            kr = _rotary(_head(k_m, h), mcos_ref[...], msin_ref[...]) * mzeta_ref[h]
            state0_ref[h] = lax.dot_general(kr.astype(bf16), _head(v_m, h), (((0,), (0,)), ((), ())),
                                            preferred_element_type=f32)

    @pl.when(jnp.logical_and(s % blocks_per_seq == 0, s < n_blocks))
    def _start_sequence():
        state_ref[...] = state0_ref[...]
        u_ref[0:HIST_ROWS, :] = hist0_ref[...]

    @pl.when(s < n_blocks)
    def _mix_block():
        group = tokens // FINISH_GROUPS
        finished = []
        for g in range(FINISH_GROUPS):
            rows = slice(g * group, (g + 1) * group)
            _finish_rows(rows)
            finished.append(_zero_tile_after(out_ref[0, rows, :]))
        x = x_ref[0]
        hn = _rms_norm(x, g1_ref[...]).astype(bf16)

        n_chunks = tokens // chunk

        cos = cos_ref[...]
        sin = sin_ref[...]
        q = _project(hn, win_ref, Q)
        k = _project(hn, win_ref, K)
        for h in range(RET_HEADS):
            cols = slice(h * HEAD_DIM, (h + 1) * HEAD_DIM)
            qr = _rotary(_head(q, h), cos, sin)
            kr = _rotary(_head(k, h), cos, sin)
            q_ref[:, cols] = qr.astype(bf16)
            k_ref[:, cols] = kr.astype(bf16)
            for c in range(n_chunks):
                rows = slice(c * chunk, (c + 1) * chunk)
                qx_ref[rows, cols] = (qr[rows] * xi_ref[h]).astype(bf16)
                kz_ref[rows, cols] = (kr[rows] * zeta_ref[h]).astype(bf16)
        v_ref[...] = _project(_after(hn, finished[0]), win_ref, V).astype(bf16)

        upd = {}
        for c in range(n_chunks):
            rows = slice(c * chunk, (c + 1) * chunk)
            for h in range(RET_HEADS):
                cols = slice(h * HEAD_DIM, (h + 1) * HEAD_DIM)
                scores = lax.dot_general(q_ref[rows, cols], k_ref[rows, cols], (((1,), (1,)), ((), ())),
                                         preferred_element_type=f32)
                p_ref[c * RET_HEADS + h] = (scores * decay_ref[h]).astype(bf16)
                upd[c, h] = lax.dot_general(kz_ref[rows, cols], v_ref[rows, cols], (((0,), (0,)), ((), ())),
                                            preferred_element_type=f32)
        for h in range(RET_HEADS):
            state = state_ref[h]
            for c in range(n_chunks):
                sbf_ref[c * RET_HEADS + h] = state.astype(bf16)
                state = chunk_decay[h] * state + upd[c, h]
            state_ref[h] = state

        u = _project(_after(hn, finished[1]), win_ref, CC) * _project(_after(hn, finished[2]), win_ref, CX)
        u_ref[HIST_ROWS:HIST_ROWS + tokens, :] = u
        u1 = u_ref[HIST_ROWS - 1:HIST_ROWS - 1 + tokens, :]
        u2 = u_ref[HIST_ROWS - 2:HIST_ROWS - 2 + tokens, :]
        conv = convw_ref[0:1, :] * u2 + convw_ref[1:2, :] * u1 + convw_ref[2:3, :] * u
        conv_out = _project(_after(hn, finished[3]), win_ref, CB) * conv * _silu(_project(hn, win_ref, CG))
        mixed_ref[:, 0:D_CONV] = conv_out.astype(bf16)
        u_ref[0:HIST_ROWS, :] = u_ref[tokens:tokens + HIST_ROWS, :]

        gate = _silu(_project(hn, win_ref, RG))
        for c in range(n_chunks):
            rows = slice(c * chunk, (c + 1) * chunk)
            for h in range(RET_HEADS):
                cols = slice(h * HEAD_DIM, (h + 1) * HEAD_DIM)
                hc = c * RET_HEADS + h
                lhs = jnp.concatenate([p_ref[hc], qx_ref[rows, cols]], axis=1)
                rhs = jnp.concatenate([v_ref[rows, cols], sbf_ref[hc]], axis=0)
                o = jnp.dot(lhs, rhs, preferred_element_type=f32)
                mu = jnp.mean(o, axis=-1, keepdims=True)
                d = o - mu
                var = jnp.mean(d * d, axis=-1, keepdims=True)
                y = d * lax.rsqrt(var + EPS) * retg_ref[:, cols]
                mixed_ref[rows, D_CONV + h * HEAD_DIM:D_CONV + (h + 1) * HEAD_DIM] = (
                    y * gate[rows, cols]).astype(bf16)

        y_ref[...] = jnp.dot(mixed_ref[...], wout_ref[...], preferred_element_type=f32) + x

    @pl.when(s == n_blocks)
    def _drain():
        _finish_previous_block()


def kernel(x, meta, norm1_g, w_in, conv_w, ret_norm_g, w_out, final_g):
    bsz, seq, d_model = x.shape
    assert d_model == D_MODEL and meta.shape == (N_META, D_MODEL)
    tokens, chunk = TOKENS_PER_STEP, RET_CHUNK
    assert seq % tokens == 0 and tokens % chunk == 0

    cos, sin = _rope_tables(N_META + seq)
    decay, xi, zeta = _decay_tables(chunk)
    chunk_decay = tuple(float(v) for v in _gammas() ** chunk)
    n_hc = (tokens // chunk) * RET_HEADS

    blocks_per_seq = seq // tokens
    n_blocks = bsz * blocks_per_seq

    def mixed_block(s):
        s = jnp.minimum(s, n_blocks - 1)
        return s // blocks_per_seq, s % blocks_per_seq

    def finished_block(s):
        s = jnp.maximum(s - 1, 0)
        return s // blocks_per_seq, s % blocks_per_seq

    const = lambda *shape: pl.BlockSpec(shape, lambda s: (0,) * len(shape))
    rope_spec = pl.BlockSpec((tokens, HEAD_DIM), lambda s: (mixed_block(s)[1], 0))
    body = functools.partial(_mixer_kernel, tokens=tokens, chunk=chunk, chunk_decay=chunk_decay,
                             blocks_per_seq=blocks_per_seq, n_blocks=n_blocks)
    return pl.pallas_call(
        body,
        grid=(n_blocks + 1,),
        in_specs=[
            pl.BlockSpec((1, tokens, D_MODEL), lambda s: (*mixed_block(s), 0)),
            const(N_META, D_MODEL),
            const(1, D_MODEL),
            pl.BlockSpec(memory_space=pl.ANY),
            const(CONV_WIDTH, D_CONV),
            const(1, D_RET),
            pl.BlockSpec(memory_space=pl.ANY),
            const(1, D_MODEL),
            rope_spec,
            rope_spec,
            const(N_META, HEAD_DIM),
            const(N_META, HEAD_DIM),
            const(RET_HEADS, chunk, chunk),
            const(RET_HEADS, chunk, HEAD_DIM),
            const(RET_HEADS, chunk, HEAD_DIM),
            const(RET_HEADS, N_META, HEAD_DIM),
        ],
        out_specs=pl.BlockSpec((1, tokens, D_MODEL), lambda s: (*finished_block(s), 0)),
        out_shape=jax.ShapeDtypeStruct((bsz, seq, D_MODEL), x.dtype),
        scratch_shapes=[
            pltpu.VMEM((D_MODEL, 8 * SEC), jnp.bfloat16),
            pltpu.VMEM((D_MODEL, D_MODEL), jnp.bfloat16),
            pltpu.VMEM((STAGE_SLOTS, STAGE_ROWS, 8 * SEC), jnp.float32),
            pltpu.SemaphoreType.DMA((STAGE_SLOTS,)),
            pltpu.VMEM((RET_HEADS, HEAD_DIM, HEAD_DIM), jnp.float32),
            pltpu.VMEM((RET_HEADS, HEAD_DIM, HEAD_DIM), jnp.float32),
            pltpu.VMEM((HIST_ROWS + tokens, D_CONV), jnp.float32),
            pltpu.VMEM((HIST_ROWS, D_CONV), jnp.float32),
            pltpu.VMEM((tokens, D_RET), jnp.bfloat16),
            pltpu.VMEM((tokens, D_RET), jnp.bfloat16),
            pltpu.VMEM((tokens, D_RET), jnp.bfloat16),
            pltpu.VMEM((tokens, D_RET), jnp.bfloat16),
            pltpu.VMEM((tokens, D_RET), jnp.bfloat16),
            pltpu.VMEM((n_hc, chunk, chunk), jnp.bfloat16),
            pltpu.VMEM((n_hc, HEAD_DIM, HEAD_DIM), jnp.bfloat16),
            pltpu.VMEM((tokens, D_MODEL), jnp.bfloat16),
            pltpu.VMEM((tokens, D_MODEL), jnp.float32),
        ],
        compiler_params=pltpu.CompilerParams(
            dimension_semantics=("arbitrary",),
            vmem_limit_bytes=VMEM_LIMIT_BYTES),
        name="hymba_mixer",
    )(x, meta, norm1_g.reshape(1, -1), w_in, conv_w,
      ret_norm_g.reshape(1, -1), w_out, final_g.reshape(1, -1),
      jnp.asarray(cos[N_META:]), jnp.asarray(sin[N_META:]),
      jnp.asarray(cos[:N_META]), jnp.asarray(sin[:N_META]),
      jnp.asarray(decay), jnp.asarray(xi), jnp.asarray(zeta), jnp.asarray(_meta_zeta()))
```

```python
import functools

import numpy as np
import jax
import jax.numpy as jnp
from jax import lax
from jax.experimental import pallas as pl
from jax.experimental.pallas import tpu as pltpu

D_MODEL = 1024
N_META = 16
D_CONV = 512
D_RET = 512
RET_HEADS = 4
HEAD_DIM = 128
HALF = HEAD_DIM // 2
CONV_WIDTH = 3
ROPE_BASE = 10000.0
EPS = 1e-6

SEC = 512
CX, CB, CC, CG, Q, K, V, RG = range(8)

TOKENS_PER_STEP = 1024
RET_CHUNK = 128
HIST_ROWS = 8
STAGE_ROWS = 256
STAGE_SLOTS = 8
FINISH_GROUPS = 4
VMEM_LIMIT_BYTES = 58 * 1024 * 1024


def _gammas():
    return 1.0 - 2.0 ** (-5.0 - np.arange(RET_HEADS, dtype=np.float64))


def _rope_tables(n_pos):
    freqs = 1.0 / (ROPE_BASE ** (np.arange(HALF, dtype=np.float64) / HALF))
    ang = np.arange(n_pos, dtype=np.float64)[:, None] * freqs[None, :]
    cos = np.concatenate([np.cos(ang), np.cos(ang)], axis=1)
    sin = np.concatenate([-np.sin(ang), np.sin(ang)], axis=1)
    return cos.astype(np.float32), sin.astype(np.float32)


def _decay_tables(chunk):
    g = _gammas()
    idx = np.arange(chunk, dtype=np.float64)
    diff = idx[:, None] - idx[None, :]
    scale = HEAD_DIM ** -0.5
    decay = np.where(diff[None] >= 0, g[:, None, None] ** np.maximum(diff[None], 0.0), 0.0) * scale
    xi = (g[:, None] ** (idx[None, :] + 1.0)) * scale
    zeta = g[:, None] ** (chunk - 1.0 - idx[None, :])
    xi = np.broadcast_to(xi[:, :, None], (RET_HEADS, chunk, HEAD_DIM))
    zeta = np.broadcast_to(zeta[:, :, None], (RET_HEADS, chunk, HEAD_DIM))
    return decay.astype(np.float32), xi.astype(np.float32), zeta.astype(np.float32)


def _meta_zeta():
    g = _gammas()
    j = np.arange(N_META, dtype=np.float64)
    z = g[:, None] ** (N_META - 1.0 - j[None, :])
    return np.broadcast_to(z[:, :, None], (RET_HEADS, N_META, HEAD_DIM)).astype(np.float32)


def _rms_norm(x, g):
    return x * lax.rsqrt(jnp.mean(x * x, axis=-1, keepdims=True) + EPS) * g


def _silu(x):
    return x * (1.0 / (1.0 + jnp.exp(-x)))


def _rotary(t, cos, sin):
    return t * cos + pltpu.roll(t, HALF, axis=1) * sin


def _project(hn_bf16, win_ref, sec):
    return jnp.dot(hn_bf16, win_ref[:, sec * SEC:(sec + 1) * SEC], preferred_element_type=jnp.float32)


def _head(t, h):
    return t[:, h * HEAD_DIM:(h + 1) * HEAD_DIM]


def _zero_tile_after(stored_f32):
    bits = pltpu.bitcast(stored_f32, jnp.uint32)
    rows, cols = bits.shape
    acc = bits[0:8]
    for r in range(8, rows, 8):
        acc = acc | bits[r:r + 8]
    word = acc[:, 0:128]
    for c in range(128, cols, 128):
        word = word | acc[:, c:c + 128]
    return pltpu.bitcast((word >> 16) >> 16, jnp.bfloat16)


def _after(lhs_bf16, zero_tile):
    top = jnp.concatenate([lhs_bf16[0:16, 0:128] + zero_tile, lhs_bf16[0:16, 128:]], axis=1)
    return jnp.concatenate([top, lhs_bf16[16:]], axis=0)


def _weight_chunks(*weights):
    chunks = []
    for w_hbm, w_bf16_ref in weights:
        n_rows, n_cols = w_hbm.shape
        for c0 in range(0, n_cols, D_MODEL):
            for r0 in range(0, n_rows, STAGE_ROWS):
                chunks.append((w_hbm, w_bf16_ref, r0, c0))
    return chunks


def _stage_rows(idx):
    slot = idx % STAGE_SLOTS
    return pl.ds(slot * STAGE_ROWS, STAGE_ROWS)


def _stage_copy(chunks, idx, stage_ref, sem_ref):
    w_hbm, _, r0, c0 = chunks[idx]
    return pltpu.make_async_copy(w_hbm.at[pl.ds(r0, STAGE_ROWS), pl.ds(c0, D_MODEL)],
                                 stage_ref.at[_stage_rows(idx), :], sem_ref.at[idx % STAGE_SLOTS])


def _load_weights_as_bf16(chunks, stage_ref, sem_ref):
    for i in range(min(STAGE_SLOTS, len(chunks))):
        _stage_copy(chunks, i, stage_ref, sem_ref).start()
    for i, (_, w_bf16_ref, r0, c0) in enumerate(chunks):
        _stage_copy(chunks, i, stage_ref, sem_ref).wait()
        w_bf16_ref[r0:r0 + STAGE_ROWS, c0:c0 + D_MODEL] = stage_ref[_stage_rows(i), :].astype(jnp.bfloat16)
        if i + STAGE_SLOTS < len(chunks):
            _stage_copy(chunks, i + STAGE_SLOTS, stage_ref, sem_ref).start()


def _mixer_kernel(x_ref, meta_ref, g1_ref, win_hbm, convw_ref, retg_ref, wout_hbm, gf_ref,
                  cos_ref, sin_ref, mcos_ref, msin_ref, decay_ref, xi_ref, zeta_ref, mzeta_ref,
                  out_ref,
                  win_ref, wout_ref, ystage_ref, stage_sem,
                  state_ref, state0_ref, u_ref, hist0_ref, q_ref, qx_ref, k_ref, kz_ref, v_ref,
                  p_ref, sbf_ref, mixed_ref,
                  *, tokens, chunk, chunk_decay, blocks_per_seq, n_blocks):
    s = pl.program_id(0)
    bf16 = jnp.bfloat16
    f32 = jnp.float32
    y_ref = ystage_ref.at[0:tokens, :]

    def _finish_rows(rows):
        out_ref[0, rows, :] = _rms_norm(y_ref[rows, :], gf_ref[...])

    def _finish_previous_block():
        _finish_rows(slice(0, tokens))

    @pl.when(s == 0)
    def _first_step():
        _load_weights_as_bf16(_weight_chunks((win_hbm, win_ref), (wout_hbm, wout_ref)), ystage_ref, stage_sem)
        y_ref[...] = jnp.zeros((tokens, D_MODEL), f32)
        hm = _rms_norm(meta_ref[...], g1_ref[...]).astype(bf16)
        u_m = _project(hm, win_ref, CC) * _project(hm, win_ref, CX)
        hist0_ref[...] = u_m[N_META - HIST_ROWS:, :]
        k_m = _project(hm, win_ref, K)
        v_m = _project(hm, win_ref, V).astype(bf16)
        for h in range(RET_HEADS):
            kr = _rotary(_head(k_m, h), mcos_ref[...], msin_ref[...]) * mzeta_ref[h]
            state0_ref[h] = lax.dot_general(kr.astype(bf16), _head(v_m, h), (((0,), (0,)), ((), ())),
                                            preferred_element_type=f32)

    @pl.when(jnp.logical_and(s % blocks_per_seq == 0, s < n_blocks))
    def _start_sequence():
        state_ref[...] = state0_ref[...]
        u_ref[0:HIST_ROWS, :] = hist0_ref[...]

    @pl.when(s < n_blocks)
    def _mix_block():
        group = tokens // FINISH_GROUPS
        finished = []
        for g in range(FINISH_GROUPS):
            rows = slice(g * group, (g + 1) * group)
            _finish_rows(rows)
            finished.append(_zero_tile_after(out_ref[0, rows, :]))
        x = x_ref[0]
        hn = _rms_norm(x, g1_ref[...]).astype(bf16)

        n_chunks = tokens // chunk

        cos = cos_ref[...]
        sin = sin_ref[...]
        q = _project(hn, win_ref, Q)
        k = _project(hn, win_ref, K)
        for h in range(RET_HEADS):
            cols = slice(h * HEAD_DIM, (h + 1) * HEAD_DIM)
            qr = _rotary(_head(q, h), cos, sin)
            kr = _rotary(_head(k, h), cos, sin)
            q_ref[:, cols] = qr.astype(bf16)
            k_ref[:, cols] = kr.astype(bf16)
            for c in range(n_chunks):
                rows = slice(c * chunk, (c + 1) * chunk)
                qx_ref[rows, cols] = (qr[rows] * xi_ref[h]).astype(bf16)
                kz_ref[rows, cols] = (kr[rows] * zeta_ref[h]).astype(bf16)
        v_ref[...] = _project(_after(hn, finished[0]), win_ref, V).astype(bf16)

        upd = {}
        for c in range(n_chunks):
            rows = slice(c * chunk, (c + 1) * chunk)
            for h in range(RET_HEADS):
                cols = slice(h * HEAD_DIM, (h + 1) * HEAD_DIM)
                scores = lax.dot_general(q_ref[rows, cols], k_ref[rows, cols], (((1,), (1,)), ((), ())),
                                         preferred_element_type=f32)
                p_ref[c * RET_HEADS + h] = (scores * decay_ref[h]).astype(bf16)
                upd[c, h] = lax.dot_general(kz_ref[rows, cols], v_ref[rows, cols], (((0,), (0,)), ((), ())),
                                            preferred_element_type=f32)
        for h in range(RET_HEADS):
            state = state_ref[h]
            for c in range(n_chunks):
                sbf_ref[c * RET_HEADS + h] = state.astype(bf16)
                state = chunk_decay[h] * state + upd[c, h]
            state_ref[h] = state

        u = _project(_after(hn, finished[1]), win_ref, CC) * _project(_after(hn, finished[2]), win_ref, CX)
        u_ref[HIST_ROWS:HIST_ROWS + tokens, :] = u
        u1 = u_ref[HIST_ROWS - 1:HIST_ROWS - 1 + tokens, :]
        u2 = u_ref[HIST_ROWS - 2:HIST_ROWS - 2 + tokens, :]
        conv = convw_ref[0:1, :] * u2 + convw_ref[1:2, :] * u1 + convw_ref[2:3, :] * u
        conv_out = _project(_after(hn, finished[3]), win_ref, CB) * conv * _silu(_project(hn, win_ref, CG))
        mixed_ref[:, 0:D_CONV] = conv_out.astype(bf16)
        u_ref[0:HIST_ROWS, :] = u_ref[tokens:tokens + HIST_ROWS, :]

        gate = _silu(_project(hn, win_ref, RG))
        for c in range(n_chunks):
            rows = slice(c * chunk, (c + 1) * chunk)
            for h in range(RET_HEADS):
                cols = slice(h * HEAD_DIM, (h + 1) * HEAD_DIM)
                hc = c * RET_HEADS + h
                lhs = jnp.concatenate([p_ref[hc], qx_ref[rows, cols]], axis=1)
                rhs = jnp.concatenate([v_ref[rows, cols], sbf_ref[hc]], axis=0)
                o = jnp.dot(lhs, rhs, preferred_element_type=f32)
                mu = jnp.mean(o, axis=-1, keepdims=True)
                d = o - mu
                var = jnp.mean(d * d, axis=-1, keepdims=True)
                y = d * lax.rsqrt(var + EPS) * retg_ref[:, cols]
                mixed_ref[rows, D_CONV + h * HEAD_DIM:D_CONV + (h + 1) * HEAD_DIM] = (
                    y * gate[rows, cols]).astype(bf16)

        y_ref[...] = jnp.dot(mixed_ref[...], wout_ref[...], preferred_element_type=f32) + x

    @pl.when(s == n_blocks)
    def _drain():
        _finish_previous_block()


def kernel(x, meta, norm1_g, w_in, conv_w, ret_norm_g, w_out, final_g):
    bsz, seq, d_model = x.shape
    assert d_model == D_MODEL and meta.shape == (N_META, D_MODEL)
    tokens, chunk = TOKENS_PER_STEP, RET_CHUNK
    assert seq % tokens == 0 and tokens % chunk == 0 and tokens <= STAGE_SLOTS * STAGE_ROWS

    cos, sin = _rope_tables(N_META + seq)
    decay, xi, zeta = _decay_tables(chunk)
    chunk_decay = tuple(float(v) for v in _gammas() ** chunk)
    n_hc = (tokens // chunk) * RET_HEADS

    blocks_per_seq = seq // tokens
    n_blocks = bsz * blocks_per_seq

    def mixed_block(s):
        s = jnp.minimum(s, n_blocks - 1)
        return s // blocks_per_seq, s % blocks_per_seq

    def finished_block(s):
        s = jnp.maximum(s - 1, 0)
        return s // blocks_per_seq, s % blocks_per_seq

    const = lambda *shape: pl.BlockSpec(shape, lambda s: (0,) * len(shape))
    rope_spec = pl.BlockSpec((tokens, HEAD_DIM), lambda s: (mixed_block(s)[1], 0))
    body = functools.partial(_mixer_kernel, tokens=tokens, chunk=chunk, chunk_decay=chunk_decay,
                             blocks_per_seq=blocks_per_seq, n_blocks=n_blocks)
    return pl.pallas_call(
        body,
        grid=(n_blocks + 1,),
        in_specs=[
            pl.BlockSpec((1, tokens, D_MODEL), lambda s: (*mixed_block(s), 0)),
            const(N_META, D_MODEL),
            const(1, D_MODEL),
            pl.BlockSpec(memory_space=pl.ANY),
            const(CONV_WIDTH, D_CONV),
            const(1, D_RET),
            pl.BlockSpec(memory_space=pl.ANY),
            const(1, D_MODEL),
            rope_spec,
            rope_spec,
            const(N_META, HEAD_DIM),
            const(N_META, HEAD_DIM),
            const(RET_HEADS, chunk, chunk),
            const(RET_HEADS, chunk, HEAD_DIM),
            const(RET_HEADS, chunk, HEAD_DIM),
            const(RET_HEADS, N_META, HEAD_DIM),
        ],
        out_specs=pl.BlockSpec((1, tokens, D_MODEL), lambda s: (*finished_block(s), 0)),
        out_shape=jax.ShapeDtypeStruct((bsz, seq, D_MODEL), x.dtype),
        scratch_shapes=[
            pltpu.VMEM((D_MODEL, 8 * SEC), jnp.bfloat16),
            pltpu.VMEM((D_MODEL, D_MODEL), jnp.bfloat16),
            pltpu.VMEM((STAGE_SLOTS * STAGE_ROWS, D_MODEL), jnp.float32),
            pltpu.SemaphoreType.DMA((STAGE_SLOTS,)),
            pltpu.VMEM((RET_HEADS, HEAD_DIM, HEAD_DIM), jnp.float32),
            pltpu.VMEM((RET_HEADS, HEAD_DIM, HEAD_DIM), jnp.float32),
            pltpu.VMEM((HIST_ROWS + tokens, D_CONV), jnp.float32),
            pltpu.VMEM((HIST_ROWS, D_CONV), jnp.float32),
            pltpu.VMEM((tokens, D_RET), jnp.bfloat16),
            pltpu.VMEM((tokens, D_RET), jnp.bfloat16),
            pltpu.VMEM((tokens, D_RET), jnp.bfloat16),
            pltpu.VMEM((tokens, D_RET), jnp.bfloat16),
            pltpu.VMEM((tokens, D_RET), jnp.bfloat16),
            pltpu.VMEM((n_hc, chunk, chunk), jnp.bfloat16),
            pltpu.VMEM((n_hc, HEAD_DIM, HEAD_DIM), jnp.bfloat16),
            pltpu.VMEM((tokens, D_MODEL), jnp.bfloat16),
        ],
        compiler_params=pltpu.CompilerParams(
            dimension_semantics=("arbitrary",),
            vmem_limit_bytes=VMEM_LIMIT_BYTES),
        name="hymba_mixer",
    )(x, meta, norm1_g.reshape(1, -1), w_in, conv_w,
      ret_norm_g.reshape(1, -1), w_out, final_g.reshape(1, -1),
      jnp.asarray(cos[N_META:]), jnp.asarray(sin[N_META:]),
      jnp.asarray(cos[:N_META]), jnp.asarray(sin[:N_META]),
      jnp.asarray(decay), jnp.asarray(xi), jnp.asarray(zeta), jnp.asarray(_meta_zeta()))
```

```python
import functools

import numpy as np
import jax
import jax.numpy as jnp
from jax import lax
from jax.experimental import pallas as pl
from jax.experimental.pallas import tpu as pltpu

D_MODEL = 1024
N_META = 16
D_CONV = 512
D_RET = 512
RET_HEADS = 4
HEAD_DIM = 128
HALF = HEAD_DIM // 2
CONV_WIDTH = 3
ROPE_BASE = 10000.0
EPS = 1e-6

SEC = 512
CX, CB, CC, CG, Q, K, V, RG = range(8)

TOKENS_PER_STEP = 1024
RET_CHUNK = 128
HIST_ROWS = 8
STAGE_ROWS = 256
STAGE_SLOTS = 8
FINISH_GROUPS = 4
VMEM_LIMIT_BYTES = 58 * 1024 * 1024


def _gammas():
    return 1.0 - 2.0 ** (-5.0 - np.arange(RET_HEADS, dtype=np.float64))


def _rope_tables(n_pos):
    freqs = 1.0 / (ROPE_BASE ** (np.arange(HALF, dtype=np.float64) / HALF))
    ang = np.arange(n_pos, dtype=np.float64)[:, None] * freqs[None, :]
    cos = np.concatenate([np.cos(ang), np.cos(ang)], axis=1)
    sin = np.concatenate([-np.sin(ang), np.sin(ang)], axis=1)
    return cos.astype(np.float32), sin.astype(np.float32)


def _decay_tables(chunk):
    g = _gammas()
    idx = np.arange(chunk, dtype=np.float64)
    diff = idx[:, None] - idx[None, :]
    scale = HEAD_DIM ** -0.5
    decay = np.where(diff[None] >= 0, g[:, None, None] ** np.maximum(diff[None], 0.0), 0.0) * scale
    xi = (g[:, None] ** (idx[None, :] + 1.0)) * scale
    zeta = g[:, None] ** (chunk - 1.0 - idx[None, :])
    xi = np.broadcast_to(xi[:, :, None], (RET_HEADS, chunk, HEAD_DIM))
    zeta = np.broadcast_to(zeta[:, :, None], (RET_HEADS, chunk, HEAD_DIM))
    return decay.astype(np.float32), xi.astype(np.float32), zeta.astype(np.float32)


def _meta_zeta():
    g = _gammas()
    j = np.arange(N_META, dtype=np.float64)
    z = g[:, None] ** (N_META - 1.0 - j[None, :])
    return np.broadcast_to(z[:, :, None], (RET_HEADS, N_META, HEAD_DIM)).astype(np.float32)


def _rms_norm(x, g):
    return x * lax.rsqrt(jnp.mean(x * x, axis=-1, keepdims=True) + EPS) * g


def _silu(x):
    return x * (1.0 / (1.0 + jnp.exp(-x)))


def _rotary(t, cos, sin):
    return t * cos + pltpu.roll(t, HALF, axis=1) * sin


def _project(hn_bf16, win_ref, sec):
    return jnp.dot(hn_bf16, win_ref[:, sec * SEC:(sec + 1) * SEC], preferred_element_type=jnp.float32)


def _head(t, h):
    return t[:, h * HEAD_DIM:(h + 1) * HEAD_DIM]


def _zero_tile_after(stored_f32):
    bits = pltpu.bitcast(stored_f32, jnp.uint32)
    rows, cols = bits.shape
    acc = bits[0:8]
    for r in range(8, rows, 8):
        acc = acc | bits[r:r + 8]
    word = acc[:, 0:128]
    for c in range(128, cols, 128):
        word = word | acc[:, c:c + 128]
    return pltpu.bitcast((word >> 16) >> 16, jnp.bfloat16)


def _after(lhs_bf16, zero_tile):
    top = jnp.concatenate([lhs_bf16[0:16, 0:128] + zero_tile, lhs_bf16[0:16, 128:]], axis=1)
    return jnp.concatenate([top, lhs_bf16[16:]], axis=0)


def _weight_chunks(*weights):
    chunks = []
    for w_hbm, w_bf16_ref in weights:
        n_rows, n_cols = w_hbm.shape
        for c0 in range(0, n_cols, D_MODEL):
            for r0 in range(0, n_rows, STAGE_ROWS):
                chunks.append((w_hbm, w_bf16_ref, r0, c0))
    return chunks


def _stage_rows(idx):
    slot = idx % STAGE_SLOTS
    return pl.ds(slot * STAGE_ROWS, STAGE_ROWS)


def _stage_copy(chunks, idx, stage_ref, sem_ref):
    w_hbm, _, r0, c0 = chunks[idx]
    return pltpu.make_async_copy(w_hbm.at[pl.ds(r0, STAGE_ROWS), pl.ds(c0, D_MODEL)],
                                 stage_ref.at[_stage_rows(idx), :], sem_ref.at[idx % STAGE_SLOTS])


def _load_weights_as_bf16(chunks, stage_ref, sem_ref):
    for i in range(min(STAGE_SLOTS, len(chunks))):
        _stage_copy(chunks, i, stage_ref, sem_ref).start()
    for i, (_, w_bf16_ref, r0, c0) in enumerate(chunks):
        _stage_copy(chunks, i, stage_ref, sem_ref).wait()
        w_bf16_ref[r0:r0 + STAGE_ROWS, c0:c0 + D_MODEL] = stage_ref[_stage_rows(i), :].astype(jnp.bfloat16)
        if i + STAGE_SLOTS < len(chunks):
            _stage_copy(chunks, i + STAGE_SLOTS, stage_ref, sem_ref).start()


def _mixer_kernel(x_ref, meta_ref, g1_ref, win_hbm, convw_ref, retg_ref, wout_hbm, gf_ref,
                  cos_ref, sin_ref, mcos_ref, msin_ref, decay_ref, xi_ref, zeta_ref, mzeta_ref,
                  out_ref,
                  win_ref, wout_ref, ystage_ref, stage_sem,
                  state_ref, state0_ref, u_ref, hist0_ref, q_ref, qx_ref, kt_ref, kz_ref, v_ref,
                  p_ref, sbf_ref, mixed_ref,
                  *, tokens, chunk, chunk_decay, blocks_per_seq, n_blocks):
    s = pl.program_id(0)
    bf16 = jnp.bfloat16
    f32 = jnp.float32
    y_ref = ystage_ref.at[0:tokens, :]

    def _finish_rows(rows):
        out_ref[0, rows, :] = _rms_norm(y_ref[rows, :], gf_ref[...])

    def _finish_previous_block():
        _finish_rows(slice(0, tokens))

    @pl.when(s == 0)
    def _first_step():
        _load_weights_as_bf16(_weight_chunks((win_hbm, win_ref), (wout_hbm, wout_ref)), ystage_ref, stage_sem)
        y_ref[...] = jnp.zeros((tokens, D_MODEL), f32)
        hm = _rms_norm(meta_ref[...], g1_ref[...]).astype(bf16)
        u_m = _project(hm, win_ref, CC) * _project(hm, win_ref, CX)
        hist0_ref[...] = u_m[N_META - HIST_ROWS:, :]
        k_m = _project(hm, win_ref, K)
        v_m = _project(hm, win_ref, V).astype(bf16)
        for h in range(RET_HEADS):
            kr = _rotary(_head(k_m, h), mcos_ref[...], msin_ref[...]) * mzeta_ref[h]
            state0_ref[h] = lax.dot_general(kr.astype(bf16), _head(v_m, h), (((0,), (0,)), ((), ())),
                                            preferred_element_type=f32)

    @pl.when(jnp.logical_and(s % blocks_per_seq == 0, s < n_blocks))
    def _start_sequence():
        state_ref[...] = state0_ref[...]
        u_ref[0:HIST_ROWS, :] = hist0_ref[...]

    @pl.when(s < n_blocks)
    def _mix_block():
        group = tokens // FINISH_GROUPS
        finished = []
        for g in range(FINISH_GROUPS):
            rows = slice(g * group, (g + 1) * group)
            _finish_rows(rows)
            finished.append(_zero_tile_after(out_ref[0, rows, :]))
        x = x_ref[0]
        hn = _rms_norm(x, g1_ref[...]).astype(bf16)

        n_chunks = tokens // chunk

        cos = cos_ref[...]
        sin = sin_ref[...]
        q = _project(hn, win_ref, Q)
        k = _project(hn, win_ref, K)
        for h in range(RET_HEADS):
            cols = slice(h * HEAD_DIM, (h + 1) * HEAD_DIM)
            qr = _rotary(_head(q, h), cos, sin)
            kr = _rotary(_head(k, h), cos, sin)
            q_ref[:, cols] = qr.astype(bf16)
            kt_ref[h] = kr.T.astype(bf16)
            for c in range(n_chunks):
                rows = slice(c * chunk, (c + 1) * chunk)
                qx_ref[rows, cols] = (qr[rows] * xi_ref[h]).astype(bf16)
                kz_ref[rows, cols] = (kr[rows] * zeta_ref[h]).astype(bf16)
        v_ref[...] = _project(_after(hn, finished[0]), win_ref, V).astype(bf16)

        upd = {}
        for c in range(n_chunks):
            rows = slice(c * chunk, (c + 1) * chunk)
            for h in range(RET_HEADS):
                cols = slice(h * HEAD_DIM, (h + 1) * HEAD_DIM)
                scores = jnp.dot(q_ref[rows, cols], kt_ref[h, :, rows], preferred_element_type=f32)
                p_ref[c * RET_HEADS + h] = (scores * decay_ref[h]).astype(bf16)
                upd[c, h] = lax.dot_general(kz_ref[rows, cols], v_ref[rows, cols], (((0,), (0,)), ((), ())),
                                            preferred_element_type=f32)
        for h in range(RET_HEADS):
            state = state_ref[h]
            for c in range(n_chunks):
                sbf_ref[c * RET_HEADS + h] = state.astype(bf16)
                state = chunk_decay[h] * state + upd[c, h]
            state_ref[h] = state

        u = _project(_after(hn, finished[1]), win_ref, CC) * _project(_after(hn, finished[2]), win_ref, CX)
        u_ref[HIST_ROWS:HIST_ROWS + tokens, :] = u
        u1 = u_ref[HIST_ROWS - 1:HIST_ROWS - 1 + tokens, :]
        u2 = u_ref[HIST_ROWS - 2:HIST_ROWS - 2 + tokens, :]
        conv = convw_ref[0:1, :] * u2 + convw_ref[1:2, :] * u1 + convw_ref[2:3, :] * u
        conv_out = _project(_after(hn, finished[3]), win_ref, CB) * conv * _silu(_project(hn, win_ref, CG))
        mixed_ref[:, 0:D_CONV] = conv_out.astype(bf16)
        u_ref[0:HIST_ROWS, :] = u_ref[tokens:tokens + HIST_ROWS, :]

        gate = _silu(_project(hn, win_ref, RG))
        for c in range(n_chunks):
            rows = slice(c * chunk, (c + 1) * chunk)
            for h in range(RET_HEADS):
                cols = slice(h * HEAD_DIM, (h + 1) * HEAD_DIM)
                hc = c * RET_HEADS + h
                lhs = jnp.concatenate([p_ref[hc], qx_ref[rows, cols]], axis=1)
                rhs = jnp.concatenate([v_ref[rows, cols], sbf_ref[hc]], axis=0)
                o = jnp.dot(lhs, rhs, preferred_element_type=f32)
                mu = jnp.mean(o, axis=-1, keepdims=True)
                d = o - mu
                var = jnp.mean(d * d, axis=-1, keepdims=True)
                y = d * lax.rsqrt(var + EPS) * retg_ref[:, cols]
                mixed_ref[rows, D_CONV + h * HEAD_DIM:D_CONV + (h + 1) * HEAD_DIM] = (
                    y * gate[rows, cols]).astype(bf16)

        y_ref[...] = jnp.dot(mixed_ref[...], wout_ref[...], preferred_element_type=f32) + x

    @pl.when(s == n_blocks)
    def _drain():
        _finish_previous_block()


def kernel(x, meta, norm1_g, w_in, conv_w, ret_norm_g, w_out, final_g):
    bsz, seq, d_model = x.shape
    assert d_model == D_MODEL and meta.shape == (N_META, D_MODEL)
    tokens, chunk = TOKENS_PER_STEP, RET_CHUNK
    assert seq % tokens == 0 and tokens % chunk == 0 and tokens <= STAGE_SLOTS * STAGE_ROWS

    cos, sin = _rope_tables(N_META + seq)
    decay, xi, zeta = _decay_tables(chunk)
    chunk_decay = tuple(float(v) for v in _gammas() ** chunk)
    n_hc = (tokens // chunk) * RET_HEADS

    blocks_per_seq = seq // tokens
    n_blocks = bsz * blocks_per_seq

    def mixed_block(s):
        s = jnp.minimum(s, n_blocks - 1)
        return s // blocks_per_seq, s % blocks_per_seq

    def finished_block(s):
        s = jnp.maximum(s - 1, 0)
        return s // blocks_per_seq, s % blocks_per_seq

    const = lambda *shape: pl.BlockSpec(shape, lambda s: (0,) * len(shape))
    rope_spec = pl.BlockSpec((tokens, HEAD_DIM), lambda s: (mixed_block(s)[1], 0))
    body = functools.partial(_mixer_kernel, tokens=tokens, chunk=chunk, chunk_decay=chunk_decay,
                             blocks_per_seq=blocks_per_seq, n_blocks=n_blocks)
    return pl.pallas_call(
        body,
        grid=(n_blocks + 1,),
        in_specs=[
            pl.BlockSpec((1, tokens, D_MODEL), lambda s: (*mixed_block(s), 0)),
            const(N_META, D_MODEL),
            const(1, D_MODEL),
            pl.BlockSpec(memory_space=pl.ANY),
            const(CONV_WIDTH, D_CONV),
            const(1, D_RET),
            pl.BlockSpec(memory_space=pl.ANY),
            const(1, D_MODEL),
            rope_spec,
            rope_spec,
            const(N_META, HEAD_DIM),
            const(N_META, HEAD_DIM),
            const(RET_HEADS, chunk, chunk),
            const(RET_HEADS, chunk, HEAD_DIM),
            const(RET_HEADS, chunk, HEAD_DIM),
            const(RET_HEADS, N_META, HEAD_DIM),
        ],
        out_specs=pl.BlockSpec((1, tokens, D_MODEL), lambda s: (*finished_block(s), 0)),
        out_shape=jax.ShapeDtypeStruct((bsz, seq, D_MODEL), x.dtype),
        scratch_shapes=[
            pltpu.VMEM((D_MODEL, 8 * SEC), jnp.bfloat16),
            pltpu.VMEM((D_MODEL, D_MODEL), jnp.bfloat16),
            pltpu.VMEM((STAGE_SLOTS * STAGE_ROWS, D_MODEL), jnp.float32),
            pltpu.SemaphoreType.DMA((STAGE_SLOTS,)),
            pltpu.VMEM((RET_HEADS, HEAD_DIM, HEAD_DIM), jnp.float32),
            pltpu.VMEM((RET_HEADS, HEAD_DIM, HEAD_DIM), jnp.float32),
            pltpu.VMEM((HIST_ROWS + tokens, D_CONV), jnp.float32),
            pltpu.VMEM((HIST_ROWS, D_CONV), jnp.float32),
            pltpu.VMEM((tokens, D_RET), jnp.bfloat16),
            pltpu.VMEM((tokens, D_RET), jnp.bfloat16),
            pltpu.VMEM((RET_HEADS, HEAD_DIM, tokens), jnp.bfloat16),
            pltpu.VMEM((tokens, D_RET), jnp.bfloat16),
            pltpu.VMEM((tokens, D_RET), jnp.bfloat16),
            pltpu.VMEM((n_hc, chunk, chunk), jnp.bfloat16),
            pltpu.VMEM((n_hc, HEAD_DIM, HEAD_DIM), jnp.bfloat16),
            pltpu.VMEM((tokens, D_MODEL), jnp.bfloat16),
        ],
        compiler_params=pltpu.CompilerParams(
            dimension_semantics=("arbitrary",),
            vmem_limit_bytes=VMEM_LIMIT_BYTES),
        name="hymba_mixer",
    )(x, meta, norm1_g.reshape(1, -1), w_in, conv_w,
      ret_norm_g.reshape(1, -1), w_out, final_g.reshape(1, -1),
      jnp.asarray(cos[N_META:]), jnp.asarray(sin[N_META:]),
      jnp.asarray(cos[:N_META]), jnp.asarray(sin[:N_META]),
      jnp.asarray(decay), jnp.asarray(xi), jnp.asarray(zeta), jnp.asarray(_meta_zeta()))
```

```python
import functools

import numpy as np
import jax
import jax.numpy as jnp
from jax import lax
from jax.experimental import pallas as pl
from jax.experimental.pallas import tpu as pltpu

D_MODEL = 1024
N_META = 16
D_CONV = 512
D_RET = 512
RET_HEADS = 4
HEAD_DIM = 128
HALF = HEAD_DIM // 2
CONV_WIDTH = 3
ROPE_BASE = 10000.0
EPS = 1e-6

SEC = 512
CX, CB, CC, CG, Q, K, V, RG = range(8)

TOKENS_PER_STEP = 1024
RET_CHUNK = 128
HIST_ROWS = 8
STAGE_ROWS = 256
STAGE_SLOTS = 8
PAR_G1, PAR_GF, PAR_RETG, PAR_CONVW, PAR_ROWS = 0, 1, 2, 3, 8
FINISH_GROUPS = 4
VMEM_LIMIT_BYTES = 58 * 1024 * 1024


def _gammas():
    return 1.0 - 2.0 ** (-5.0 - np.arange(RET_HEADS, dtype=np.float64))


def _rope_tables(n_pos):
    freqs = 1.0 / (ROPE_BASE ** (np.arange(HALF, dtype=np.float64) / HALF))
    ang = np.arange(n_pos, dtype=np.float64)[:, None] * freqs[None, :]
    cos = np.concatenate([np.cos(ang), np.cos(ang)], axis=1)
    sin = np.concatenate([-np.sin(ang), np.sin(ang)], axis=1)
    return cos.astype(np.float32), sin.astype(np.float32)


def _decay_tables(chunk):
    g = _gammas()
    idx = np.arange(chunk, dtype=np.float64)
    diff = idx[:, None] - idx[None, :]
    scale = HEAD_DIM ** -0.5
    decay = np.where(diff[None] >= 0, g[:, None, None] ** np.maximum(diff[None], 0.0), 0.0) * scale
    xi = (g[:, None] ** (idx[None, :] + 1.0)) * scale
    zeta = g[:, None] ** (chunk - 1.0 - idx[None, :])
    xi = np.broadcast_to(xi[:, :, None], (RET_HEADS, chunk, HEAD_DIM))
    zeta = np.broadcast_to(zeta[:, :, None], (RET_HEADS, chunk, HEAD_DIM))
    return decay.astype(np.float32), xi.astype(np.float32), zeta.astype(np.float32)


def _meta_zeta():
    g = _gammas()
    j = np.arange(N_META, dtype=np.float64)
    z = g[:, None] ** (N_META - 1.0 - j[None, :])
    return np.broadcast_to(z[:, :, None], (RET_HEADS, N_META, HEAD_DIM)).astype(np.float32)


def _rms_norm(x, g):
    return x * lax.rsqrt(jnp.mean(x * x, axis=-1, keepdims=True) + EPS) * g


def _silu(x):
    return x * (1.0 / (1.0 + jnp.exp(-x)))


def _rotary(t, cos, sin):
    return t * cos + pltpu.roll(t, HALF, axis=1) * sin


def _project(hn_bf16, win_ref, sec):
    return jnp.dot(hn_bf16, win_ref[:, sec * SEC:(sec + 1) * SEC], preferred_element_type=jnp.float32)


def _head(t, h):
    return t[:, h * HEAD_DIM:(h + 1) * HEAD_DIM]


def _zero_tile_after(stored_f32):
    bits = pltpu.bitcast(stored_f32, jnp.uint32)
    rows, cols = bits.shape
    acc = bits[0:8]
    for r in range(8, rows, 8):
        acc = acc | bits[r:r + 8]
    word = acc[:, 0:128]
    for c in range(128, cols, 128):
        word = word | acc[:, c:c + 128]
    return pltpu.bitcast((word >> 16) >> 16, jnp.bfloat16)


def _after(lhs_bf16, zero_tile):
    top = jnp.concatenate([lhs_bf16[0:16, 0:128] + zero_tile, lhs_bf16[0:16, 128:]], axis=1)
    return jnp.concatenate([top, lhs_bf16[16:]], axis=0)


def _weight_chunks(*weights):
    chunks = []
    for w_hbm, w_bf16_ref in weights:
        n_rows, n_cols = w_hbm.shape
        for c0 in range(0, n_cols, D_MODEL):
            for r0 in range(0, n_rows, STAGE_ROWS):
                chunks.append((w_hbm, w_bf16_ref, r0, c0))
    return chunks


def _stage_rows(idx):
    slot = idx % STAGE_SLOTS
    return pl.ds(slot * STAGE_ROWS, STAGE_ROWS)


def _stage_copy(chunks, idx, stage_ref, sem_ref):
    w_hbm, _, r0, c0 = chunks[idx]
    return pltpu.make_async_copy(w_hbm.at[pl.ds(r0, STAGE_ROWS), pl.ds(c0, D_MODEL)],
                                 stage_ref.at[_stage_rows(idx), :], sem_ref.at[idx % STAGE_SLOTS])


def _load_weights_as_bf16(chunks, stage_ref, sem_ref):
    for i in range(min(STAGE_SLOTS, len(chunks))):
        _stage_copy(chunks, i, stage_ref, sem_ref).start()
    for i, (_, w_bf16_ref, r0, c0) in enumerate(chunks):
        _stage_copy(chunks, i, stage_ref, sem_ref).wait()
        w_bf16_ref[r0:r0 + STAGE_ROWS, c0:c0 + D_MODEL] = stage_ref[_stage_rows(i), :].astype(jnp.bfloat16)
        if i + STAGE_SLOTS < len(chunks):
            _stage_copy(chunks, i + STAGE_SLOTS, stage_ref, sem_ref).start()


def _mixer_kernel(x_ref, meta_ref, g1_hbm, win_hbm, convw_hbm, retg_hbm, wout_hbm, gf_hbm,
                  cos_ref, sin_ref, mrope_ref, decay_ref, xi_ref, zeta_ref, mzeta_ref,
                  out_ref,
                  win_ref, wout_ref, ystage_ref, stage_sem, par_ref, par_sem,
                  state_ref, state0_ref, u_ref, hist0_ref, q_ref, qx_ref, k_ref, kz_ref, v_ref,
                  p_ref, sbf_ref, mixed_ref,
                  *, tokens, chunk, chunk_decay, blocks_per_seq, n_blocks):
    s = pl.program_id(0)
    bf16 = jnp.bfloat16
    f32 = jnp.float32
    y_ref = ystage_ref.at[0:tokens, :]
    g1_ref = par_ref.at[PAR_G1:PAR_G1 + 1, :]
    gf_ref = par_ref.at[PAR_GF:PAR_GF + 1, :]
    retg_ref = par_ref.at[PAR_RETG:PAR_RETG + 1, 0:D_RET]
    convw_ref = par_ref.at[PAR_CONVW:PAR_CONVW + CONV_WIDTH, 0:D_CONV]
    par_copies = [pltpu.make_async_copy(src, dst, par_sem.at[i]) for i, (src, dst) in enumerate(
        ((g1_hbm, g1_ref), (gf_hbm, gf_ref), (retg_hbm, retg_ref), (convw_hbm, convw_ref)))]

    def _finish_rows(rows):
        out_ref[0, rows, :] = _rms_norm(y_ref[rows, :], gf_ref[...])

    def _finish_previous_block():
        _finish_rows(slice(0, tokens))

    @pl.when(s == 0)
    def _first_step():
        for copy in par_copies:
            copy.start()
        _load_weights_as_bf16(_weight_chunks((win_hbm, win_ref), (wout_hbm, wout_ref)), ystage_ref, stage_sem)
        y_ref[...] = jnp.zeros((tokens, D_MODEL), f32)
        for copy in par_copies:
            copy.wait()
        hm = _rms_norm(meta_ref[...], g1_ref[...]).astype(bf16)
        u_m = _project(hm, win_ref, CC) * _project(hm, win_ref, CX)
        hist0_ref[...] = u_m[N_META - HIST_ROWS:, :]
        k_m = _project(hm, win_ref, K)
        v_m = _project(hm, win_ref, V).astype(bf16)
        for h in range(RET_HEADS):
            kr = _rotary(_head(k_m, h), mrope_ref[0:N_META, :], mrope_ref[N_META:, :]) * mzeta_ref[h]
            state0_ref[h] = lax.dot_general(kr.astype(bf16), _head(v_m, h), (((0,), (0,)), ((), ())),
                                            preferred_element_type=f32)

    @pl.when(jnp.logical_and(s % blocks_per_seq == 0, s < n_blocks))
    def _start_sequence():
        state_ref[...] = state0_ref[...]
        u_ref[0:HIST_ROWS, :] = hist0_ref[...]

    @pl.when(s < n_blocks)
    def _mix_block():
        group = tokens // FINISH_GROUPS
        finished = []
        for g in range(FINISH_GROUPS):
            rows = slice(g * group, (g + 1) * group)
            _finish_rows(rows)
            finished.append(_zero_tile_after(out_ref[0, rows, :]))
        x = x_ref[0]
        hn = _rms_norm(x, g1_ref[...]).astype(bf16)

        n_chunks = tokens // chunk

        cos = cos_ref[...]
        sin = sin_ref[...]
        q = _project(hn, win_ref, Q)
        k = _project(hn, win_ref, K)
        for h in range(RET_HEADS):
            cols = slice(h * HEAD_DIM, (h + 1) * HEAD_DIM)
            qr = _rotary(_head(q, h), cos, sin)
            kr = _rotary(_head(k, h), cos, sin)
            q_ref[:, cols] = qr.astype(bf16)
            k_ref[:, cols] = kr.astype(bf16)
            for c in range(n_chunks):
                rows = slice(c * chunk, (c + 1) * chunk)
                qx_ref[rows, cols] = (qr[rows] * xi_ref[h]).astype(bf16)
                kz_ref[rows, cols] = (kr[rows] * zeta_ref[h]).astype(bf16)
        v_ref[...] = _project(_after(hn, finished[0]), win_ref, V).astype(bf16)

        upd = {}
        for c in range(n_chunks):
            rows = slice(c * chunk, (c + 1) * chunk)
            for h in range(RET_HEADS):
                cols = slice(h * HEAD_DIM, (h + 1) * HEAD_DIM)
                scores = lax.dot_general(q_ref[rows, cols], k_ref[rows, cols], (((1,), (1,)), ((), ())),
                                         preferred_element_type=f32)
                p_ref[c * RET_HEADS + h] = (scores * decay_ref[h]).astype(bf16)
                upd[c, h] = lax.dot_general(kz_ref[rows, cols], v_ref[rows, cols], (((0,), (0,)), ((), ())),
                                            preferred_element_type=f32)
        for h in range(RET_HEADS):
            state = state_ref[h]
            for c in range(n_chunks):
                sbf_ref[c * RET_HEADS + h] = state.astype(bf16)
                state = chunk_decay[h] * state + upd[c, h]
            state_ref[h] = state

        u = _project(_after(hn, finished[1]), win_ref, CC) * _project(_after(hn, finished[2]), win_ref, CX)
        u_ref[HIST_ROWS:HIST_ROWS + tokens, :] = u
        u1 = u_ref[HIST_ROWS - 1:HIST_ROWS - 1 + tokens, :]
        u2 = u_ref[HIST_ROWS - 2:HIST_ROWS - 2 + tokens, :]
        conv = convw_ref[0:1, :] * u2 + convw_ref[1:2, :] * u1 + convw_ref[2:3, :] * u
        conv_out = _project(_after(hn, finished[3]), win_ref, CB) * conv * _silu(_project(hn, win_ref, CG))
        mixed_ref[:, 0:D_CONV] = conv_out.astype(bf16)
        u_ref[0:HIST_ROWS, :] = u_ref[tokens:tokens + HIST_ROWS, :]

        gate = _silu(_project(hn, win_ref, RG))
        for c in range(n_chunks):
            rows = slice(c * chunk, (c + 1) * chunk)
            for h in range(RET_HEADS):
                cols = slice(h * HEAD_DIM, (h + 1) * HEAD_DIM)
                hc = c * RET_HEADS + h
                lhs = jnp.concatenate([p_ref[hc], qx_ref[rows, cols]], axis=1)
                rhs = jnp.concatenate([v_ref[rows, cols], sbf_ref[hc]], axis=0)
                o = jnp.dot(lhs, rhs, preferred_element_type=f32)
                mu = jnp.mean(o, axis=-1, keepdims=True)
                d = o - mu
                var = jnp.mean(d * d, axis=-1, keepdims=True)
                y = d * lax.rsqrt(var + EPS) * retg_ref[:, cols]
                mixed_ref[rows, D_CONV + h * HEAD_DIM:D_CONV + (h + 1) * HEAD_DIM] = (
                    y * gate[rows, cols]).astype(bf16)

        y_ref[...] = jnp.dot(mixed_ref[...], wout_ref[...], preferred_element_type=f32) + x

    @pl.when(s == n_blocks)
    def _drain():
        _finish_previous_block()


def kernel(x, meta, norm1_g, w_in, conv_w, ret_norm_g, w_out, final_g):
    bsz, seq, d_model = x.shape
    assert d_model == D_MODEL and meta.shape == (N_META, D_MODEL)
    tokens, chunk = TOKENS_PER_STEP, RET_CHUNK
    assert seq % tokens == 0 and tokens % chunk == 0 and tokens <= STAGE_SLOTS * STAGE_ROWS

    cos, sin = _rope_tables(N_META + seq)
    decay, xi, zeta = _decay_tables(chunk)
    chunk_decay = tuple(float(v) for v in _gammas() ** chunk)
    n_hc = (tokens // chunk) * RET_HEADS

    blocks_per_seq = seq // tokens
    n_blocks = bsz * blocks_per_seq

    def mixed_block(s):
        s = jnp.minimum(s, n_blocks - 1)
        return s // blocks_per_seq, s % blocks_per_seq

    def finished_block(s):
        s = jnp.maximum(s - 1, 0)
        return s // blocks_per_seq, s % blocks_per_seq

    const = lambda *shape: pl.BlockSpec(shape, lambda s: (0,) * len(shape))
    rope_spec = pl.BlockSpec((tokens, HEAD_DIM), lambda s: (mixed_block(s)[1], 0))
    body = functools.partial(_mixer_kernel, tokens=tokens, chunk=chunk, chunk_decay=chunk_decay,
                             blocks_per_seq=blocks_per_seq, n_blocks=n_blocks)
    return pl.pallas_call(
        body,
        grid=(n_blocks + 1,),
        in_specs=[
            pl.BlockSpec((1, tokens, D_MODEL), lambda s: (*mixed_block(s), 0)),
            const(N_META, D_MODEL),
            pl.BlockSpec(memory_space=pl.ANY),
            pl.BlockSpec(memory_space=pl.ANY),
            pl.BlockSpec(memory_space=pl.ANY),
            pl.BlockSpec(memory_space=pl.ANY),
            pl.BlockSpec(memory_space=pl.ANY),
            pl.BlockSpec(memory_space=pl.ANY),
            rope_spec,
            rope_spec,
            const(2 * N_META, HEAD_DIM),
            const(RET_HEADS, chunk, chunk),
            const(RET_HEADS, chunk, HEAD_DIM),
            const(RET_HEADS, chunk, HEAD_DIM),
            const(RET_HEADS, N_META, HEAD_DIM),
        ],
        out_specs=pl.BlockSpec((1, tokens, D_MODEL), lambda s: (*finished_block(s), 0)),
        out_shape=jax.ShapeDtypeStruct((bsz, seq, D_MODEL), x.dtype),
        scratch_shapes=[
            pltpu.VMEM((D_MODEL, 8 * SEC), jnp.bfloat16),
            pltpu.VMEM((D_MODEL, D_MODEL), jnp.bfloat16),
            pltpu.VMEM((STAGE_SLOTS * STAGE_ROWS, D_MODEL), jnp.float32),
            pltpu.SemaphoreType.DMA((STAGE_SLOTS,)),
            pltpu.VMEM((PAR_ROWS, D_MODEL), jnp.float32),
            pltpu.SemaphoreType.DMA((4,)),
            pltpu.VMEM((RET_HEADS, HEAD_DIM, HEAD_DIM), jnp.float32),
            pltpu.VMEM((RET_HEADS, HEAD_DIM, HEAD_DIM), jnp.float32),
            pltpu.VMEM((HIST_ROWS + tokens, D_CONV), jnp.float32),
            pltpu.VMEM((HIST_ROWS, D_CONV), jnp.float32),
            pltpu.VMEM((tokens, D_RET), jnp.bfloat16),
            pltpu.VMEM((tokens, D_RET), jnp.bfloat16),
            pltpu.VMEM((tokens, D_RET), jnp.bfloat16),
            pltpu.VMEM((tokens, D_RET), jnp.bfloat16),
            pltpu.VMEM((tokens, D_RET), jnp.bfloat16),
            pltpu.VMEM((n_hc, chunk, chunk), jnp.bfloat16),
            pltpu.VMEM((n_hc, HEAD_DIM, HEAD_DIM), jnp.bfloat16),
            pltpu.VMEM((tokens, D_MODEL), jnp.bfloat16),
        ],
        compiler_params=pltpu.CompilerParams(
            dimension_semantics=("arbitrary",),
            vmem_limit_bytes=VMEM_LIMIT_BYTES),
        name="hymba_mixer",
    )(x, meta, norm1_g.reshape(1, -1), w_in, conv_w,
      ret_norm_g.reshape(1, -1), w_out, final_g.reshape(1, -1),
      jnp.asarray(cos[N_META:]), jnp.asarray(sin[N_META:]),
      jnp.asarray(np.concatenate([cos[:N_META], sin[:N_META]], axis=0)),
      jnp.asarray(decay), jnp.asarray(xi), jnp.asarray(zeta), jnp.asarray(_meta_zeta()))
```

```python
import functools

import numpy as np
import jax
import jax.numpy as jnp
from jax import lax
from jax.experimental import pallas as pl
from jax.experimental.pallas import tpu as pltpu

D_MODEL = 1024
N_META = 16
D_CONV = 512
D_RET = 512
RET_HEADS = 4
HEAD_DIM = 128
HALF = HEAD_DIM // 2
CONV_WIDTH = 3
ROPE_BASE = 10000.0
EPS = 1e-6

SEC = 512
CX, CB, CC, CG, Q, K, V, RG = range(8)

TOKENS_PER_STEP = 1024
RET_CHUNK = 128
HIST_ROWS = 8
STAGE_ROWS = 256
STAGE_SLOTS = 16
SECTION_ORDER = (Q, K, V, CC, CX, CB, CG, RG)
PAR_G1, PAR_GF, PAR_RETG, PAR_CONVW, PAR_ROWS = 0, 1, 2, 3, 8
FINISH_GROUPS = 4
VMEM_LIMIT_BYTES = 58 * 1024 * 1024


def _gammas():
    return 1.0 - 2.0 ** (-5.0 - np.arange(RET_HEADS, dtype=np.float64))


def _rope_tables(n_pos):
    freqs = 1.0 / (ROPE_BASE ** (np.arange(HALF, dtype=np.float64) / HALF))
    ang = np.arange(n_pos, dtype=np.float64)[:, None] * freqs[None, :]
    cos = np.concatenate([np.cos(ang), np.cos(ang)], axis=1)
    sin = np.concatenate([-np.sin(ang), np.sin(ang)], axis=1)
    return cos.astype(np.float32), sin.astype(np.float32)


def _decay_tables(chunk):
    g = _gammas()
    idx = np.arange(chunk, dtype=np.float64)
    diff = idx[:, None] - idx[None, :]
    scale = HEAD_DIM ** -0.5
    decay = np.where(diff[None] >= 0, g[:, None, None] ** np.maximum(diff[None], 0.0), 0.0) * scale
    xi = (g[:, None] ** (idx[None, :] + 1.0)) * scale
    zeta = g[:, None] ** (chunk - 1.0 - idx[None, :])
    xi = np.broadcast_to(xi[:, :, None], (RET_HEADS, chunk, HEAD_DIM))
    zeta = np.broadcast_to(zeta[:, :, None], (RET_HEADS, chunk, HEAD_DIM))
    return decay.astype(np.float32), xi.astype(np.float32), zeta.astype(np.float32)


def _meta_zeta():
    g = _gammas()
    j = np.arange(N_META, dtype=np.float64)
    z = g[:, None] ** (N_META - 1.0 - j[None, :])
    return np.broadcast_to(z[:, :, None], (RET_HEADS, N_META, HEAD_DIM)).astype(np.float32)


def _rms_norm(x, g):
    return x * lax.rsqrt(jnp.mean(x * x, axis=-1, keepdims=True) + EPS) * g


def _silu(x):
    return x * (1.0 / (1.0 + jnp.exp(-x)))


def _rotary(t, cos, sin):
    return t * cos + pltpu.roll(t, HALF, axis=1) * sin


def _project(hn_bf16, win_ref, sec):
    return jnp.dot(hn_bf16, win_ref[:, sec * SEC:(sec + 1) * SEC], preferred_element_type=jnp.float32)


def _head(t, h):
    return t[:, h * HEAD_DIM:(h + 1) * HEAD_DIM]


def _zero_tile_after(stored_f32):
    bits = pltpu.bitcast(stored_f32, jnp.uint32)
    rows, cols = bits.shape
    acc = bits[0:8]
    for r in range(8, rows, 8):
        acc = acc | bits[r:r + 8]
    word = acc[:, 0:128]
    for c in range(128, cols, 128):
        word = word | acc[:, c:c + 128]
    return pltpu.bitcast((word >> 16) >> 16, jnp.bfloat16)


def _after(lhs_bf16, zero_tile):
    top = jnp.concatenate([lhs_bf16[0:16, 0:128] + zero_tile, lhs_bf16[0:16, 128:]], axis=1)
    return jnp.concatenate([top, lhs_bf16[16:]], axis=0)


def _weight_pieces(win_hbm, win_ref, wout_hbm, wout_ref):
    pieces = []
    for sec in SECTION_ORDER:
        for r0 in range(0, D_MODEL, STAGE_ROWS):
            pieces.append((win_hbm, win_ref, r0, sec * SEC))
    for c0 in range(0, D_MODEL, SEC):
        for r0 in range(0, D_MODEL, STAGE_ROWS):
            pieces.append((wout_hbm, wout_ref, r0, c0))
    return pieces


class _WeightStream:
    def __init__(self, pieces, stage_ref, sem_ref):
        self.pieces, self.stage_ref, self.sem_ref = pieces, stage_ref, sem_ref
        self.started = 0
        self.converted = 0

    @staticmethod
    def _slot(i):
        slot = i % STAGE_SLOTS
        r0, c0 = (slot // 2) * STAGE_ROWS, (slot % 2) * SEC
        return slice(r0, r0 + STAGE_ROWS), slice(c0, c0 + SEC)

    def _copy(self, i):
        w_hbm, _, r0, c0 = self.pieces[i]
        rows, cols = self._slot(i)
        return pltpu.make_async_copy(w_hbm.at[r0:r0 + STAGE_ROWS, c0:c0 + SEC],
                                     self.stage_ref.at[rows, cols], self.sem_ref.at[i % STAGE_SLOTS])

    def _start_ahead(self):
        while self.started < min(len(self.pieces), self.converted + STAGE_SLOTS):
            self._copy(self.started).start()
            self.started += 1

    def need(self, n_pieces):
        self._start_ahead()
        while self.converted < n_pieces:
            i = self.converted
            self._copy(i).wait()
            _, w_bf16_ref, r0, c0 = self.pieces[i]
            rows, cols = self._slot(i)
            w_bf16_ref[r0:r0 + STAGE_ROWS, c0:c0 + SEC] = self.stage_ref[rows, cols].astype(jnp.bfloat16)
            self.converted += 1
            self._start_ahead()


def _mixer_kernel(x_ref, meta_ref, g1_hbm, win_hbm, convw_hbm, retg_hbm, wout_hbm, gf_hbm,
                  cos_ref, sin_ref, mrope_ref, decay_ref, xi_ref, zeta_ref, mzeta_ref,
                  out_ref,
                  win_ref, wout_ref, ystage_ref, stage_sem, par_ref, par_sem,
                  state_ref, state0_ref, u_ref, hist0_ref, q_ref, qx_ref, k_ref, kz_ref, v_ref,
                  p_ref, sbf_ref, mixed_ref,
                  *, tokens, chunk, chunk_decay, blocks_per_seq, n_blocks):
    s = pl.program_id(0)
    bf16 = jnp.bfloat16
    f32 = jnp.float32
    n_chunks = tokens // chunk
    y_ref = ystage_ref.at[0:tokens, :]
    g1_ref = par_ref.at[PAR_G1:PAR_G1 + 1, :]
    gf_ref = par_ref.at[PAR_GF:PAR_GF + 1, :]
    retg_ref = par_ref.at[PAR_RETG:PAR_RETG + 1, 0:D_RET]
    convw_ref = par_ref.at[PAR_CONVW:PAR_CONVW + CONV_WIDTH, 0:D_CONV]
    par_copies = [pltpu.make_async_copy(src, dst, par_sem.at[i]) for i, (src, dst) in enumerate(
        ((g1_hbm, g1_ref), (gf_hbm, gf_ref), (retg_hbm, retg_ref), (convw_hbm, convw_ref)))]

    def _finish_rows(rows):
        out_ref[0, rows, :] = _rms_norm(y_ref[rows, :], gf_ref[...])

    def _mix_block(stream):
        first = stream is not None
        if first:
            need = lambda n_sections: stream.need(
                min((n_sections + 1) * (D_MODEL // STAGE_ROWS), len(stream.pieces)))
            tied = lambda operand, g: operand
        else:
            group = tokens // FINISH_GROUPS
            finished = []
            for g in range(FINISH_GROUPS):
                rows = slice(g * group, (g + 1) * group)
                _finish_rows(rows)
                finished.append(_zero_tile_after(out_ref[0, rows, :]))
            need = lambda n_sections: None
            tied = lambda operand, g: _after(operand, finished[g])
        x = x_ref[0]
        hn = _rms_norm(x, g1_ref[...]).astype(bf16)

        cos = cos_ref[...]
        sin = sin_ref[...]
        need(1)
        q = _project(hn, win_ref, Q)
        need(2)
        k = _project(hn, win_ref, K)
        for h in range(RET_HEADS):
            cols = slice(h * HEAD_DIM, (h + 1) * HEAD_DIM)
            qr = _rotary(_head(q, h), cos, sin)
            kr = _rotary(_head(k, h), cos, sin)
            q_ref[:, cols] = qr.astype(bf16)
            k_ref[:, cols] = kr.astype(bf16)
            for c in range(n_chunks):
                rows = slice(c * chunk, (c + 1) * chunk)
                qx_ref[rows, cols] = (qr[rows] * xi_ref[h]).astype(bf16)
                kz_ref[rows, cols] = (kr[rows] * zeta_ref[h]).astype(bf16)
        need(3)
        v_ref[...] = _project(tied(hn, 0), win_ref, V).astype(bf16)

        if first:
            hm = _rms_norm(meta_ref[...], g1_ref[...]).astype(bf16)
            k_m = _project(hm, win_ref, K)
            v_m = _project(hm, win_ref, V).astype(bf16)
            for h in range(RET_HEADS):
                kr = _rotary(_head(k_m, h), mrope_ref[0:N_META, :], mrope_ref[N_META:, :]) * mzeta_ref[h]
                state0_ref[h] = lax.dot_general(kr.astype(bf16), _head(v_m, h), (((0,), (0,)), ((), ())),
                                                preferred_element_type=f32)
            state_ref[...] = state0_ref[...]

        upd = {}
        for c in range(n_chunks):
            rows = slice(c * chunk, (c + 1) * chunk)
            for h in range(RET_HEADS):
                cols = slice(h * HEAD_DIM, (h + 1) * HEAD_DIM)
                scores = lax.dot_general(q_ref[rows, cols], k_ref[rows, cols], (((1,), (1,)), ((), ())),
                                         preferred_element_type=f32)
                p_ref[c * RET_HEADS + h] = (scores * decay_ref[h]).astype(bf16)
                upd[c, h] = lax.dot_general(kz_ref[rows, cols], v_ref[rows, cols], (((0,), (0,)), ((), ())),
                                            preferred_element_type=f32)
        for h in range(RET_HEADS):
            state = state_ref[h]
            for c in range(n_chunks):
                sbf_ref[c * RET_HEADS + h] = state.astype(bf16)
                state = chunk_decay[h] * state + upd[c, h]
            state_ref[h] = state

        need(4)
        cc = _project(tied(hn, 1), win_ref, CC)
        need(5)
        u = cc * _project(tied(hn, 2), win_ref, CX)
        if first:
            u_m = _project(hm, win_ref, CC) * _project(hm, win_ref, CX)
            hist0_ref[...] = u_m[N_META - HIST_ROWS:, :]
            u_ref[0:HIST_ROWS, :] = hist0_ref[...]
        u_ref[HIST_ROWS:HIST_ROWS + tokens, :] = u
        u1 = u_ref[HIST_ROWS - 1:HIST_ROWS - 1 + tokens, :]
        u2 = u_ref[HIST_ROWS - 2:HIST_ROWS - 2 + tokens, :]
        conv = convw_ref[0:1, :] * u2 + convw_ref[1:2, :] * u1 + convw_ref[2:3, :] * u
        need(6)
        gated = _project(tied(hn, 3), win_ref, CB) * conv
        need(7)
        mixed_ref[:, 0:D_CONV] = (gated * _silu(_project(hn, win_ref, CG))).astype(bf16)
        u_ref[0:HIST_ROWS, :] = u_ref[tokens:tokens + HIST_ROWS, :]

        need(8)
        gate = _silu(_project(hn, win_ref, RG))
        for c in range(n_chunks):
            rows = slice(c * chunk, (c + 1) * chunk)
            for h in range(RET_HEADS):
                cols = slice(h * HEAD_DIM, (h + 1) * HEAD_DIM)
                hc = c * RET_HEADS + h
                lhs = jnp.concatenate([p_ref[hc], qx_ref[rows, cols]], axis=1)
                rhs = jnp.concatenate([v_ref[rows, cols], sbf_ref[hc]], axis=0)
                o = jnp.dot(lhs, rhs, preferred_element_type=f32)
                mu = jnp.mean(o, axis=-1, keepdims=True)
                d = o - mu
                var = jnp.mean(d * d, axis=-1, keepdims=True)
                y = d * lax.rsqrt(var + EPS) * retg_ref[:, cols]
                mixed_ref[rows, D_CONV + h * HEAD_DIM:D_CONV + (h + 1) * HEAD_DIM] = (
                    y * gate[rows, cols]).astype(bf16)

        need(len(SECTION_ORDER) + D_MODEL // SEC)
        y_ref[...] = jnp.dot(mixed_ref[...], wout_ref[...], preferred_element_type=f32) + x

    @pl.when(s == 0)
    def _first_block():
        for copy in par_copies:
            copy.start()
        stream = _WeightStream(_weight_pieces(win_hbm, win_ref, wout_hbm, wout_ref), ystage_ref, stage_sem)
        stream.need(0)
        for copy in par_copies:
            copy.wait()
        _mix_block(stream)

    @pl.when(jnp.logical_and(s % blocks_per_seq == 0, jnp.logical_and(s > 0, s < n_blocks)))
    def _start_sequence():
        state_ref[...] = state0_ref[...]
        u_ref[0:HIST_ROWS, :] = hist0_ref[...]

    @pl.when(jnp.logical_and(s > 0, s < n_blocks))
    def _later_block():
        _mix_block(None)

    @pl.when(s == n_blocks)
    def _drain():
        _finish_rows(slice(0, tokens))


def kernel(x, meta, norm1_g, w_in, conv_w, ret_norm_g, w_out, final_g):
    bsz, seq, d_model = x.shape
    assert d_model == D_MODEL and meta.shape == (N_META, D_MODEL)
    tokens, chunk = TOKENS_PER_STEP, RET_CHUNK
    assert seq % tokens == 0 and tokens % chunk == 0 and tokens <= STAGE_SLOTS // 2 * STAGE_ROWS

    cos, sin = _rope_tables(N_META + seq)
    decay, xi, zeta = _decay_tables(chunk)
    chunk_decay = tuple(float(v) for v in _gammas() ** chunk)
    n_hc = (tokens // chunk) * RET_HEADS

    blocks_per_seq = seq // tokens
    n_blocks = bsz * blocks_per_seq

    def mixed_block(s):
        s = jnp.minimum(s, n_blocks - 1)
        return s // blocks_per_seq, s % blocks_per_seq

    def finished_block(s):
        s = jnp.maximum(s - 1, 0)
        return s // blocks_per_seq, s % blocks_per_seq

    const = lambda *shape: pl.BlockSpec(shape, lambda s: (0,) * len(shape))
    rope_spec = pl.BlockSpec((tokens, HEAD_DIM), lambda s: (mixed_block(s)[1], 0))
    body = functools.partial(_mixer_kernel, tokens=tokens, chunk=chunk, chunk_decay=chunk_decay,
                             blocks_per_seq=blocks_per_seq, n_blocks=n_blocks)
    return pl.pallas_call(
        body,
        grid=(n_blocks + 1,),
        in_specs=[
            pl.BlockSpec((1, tokens, D_MODEL), lambda s: (*mixed_block(s), 0)),
            const(N_META, D_MODEL),
            pl.BlockSpec(memory_space=pl.ANY),
            pl.BlockSpec(memory_space=pl.ANY),
            pl.BlockSpec(memory_space=pl.ANY),
            pl.BlockSpec(memory_space=pl.ANY),
            pl.BlockSpec(memory_space=pl.ANY),
            pl.BlockSpec(memory_space=pl.ANY),
            rope_spec,
            rope_spec,
            const(2 * N_META, HEAD_DIM),
            const(RET_HEADS, chunk, chunk),
            const(RET_HEADS, chunk, HEAD_DIM),
            const(RET_HEADS, chunk, HEAD_DIM),
            const(RET_HEADS, N_META, HEAD_DIM),
        ],
        out_specs=pl.BlockSpec((1, tokens, D_MODEL), lambda s: (*finished_block(s), 0)),
        out_shape=jax.ShapeDtypeStruct((bsz, seq, D_MODEL), x.dtype),
        scratch_shapes=[
            pltpu.VMEM((D_MODEL, 8 * SEC), jnp.bfloat16),
            pltpu.VMEM((D_MODEL, D_MODEL), jnp.bfloat16),
            pltpu.VMEM((STAGE_SLOTS // 2 * STAGE_ROWS, D_MODEL), jnp.float32),
            pltpu.SemaphoreType.DMA((STAGE_SLOTS,)),
            pltpu.VMEM((PAR_ROWS, D_MODEL), jnp.float32),
            pltpu.SemaphoreType.DMA((4,)),
            pltpu.VMEM((RET_HEADS, HEAD_DIM, HEAD_DIM), jnp.float32),
            pltpu.VMEM((RET_HEADS, HEAD_DIM, HEAD_DIM), jnp.float32),
            pltpu.VMEM((HIST_ROWS + tokens, D_CONV), jnp.float32),
            pltpu.VMEM((HIST_ROWS, D_CONV), jnp.float32),
            pltpu.VMEM((tokens, D_RET), jnp.bfloat16),
            pltpu.VMEM((tokens, D_RET), jnp.bfloat16),
            pltpu.VMEM((tokens, D_RET), jnp.bfloat16),
            pltpu.VMEM((tokens, D_RET), jnp.bfloat16),
            pltpu.VMEM((tokens, D_RET), jnp.bfloat16),
            pltpu.VMEM((n_hc, chunk, chunk), jnp.bfloat16),
            pltpu.VMEM((n_hc, HEAD_DIM, HEAD_DIM), jnp.bfloat16),
            pltpu.VMEM((tokens, D_MODEL), jnp.bfloat16),
        ],
        compiler_params=pltpu.CompilerParams(
            dimension_semantics=("arbitrary",),
            vmem_limit_bytes=VMEM_LIMIT_BYTES),
        name="hymba_mixer",
    )(x, meta, norm1_g.reshape(1, -1), w_in, conv_w,
      ret_norm_g.reshape(1, -1), w_out, final_g.reshape(1, -1),
      jnp.asarray(cos[N_META:]), jnp.asarray(sin[N_META:]),
      jnp.asarray(np.concatenate([cos[:N_META], sin[:N_META]], axis=0)),
      jnp.asarray(decay), jnp.asarray(xi), jnp.asarray(zeta), jnp.asarray(_meta_zeta()))
```

```python
import functools

import numpy as np
import jax
import jax.numpy as jnp
from jax import lax
from jax.experimental import pallas as pl
from jax.experimental.pallas import tpu as pltpu

D_MODEL = 1024
N_META = 16
D_CONV = 512
D_RET = 512
RET_HEADS = 4
HEAD_DIM = 128
HALF = HEAD_DIM // 2
CONV_WIDTH = 3
ROPE_BASE = 10000.0
EPS = 1e-6

SEC = 512
CX, CB, CC, CG, Q, K, V, RG = range(8)

TOKENS_PER_STEP = 1024
RET_CHUNK = 128
HIST_ROWS = 8
STAGE_ROWS = 256
STAGE_SLOTS = 8
PAR_G1, PAR_GF, PAR_RETG, PAR_CONVW, PAR_ROWS = 0, 1, 2, 3, 8
FINISH_GROUPS = 4
VMEM_LIMIT_BYTES = 58 * 1024 * 1024


def _gammas():
    return 1.0 - 2.0 ** (-5.0 - np.arange(RET_HEADS, dtype=np.float64))


def _rope_tables(n_pos):
    freqs = 1.0 / (ROPE_BASE ** (np.arange(HALF, dtype=np.float64) / HALF))
    ang = np.arange(n_pos, dtype=np.float64)[:, None] * freqs[None, :]
    cos = np.concatenate([np.cos(ang), np.cos(ang)], axis=1)
    sin = np.concatenate([-np.sin(ang), np.sin(ang)], axis=1)
    return cos.astype(np.float32), sin.astype(np.float32)


def _decay_tables(chunk):
    g = _gammas()
    idx = np.arange(chunk, dtype=np.float64)
    diff = idx[:, None] - idx[None, :]
    scale = HEAD_DIM ** -0.5
    decay = np.where(diff[None] >= 0, g[:, None, None] ** np.maximum(diff[None], 0.0), 0.0) * scale
    xi = (g[:, None] ** (idx[None, :] + 1.0)) * scale
    zeta = g[:, None] ** (chunk - 1.0 - idx[None, :])
    xi = np.broadcast_to(xi[:, :, None], (RET_HEADS, chunk, HEAD_DIM))
    zeta = np.broadcast_to(zeta[:, :, None], (RET_HEADS, chunk, HEAD_DIM))
    return decay.astype(np.float32), xi.astype(np.float32), zeta.astype(np.float32)


def _meta_zeta():
    g = _gammas()
    j = np.arange(N_META, dtype=np.float64)
    z = g[:, None] ** (N_META - 1.0 - j[None, :])
    return np.broadcast_to(z[:, :, None], (RET_HEADS, N_META, HEAD_DIM)).astype(np.float32)


def _rms_norm(x, g):
    return x * lax.rsqrt(jnp.mean(x * x, axis=-1, keepdims=True) + EPS) * g


def _silu(x):
    return x * (1.0 / (1.0 + jnp.exp(-x)))


def _rotary(t, cos, sin):
    return t * cos + pltpu.roll(t, HALF, axis=1) * sin


def _project(hn_bf16, win_ref, sec):
    return jnp.dot(hn_bf16, win_ref[:, sec * SEC:(sec + 1) * SEC], preferred_element_type=jnp.float32)


def _head(t, h):
    return t[:, h * HEAD_DIM:(h + 1) * HEAD_DIM]


def _zero_tile_after(stored_f32):
    bits = pltpu.bitcast(stored_f32, jnp.uint32)
    rows, cols = bits.shape
    acc = bits[0:8]
    for r in range(8, rows, 8):
        acc = acc | bits[r:r + 8]
    word = acc[:, 0:128]
    for c in range(128, cols, 128):
        word = word | acc[:, c:c + 128]
    return pltpu.bitcast((word >> 16) >> 16, jnp.bfloat16)


def _after(lhs_bf16, zero_tile):
    top = jnp.concatenate([lhs_bf16[0:16, 0:128] + zero_tile, lhs_bf16[0:16, 128:]], axis=1)
    return jnp.concatenate([top, lhs_bf16[16:]], axis=0)


def _weight_chunks(*weights):
    chunks = []
    for w_hbm, w_bf16_ref in weights:
        n_rows, n_cols = w_hbm.shape
        for c0 in range(0, n_cols, D_MODEL):
            for r0 in range(0, n_rows, STAGE_ROWS):
                chunks.append((w_hbm, w_bf16_ref, r0, c0))
    return chunks


def _stage_rows(idx):
    slot = idx % STAGE_SLOTS
    return pl.ds(slot * STAGE_ROWS, STAGE_ROWS)


def _stage_copy(chunks, idx, stage_ref, sem_ref):
    w_hbm, _, r0, c0 = chunks[idx]
    return pltpu.make_async_copy(w_hbm.at[pl.ds(r0, STAGE_ROWS), pl.ds(c0, D_MODEL)],
                                 stage_ref.at[_stage_rows(idx), :], sem_ref.at[idx % STAGE_SLOTS])


def _load_weights_as_bf16(chunks, stage_ref, sem_ref):
    for i in range(min(STAGE_SLOTS, len(chunks))):
        _stage_copy(chunks, i, stage_ref, sem_ref).start()
    for i, (_, w_bf16_ref, r0, c0) in enumerate(chunks):
        _stage_copy(chunks, i, stage_ref, sem_ref).wait()
        w_bf16_ref[r0:r0 + STAGE_ROWS, c0:c0 + D_MODEL] = stage_ref[_stage_rows(i), :].astype(jnp.bfloat16)
        if i + STAGE_SLOTS < len(chunks):
            _stage_copy(chunks, i + STAGE_SLOTS, stage_ref, sem_ref).start()


def _mixer_kernel(x_ref, meta_ref, g1_hbm, win_hbm, convw_hbm, retg_hbm, wout_hbm, gf_hbm,
                  cos_ref, sin_ref, mrope_ref, decay_ref, xi_ref, zeta_ref, mzeta_ref,
                  out_ref,
                  win_ref, wout_ref, ystage_ref, stage_sem, par_ref, par_sem,
                  state_ref, state0_ref, u_ref, hist0_ref, q_ref, qx_ref, k_ref, kz_ref, v_ref,
                  p_ref, sbf_ref, mixed_ref,
                  *, tokens, chunk, chunk_decay, blocks_per_seq, n_blocks):
    s = pl.program_id(0)
    bf16 = jnp.bfloat16
    f32 = jnp.float32
    y_ref = ystage_ref.at[0:tokens, :]
    g1_ref = par_ref.at[PAR_G1:PAR_G1 + 1, :]
    gf_ref = par_ref.at[PAR_GF:PAR_GF + 1, :]
    retg_ref = par_ref.at[PAR_RETG:PAR_RETG + 1, 0:D_RET]
    convw_ref = par_ref.at[PAR_CONVW:PAR_CONVW + CONV_WIDTH, 0:D_CONV]
    par_copies = [pltpu.make_async_copy(src, dst, par_sem.at[i]) for i, (src, dst) in enumerate(
        ((g1_hbm, g1_ref), (gf_hbm, gf_ref), (retg_hbm, retg_ref), (convw_hbm, convw_ref)))]

    def _finish_rows(rows):
        out_ref[0, rows, :] = _rms_norm(y_ref[rows, :], gf_ref[...])

    def _finish_previous_block():
        _finish_rows(slice(0, tokens))

    @pl.when(s == 0)
    def _first_step():
        for copy in par_copies:
            copy.start()
        _load_weights_as_bf16(_weight_chunks((win_hbm, win_ref), (wout_hbm, wout_ref)), ystage_ref, stage_sem)
        y_ref[...] = jnp.zeros((tokens, D_MODEL), f32)
        for copy in par_copies:
            copy.wait()
        hm = _rms_norm(meta_ref[...], g1_ref[...]).astype(bf16)
        u_m = _project(hm, win_ref, CC) * _project(hm, win_ref, CX)
        hist0_ref[...] = u_m[N_META - HIST_ROWS:, :]
        k_m = _project(hm, win_ref, K)
        v_m = _project(hm, win_ref, V).astype(bf16)
        for h in range(RET_HEADS):
            kr = _rotary(_head(k_m, h), mrope_ref[0:N_META, :], mrope_ref[N_META:, :]) * mzeta_ref[h]
            state0_ref[h] = lax.dot_general(kr.astype(bf16), _head(v_m, h), (((0,), (0,)), ((), ())),
                                            preferred_element_type=f32)

    @pl.when(jnp.logical_and(s % blocks_per_seq == 0, s < n_blocks))
    def _start_sequence():
        state_ref[...] = state0_ref[...]
        u_ref[0:HIST_ROWS, :] = hist0_ref[...]

    @pl.when(s < n_blocks)
    def _mix_block():
        group = tokens // FINISH_GROUPS
        finished = []
        for g in range(FINISH_GROUPS):
            rows = slice(g * group, (g + 1) * group)
            _finish_rows(rows)
            finished.append(_zero_tile_after(out_ref[0, rows, :]))
        x = x_ref[0]
        hn = _rms_norm(x, g1_ref[...]).astype(bf16)

        n_chunks = tokens // chunk

        cos = cos_ref[...]
        sin = sin_ref[...]
        q = _project(hn, win_ref, Q)
        k = _project(hn, win_ref, K)
        for h in range(RET_HEADS):
            cols = slice(h * HEAD_DIM, (h + 1) * HEAD_DIM)
            qr = _rotary(_head(q, h), cos, sin)
            kr = _rotary(_head(k, h), cos, sin)
            q_ref[:, cols] = qr.astype(bf16)
            k_ref[:, cols] = kr.astype(bf16)
            for c in range(n_chunks):
                rows = slice(c * chunk, (c + 1) * chunk)
                qx_ref[rows, cols] = (qr[rows] * xi_ref[h]).astype(bf16)
                kz_ref[rows, cols] = (kr[rows] * zeta_ref[h]).astype(bf16)
        v_ref[...] = _project(hn, win_ref, V).astype(bf16)

        upd = {}
        for c in range(n_chunks):
            rows = slice(c * chunk, (c + 1) * chunk)
            for h in range(RET_HEADS):
                cols = slice(h * HEAD_DIM, (h + 1) * HEAD_DIM)
                scores = lax.dot_general(q_ref[rows, cols], k_ref[rows, cols], (((1,), (1,)), ((), ())),
                                         preferred_element_type=f32)
                p_ref[c * RET_HEADS + h] = (scores * decay_ref[h]).astype(bf16)
                upd[c, h] = lax.dot_general(kz_ref[rows, cols], v_ref[rows, cols], (((0,), (0,)), ((), ())),
                                            preferred_element_type=f32)
        for h in range(RET_HEADS):
            state = state_ref[h]
            for c in range(n_chunks):
                sbf_ref[c * RET_HEADS + h] = state.astype(bf16)
                state = chunk_decay[h] * state + upd[c, h]
            state_ref[h] = state

        u = _project(_after(hn, finished[0]), win_ref, CC) * _project(_after(hn, finished[1]), win_ref, CX)
        u_ref[HIST_ROWS:HIST_ROWS + tokens, :] = u
        u1 = u_ref[HIST_ROWS - 1:HIST_ROWS - 1 + tokens, :]
        u2 = u_ref[HIST_ROWS - 2:HIST_ROWS - 2 + tokens, :]
        conv = convw_ref[0:1, :] * u2 + convw_ref[1:2, :] * u1 + convw_ref[2:3, :] * u
        conv_out = _project(_after(hn, finished[2]), win_ref, CB) * conv * _silu(_project(_after(hn, finished[3]), win_ref, CG))
        mixed_ref[:, 0:D_CONV] = conv_out.astype(bf16)
        u_ref[0:HIST_ROWS, :] = u_ref[tokens:tokens + HIST_ROWS, :]

        gate = _silu(_project(hn, win_ref, RG))
        for c in range(n_chunks):
            rows = slice(c * chunk, (c + 1) * chunk)
            for h in range(RET_HEADS):
                cols = slice(h * HEAD_DIM, (h + 1) * HEAD_DIM)
                hc = c * RET_HEADS + h
                lhs = jnp.concatenate([p_ref[hc], qx_ref[rows, cols]], axis=1)
                rhs = jnp.concatenate([v_ref[rows, cols], sbf_ref[hc]], axis=0)
                o = jnp.dot(lhs, rhs, preferred_element_type=f32)
                mu = jnp.mean(o, axis=-1, keepdims=True)
                d = o - mu
                var = jnp.mean(d * d, axis=-1, keepdims=True)
                y = d * lax.rsqrt(var + EPS) * retg_ref[:, cols]
                mixed_ref[rows, D_CONV + h * HEAD_DIM:D_CONV + (h + 1) * HEAD_DIM] = (
                    y * gate[rows, cols]).astype(bf16)

        y_ref[...] = jnp.dot(mixed_ref[...], wout_ref[...], preferred_element_type=f32) + x

    @pl.when(s == n_blocks)
    def _drain():
        _finish_previous_block()


def kernel(x, meta, norm1_g, w_in, conv_w, ret_norm_g, w_out, final_g):
    bsz, seq, d_model = x.shape
    assert d_model == D_MODEL and meta.shape == (N_META, D_MODEL)
    tokens, chunk = TOKENS_PER_STEP, RET_CHUNK
    assert seq % tokens == 0 and tokens % chunk == 0 and tokens <= STAGE_SLOTS * STAGE_ROWS

    cos, sin = _rope_tables(N_META + seq)
    decay, xi, zeta = _decay_tables(chunk)
    chunk_decay = tuple(float(v) for v in _gammas() ** chunk)
    n_hc = (tokens // chunk) * RET_HEADS

    blocks_per_seq = seq // tokens
    n_blocks = bsz * blocks_per_seq

    def mixed_block(s):
        s = jnp.minimum(s, n_blocks - 1)
        return s // blocks_per_seq, s % blocks_per_seq

    def finished_block(s):
        s = jnp.maximum(s - 1, 0)
        return s // blocks_per_seq, s % blocks_per_seq

    const = lambda *shape: pl.BlockSpec(shape, lambda s: (0,) * len(shape))
    rope_spec = pl.BlockSpec((tokens, HEAD_DIM), lambda s: (mixed_block(s)[1], 0))
    body = functools.partial(_mixer_kernel, tokens=tokens, chunk=chunk, chunk_decay=chunk_decay,
                             blocks_per_seq=blocks_per_seq, n_blocks=n_blocks)
    return pl.pallas_call(
        body,
        grid=(n_blocks + 1,),
        in_specs=[
            pl.BlockSpec((1, tokens, D_MODEL), lambda s: (*mixed_block(s), 0)),
            const(N_META, D_MODEL),
            pl.BlockSpec(memory_space=pl.ANY),
            pl.BlockSpec(memory_space=pl.ANY),
            pl.BlockSpec(memory_space=pl.ANY),
            pl.BlockSpec(memory_space=pl.ANY),
            pl.BlockSpec(memory_space=pl.ANY),
            pl.BlockSpec(memory_space=pl.ANY),
            rope_spec,
            rope_spec,
            const(2 * N_META, HEAD_DIM),
            const(RET_HEADS, chunk, chunk),
            const(RET_HEADS, chunk, HEAD_DIM),
            const(RET_HEADS, chunk, HEAD_DIM),
            const(RET_HEADS, N_META, HEAD_DIM),
        ],
        out_specs=pl.BlockSpec((1, tokens, D_MODEL), lambda s: (*finished_block(s), 0)),
        out_shape=jax.ShapeDtypeStruct((bsz, seq, D_MODEL), x.dtype),
        scratch_shapes=[
            pltpu.VMEM((D_MODEL, 8 * SEC), jnp.bfloat16),
            pltpu.VMEM((D_MODEL, D_MODEL), jnp.bfloat16),
            pltpu.VMEM((STAGE_SLOTS * STAGE_ROWS, D_MODEL), jnp.float32),
            pltpu.SemaphoreType.DMA((STAGE_SLOTS,)),
            pltpu.VMEM((PAR_ROWS, D_MODEL), jnp.float32),
            pltpu.SemaphoreType.DMA((4,)),
            pltpu.VMEM((RET_HEADS, HEAD_DIM, HEAD_DIM), jnp.float32),
            pltpu.VMEM((RET_HEADS, HEAD_DIM, HEAD_DIM), jnp.float32),
            pltpu.VMEM((HIST_ROWS + tokens, D_CONV), jnp.float32),
            pltpu.VMEM((HIST_ROWS, D_CONV), jnp.float32),
            pltpu.VMEM((tokens, D_RET), jnp.bfloat16),
            pltpu.VMEM((tokens, D_RET), jnp.bfloat16),
            pltpu.VMEM((tokens, D_RET), jnp.bfloat16),
            pltpu.VMEM((tokens, D_RET), jnp.bfloat16),
            pltpu.VMEM((tokens, D_RET), jnp.bfloat16),
            pltpu.VMEM((n_hc, chunk, chunk), jnp.bfloat16),
            pltpu.VMEM((n_hc, HEAD_DIM, HEAD_DIM), jnp.bfloat16),
            pltpu.VMEM((tokens, D_MODEL), jnp.bfloat16),
        ],
        compiler_params=pltpu.CompilerParams(
            dimension_semantics=("arbitrary",),
            vmem_limit_bytes=VMEM_LIMIT_BYTES),
        name="hymba_mixer",
    )(x, meta, norm1_g.reshape(1, -1), w_in, conv_w,
      ret_norm_g.reshape(1, -1), w_out, final_g.reshape(1, -1),
      jnp.asarray(cos[N_META:]), jnp.asarray(sin[N_META:]),
      jnp.asarray(np.concatenate([cos[:N_META], sin[:N_META]], axis=0)),
      jnp.asarray(decay), jnp.asarray(xi), jnp.asarray(zeta), jnp.asarray(_meta_zeta()))
```

```python
import functools

import numpy as np
import jax
import jax.numpy as jnp
from jax import lax
from jax.experimental import pallas as pl
from jax.experimental.pallas import tpu as pltpu

D_MODEL = 1024
N_META = 16
D_CONV = 512
D_RET = 512
RET_HEADS = 4
HEAD_DIM = 128
HALF = HEAD_DIM // 2
CONV_WIDTH = 3
ROPE_BASE = 10000.0
EPS = 1e-6

SEC = 512
CX, CB, CC, CG, Q, K, V, RG = range(8)

TOKENS_PER_STEP = 1024
RET_CHUNK = 128
HIST_ROWS = 8
STAGE_ROWS = 256
STAGE_SLOTS = 8
PAR_G1, PAR_GF, PAR_RETG, PAR_CONVW, PAR_ROWS = 0, 1, 2, 3, 8
N_PAR_COPIES = 4
FINISH_GROUPS = 4
V7X_LANES = 128
V7X_F32_SUBLANES = 8
V7X_BF16_SUBLANES = 16
V7X_VMEM_BYTES = 64 * 1024 * 1024
VMEM_UNCLAIMED_BYTES = 6 * 1024 * 1024
VMEM_LIMIT_BYTES = V7X_VMEM_BYTES - VMEM_UNCLAIMED_BYTES


def _gammas():
    return 1.0 - 2.0 ** (-5.0 - np.arange(RET_HEADS, dtype=np.float64))


def _rope_tables(n_pos):
    freqs = 1.0 / (ROPE_BASE ** (np.arange(HALF, dtype=np.float64) / HALF))
    ang = np.arange(n_pos, dtype=np.float64)[:, None] * freqs[None, :]
    cos = np.concatenate([np.cos(ang), np.cos(ang)], axis=1)
    sin = np.concatenate([-np.sin(ang), np.sin(ang)], axis=1)
    return cos.astype(np.float32), sin.astype(np.float32)


def _decay_tables(chunk):
    g = _gammas()
    idx = np.arange(chunk, dtype=np.float64)
    diff = idx[:, None] - idx[None, :]
    scale = HEAD_DIM ** -0.5
    decay = np.where(diff[None] >= 0, g[:, None, None] ** np.maximum(diff[None], 0.0), 0.0) * scale
    xi = (g[:, None] ** (idx[None, :] + 1.0)) * scale
    zeta = g[:, None] ** (chunk - 1.0 - idx[None, :])
    xi = np.broadcast_to(xi[:, :, None], (RET_HEADS, chunk, HEAD_DIM))
    zeta = np.broadcast_to(zeta[:, :, None], (RET_HEADS, chunk, HEAD_DIM))
    return decay.astype(np.float32), xi.astype(np.float32), zeta.astype(np.float32)


def _meta_zeta():
    g = _gammas()
    j = np.arange(N_META, dtype=np.float64)
    z = g[:, None] ** (N_META - 1.0 - j[None, :])
    return np.broadcast_to(z[:, :, None], (RET_HEADS, N_META, HEAD_DIM)).astype(np.float32)


def _rms_norm(x, g):
    return x * lax.rsqrt(jnp.mean(x * x, axis=-1, keepdims=True) + EPS) * g


def _silu(x):
    return x * (1.0 / (1.0 + jnp.exp(-x)))


def _rotary(t, cos, sin):
    return t * cos + pltpu.roll(t, HALF, axis=1) * sin


def _project(hn_bf16, win_ref, sec):
    return jnp.dot(hn_bf16, win_ref[:, sec * SEC:(sec + 1) * SEC], preferred_element_type=jnp.float32)


def _head(t, h):
    return t[:, h * HEAD_DIM:(h + 1) * HEAD_DIM]


def _zero_tile_after(stored_f32):
    bits = pltpu.bitcast(stored_f32, jnp.uint32)
    rows, cols = bits.shape
    acc = bits[0:V7X_F32_SUBLANES]
    for r in range(V7X_F32_SUBLANES, rows, V7X_F32_SUBLANES):
        acc = acc | bits[r:r + V7X_F32_SUBLANES]
    word = acc[:, 0:V7X_LANES]
    for c in range(V7X_LANES, cols, V7X_LANES):
        word = word | acc[:, c:c + V7X_LANES]
    return pltpu.bitcast((word >> 16) >> 16, jnp.bfloat16)


def _after(lhs_bf16, zero_tile):
    r, c = V7X_BF16_SUBLANES, V7X_LANES
    top = jnp.concatenate([lhs_bf16[0:r, 0:c] + zero_tile, lhs_bf16[0:r, c:]], axis=1)
    return jnp.concatenate([top, lhs_bf16[r:]], axis=0)


def _weight_chunks(*weights):
    chunks = []
    for w_hbm, w_bf16_ref in weights:
        n_rows, n_cols = w_hbm.shape
        for c0 in range(0, n_cols, D_MODEL):
            for r0 in range(0, n_rows, STAGE_ROWS):
                chunks.append((w_hbm, w_bf16_ref, r0, c0))
    return chunks


def _stage_rows(idx):
    slot = idx % STAGE_SLOTS
    return pl.ds(slot * STAGE_ROWS, STAGE_ROWS)


def _stage_copy(chunks, idx, stage_ref, sem_ref):
    w_hbm, _, r0, c0 = chunks[idx]
    return pltpu.make_async_copy(w_hbm.at[pl.ds(r0, STAGE_ROWS), pl.ds(c0, D_MODEL)],
                                 stage_ref.at[_stage_rows(idx), :], sem_ref.at[idx % STAGE_SLOTS])


def _load_weights_as_bf16(chunks, stage_ref, sem_ref):
    for i in range(min(STAGE_SLOTS, len(chunks))):
        _stage_copy(chunks, i, stage_ref, sem_ref).start()
    for i, (_, w_bf16_ref, r0, c0) in enumerate(chunks):
        _stage_copy(chunks, i, stage_ref, sem_ref).wait()
        w_bf16_ref[r0:r0 + STAGE_ROWS, c0:c0 + D_MODEL] = stage_ref[_stage_rows(i), :].astype(jnp.bfloat16)
        if i + STAGE_SLOTS < len(chunks):
            _stage_copy(chunks, i + STAGE_SLOTS, stage_ref, sem_ref).start()


def _mixer_kernel(x_ref, meta_ref, g1_hbm, win_hbm, convw_hbm, retg_hbm, wout_hbm, gf_hbm,
                  cos_ref, sin_ref, mrope_ref, decay_ref, xi_ref, zeta_ref, mzeta_ref,
                  out_ref,
                  win_ref, wout_ref, ystage_ref, stage_sem, par_ref, par_sem,
                  state_ref, state0_ref, u_ref, hist0_ref, q_ref, qx_ref, k_ref, kz_ref, v_ref,
                  p_ref, sbf_ref, mixed_ref,
                  *, tokens, chunk, chunk_decay, blocks_per_seq, n_blocks):
    s = pl.program_id(0)
    bf16 = jnp.bfloat16
    f32 = jnp.float32
    y_ref = ystage_ref.at[0:tokens, :]
    g1_ref = par_ref.at[PAR_G1:PAR_G1 + 1, :]
    gf_ref = par_ref.at[PAR_GF:PAR_GF + 1, :]
    retg_ref = par_ref.at[PAR_RETG:PAR_RETG + 1, 0:D_RET]
    convw_ref = par_ref.at[PAR_CONVW:PAR_CONVW + CONV_WIDTH, 0:D_CONV]
    par_copies = [pltpu.make_async_copy(src, dst, par_sem.at[i]) for i, (src, dst) in enumerate(
        ((g1_hbm, g1_ref), (gf_hbm, gf_ref), (retg_hbm, retg_ref), (convw_hbm, convw_ref)))]

    def _finish_rows(rows):
        out_ref[0, rows, :] = _rms_norm(y_ref[rows, :], gf_ref[...])

    def _finish_previous_block():
        _finish_rows(slice(0, tokens))

    @pl.when(s == 0)
    def _first_step():
        for copy in par_copies:
            copy.start()
        _load_weights_as_bf16(_weight_chunks((win_hbm, win_ref), (wout_hbm, wout_ref)), ystage_ref, stage_sem)
        y_ref[...] = jnp.zeros((tokens, D_MODEL), f32)
        for copy in par_copies:
            copy.wait()
        hm = _rms_norm(meta_ref[...], g1_ref[...]).astype(bf16)
        u_m = _project(hm, win_ref, CC) * _project(hm, win_ref, CX)
        hist0_ref[...] = u_m[N_META - HIST_ROWS:, :]
        k_m = _project(hm, win_ref, K)
        v_m = _project(hm, win_ref, V).astype(bf16)
        for h in range(RET_HEADS):
            kr = _rotary(_head(k_m, h), mrope_ref[0:N_META, :], mrope_ref[N_META:, :]) * mzeta_ref[h]
            state0_ref[h] = lax.dot_general(kr.astype(bf16), _head(v_m, h), (((0,), (0,)), ((), ())),
                                            preferred_element_type=f32)

    @pl.when(jnp.logical_and(s % blocks_per_seq == 0, s < n_blocks))
    def _start_sequence():
        state_ref[...] = state0_ref[...]
        u_ref[0:HIST_ROWS, :] = hist0_ref[...]

    @pl.when(s < n_blocks)
    def _mix_block():
        group = tokens // FINISH_GROUPS
        finished = []
        for g in range(FINISH_GROUPS):
            rows = slice(g * group, (g + 1) * group)
            _finish_rows(rows)
            finished.append(_zero_tile_after(out_ref[0, rows, :]))
        x = x_ref[0]
        hn = _rms_norm(x, g1_ref[...]).astype(bf16)

        n_chunks = tokens // chunk

        cos = cos_ref[...]
        sin = sin_ref[...]
        q = _project(hn, win_ref, Q)
        k = _project(hn, win_ref, K)
        for h in range(RET_HEADS):
            cols = slice(h * HEAD_DIM, (h + 1) * HEAD_DIM)
            qr = _rotary(_head(q, h), cos, sin)
            kr = _rotary(_head(k, h), cos, sin)
            q_ref[:, cols] = qr.astype(bf16)
            k_ref[:, cols] = kr.astype(bf16)
            for c in range(n_chunks):
                rows = slice(c * chunk, (c + 1) * chunk)
                qx_ref[rows, cols] = (qr[rows] * xi_ref[h]).astype(bf16)
                kz_ref[rows, cols] = (kr[rows] * zeta_ref[h]).astype(bf16)
        v_ref[...] = _project(_after(hn, finished[0]), win_ref, V).astype(bf16)

        upd = {}
        for c in range(n_chunks):
            rows = slice(c * chunk, (c + 1) * chunk)
            for h in range(RET_HEADS):
                cols = slice(h * HEAD_DIM, (h + 1) * HEAD_DIM)
                scores = lax.dot_general(q_ref[rows, cols], k_ref[rows, cols], (((1,), (1,)), ((), ())),
                                         preferred_element_type=f32)
                p_ref[c * RET_HEADS + h] = (scores * decay_ref[h]).astype(bf16)
                upd[c, h] = lax.dot_general(kz_ref[rows, cols], v_ref[rows, cols], (((0,), (0,)), ((), ())),
                                            preferred_element_type=f32)
        for h in range(RET_HEADS):
            state = state_ref[h]
            for c in range(n_chunks):
                sbf_ref[c * RET_HEADS + h] = state.astype(bf16)
                state = chunk_decay[h] * state + upd[c, h]
            state_ref[h] = state

        u = _project(_after(hn, finished[1]), win_ref, CC) * _project(_after(hn, finished[2]), win_ref, CX)
        u_ref[HIST_ROWS:HIST_ROWS + tokens, :] = u
        u1 = u_ref[HIST_ROWS - 1:HIST_ROWS - 1 + tokens, :]
        u2 = u_ref[HIST_ROWS - 2:HIST_ROWS - 2 + tokens, :]
        conv = convw_ref[0:1, :] * u2 + convw_ref[1:2, :] * u1 + convw_ref[2:3, :] * u
        conv_out = _project(_after(hn, finished[3]), win_ref, CB) * conv * _silu(_project(hn, win_ref, CG))
        mixed_ref[:, 0:D_CONV] = conv_out.astype(bf16)
        u_ref[0:HIST_ROWS, :] = u_ref[tokens:tokens + HIST_ROWS, :]

        gate = _silu(_project(hn, win_ref, RG))
        for c in range(n_chunks):
            rows = slice(c * chunk, (c + 1) * chunk)
            for h in range(RET_HEADS):
                cols = slice(h * HEAD_DIM, (h + 1) * HEAD_DIM)
                hc = c * RET_HEADS + h
                lhs = jnp.concatenate([p_ref[hc], qx_ref[rows, cols]], axis=1)
                rhs = jnp.concatenate([v_ref[rows, cols], sbf_ref[hc]], axis=0)
                o = jnp.dot(lhs, rhs, preferred_element_type=f32)
                mu = jnp.mean(o, axis=-1, keepdims=True)
                d = o - mu
                var = jnp.mean(d * d, axis=-1, keepdims=True)
                y = d * lax.rsqrt(var + EPS) * retg_ref[:, cols]
                mixed_ref[rows, D_CONV + h * HEAD_DIM:D_CONV + (h + 1) * HEAD_DIM] = (
                    y * gate[rows, cols]).astype(bf16)

        y_ref[...] = jnp.dot(mixed_ref[...], wout_ref[...], preferred_element_type=f32) + x

    @pl.when(s == n_blocks)
    def _drain():
        _finish_previous_block()


def kernel(x, meta, norm1_g, w_in, conv_w, ret_norm_g, w_out, final_g):
    bsz, seq, d_model = x.shape
    assert d_model == D_MODEL and meta.shape == (N_META, D_MODEL)
    tokens, chunk = TOKENS_PER_STEP, RET_CHUNK
    assert seq % tokens == 0 and tokens % chunk == 0 and tokens <= STAGE_SLOTS * STAGE_ROWS

    cos, sin = _rope_tables(N_META + seq)
    decay, xi, zeta = _decay_tables(chunk)
    chunk_decay = tuple(float(v) for v in _gammas() ** chunk)
    n_hc = (tokens // chunk) * RET_HEADS

    blocks_per_seq = seq // tokens
    n_blocks = bsz * blocks_per_seq

    def mixed_block(s):
        s = jnp.minimum(s, n_blocks - 1)
        return s // blocks_per_seq, s % blocks_per_seq

    def finished_block(s):
        s = jnp.maximum(s - 1, 0)
        return s // blocks_per_seq, s % blocks_per_seq

    const = lambda *shape: pl.BlockSpec(shape, lambda s: (0,) * len(shape))
    rope_spec = pl.BlockSpec((tokens, HEAD_DIM), lambda s: (mixed_block(s)[1], 0))
    body = functools.partial(_mixer_kernel, tokens=tokens, chunk=chunk, chunk_decay=chunk_decay,
                             blocks_per_seq=blocks_per_seq, n_blocks=n_blocks)
    return pl.pallas_call(
        body,
        grid=(n_blocks + 1,),
        in_specs=[
            pl.BlockSpec((1, tokens, D_MODEL), lambda s: (*mixed_block(s), 0)),
            const(N_META, D_MODEL),
            pl.BlockSpec(memory_space=pl.ANY),
            pl.BlockSpec(memory_space=pl.ANY),
            pl.BlockSpec(memory_space=pl.ANY),
            pl.BlockSpec(memory_space=pl.ANY),
            pl.BlockSpec(memory_space=pl.ANY),
            pl.BlockSpec(memory_space=pl.ANY),
            rope_spec,
            rope_spec,
            const(2 * N_META, HEAD_DIM),
            const(RET_HEADS, chunk, chunk),
            const(RET_HEADS, chunk, HEAD_DIM),
            const(RET_HEADS, chunk, HEAD_DIM),
            const(RET_HEADS, N_META, HEAD_DIM),
        ],
        out_specs=pl.BlockSpec((1, tokens, D_MODEL), lambda s: (*finished_block(s), 0)),
        out_shape=jax.ShapeDtypeStruct((bsz, seq, D_MODEL), x.dtype),
        scratch_shapes=[
            pltpu.VMEM((D_MODEL, 8 * SEC), jnp.bfloat16),
            pltpu.VMEM((D_MODEL, D_MODEL), jnp.bfloat16),
            pltpu.VMEM((STAGE_SLOTS * STAGE_ROWS, D_MODEL), jnp.float32),
            pltpu.SemaphoreType.DMA((STAGE_SLOTS,)),
            pltpu.VMEM((PAR_ROWS, D_MODEL), jnp.float32),
            pltpu.SemaphoreType.DMA((N_PAR_COPIES,)),
            pltpu.VMEM((RET_HEADS, HEAD_DIM, HEAD_DIM), jnp.float32),
            pltpu.VMEM((RET_HEADS, HEAD_DIM, HEAD_DIM), jnp.float32),
            pltpu.VMEM((HIST_ROWS + tokens, D_CONV), jnp.float32),
            pltpu.VMEM((HIST_ROWS, D_CONV), jnp.float32),
            pltpu.VMEM((tokens, D_RET), jnp.bfloat16),
            pltpu.VMEM((tokens, D_RET), jnp.bfloat16),
            pltpu.VMEM((tokens, D_RET), jnp.bfloat16),
            pltpu.VMEM((tokens, D_RET), jnp.bfloat16),
            pltpu.VMEM((tokens, D_RET), jnp.bfloat16),
            pltpu.VMEM((n_hc, chunk, chunk), jnp.bfloat16),
            pltpu.VMEM((n_hc, HEAD_DIM, HEAD_DIM), jnp.bfloat16),
            pltpu.VMEM((tokens, D_MODEL), jnp.bfloat16),
        ],
        compiler_params=pltpu.CompilerParams(
            dimension_semantics=("arbitrary",),
            vmem_limit_bytes=VMEM_LIMIT_BYTES),
        name="hymba_mixer",
    )(x, meta, norm1_g.reshape(1, -1), w_in, conv_w,
      ret_norm_g.reshape(1, -1), w_out, final_g.reshape(1, -1),
      jnp.asarray(cos[N_META:]), jnp.asarray(sin[N_META:]),
      jnp.asarray(np.concatenate([cos[:N_META], sin[:N_META]], axis=0)),
      jnp.asarray(decay), jnp.asarray(xi), jnp.asarray(zeta), jnp.asarray(_meta_zeta()))
```

```python
import functools

import numpy as np
import jax
import jax.numpy as jnp
from jax import lax
from jax.experimental import pallas as pl
from jax.experimental.pallas import tpu as pltpu

D_MODEL = 1024
N_META = 16
D_CONV = 512
D_RET = 512
RET_HEADS = 4
HEAD_DIM = 128
HALF = HEAD_DIM // 2
CONV_WIDTH = 3
ROPE_BASE = 10000.0
EPS = 1e-6

SEC = 512
CX, CB, CC, CG, Q, K, V, RG = range(8)

TOKENS_PER_STEP = 1024
RET_CHUNK = 128
HIST_ROWS = 8
STAGE_ROWS = 256
STAGE_SLOTS = 8
PAR_G1, PAR_GF, PAR_RETG, PAR_CONVW, PAR_ROWS = 0, 1, 2, 3, 8
N_PAR_COPIES = 4
FINISH_GROUPS = 4
V7X_LANES = 128
V7X_F32_SUBLANES = 8
V7X_BF16_SUBLANES = 16
V7X_VMEM_BYTES = 64 * 1024 * 1024
VMEM_UNCLAIMED_BYTES = 6 * 1024 * 1024
VMEM_LIMIT_BYTES = V7X_VMEM_BYTES - VMEM_UNCLAIMED_BYTES


def _gammas():
    return 1.0 - 2.0 ** (-5.0 - np.arange(RET_HEADS, dtype=np.float64))


def _rope_tables(n_pos):
    freqs = 1.0 / (ROPE_BASE ** (np.arange(HALF, dtype=np.float64) / HALF))
    ang = np.arange(n_pos, dtype=np.float64)[:, None] * freqs[None, :]
    cos = np.concatenate([np.cos(ang), np.cos(ang)], axis=1)
    sin = np.concatenate([-np.sin(ang), np.sin(ang)], axis=1)
    return cos.astype(np.float32), sin.astype(np.float32)


def _decay_tables(chunk):
    g = _gammas()
    idx = np.arange(chunk, dtype=np.float64)
    diff = idx[:, None] - idx[None, :]
    scale = HEAD_DIM ** -0.5
    decay = np.where(diff[None] >= 0, g[:, None, None] ** np.maximum(diff[None], 0.0), 0.0) * scale
    xi = (g[:, None] ** (idx[None, :] + 1.0)) * scale
    zeta = g[:, None] ** (chunk - 1.0 - idx[None, :])
    xi = np.broadcast_to(xi[:, :, None], (RET_HEADS, chunk, HEAD_DIM))
    zeta = np.broadcast_to(zeta[:, :, None], (RET_HEADS, chunk, HEAD_DIM))
    return decay.astype(np.float32), xi.astype(np.float32), zeta.astype(np.float32)


def _meta_zeta():
    g = _gammas()
    j = np.arange(N_META, dtype=np.float64)
    z = g[:, None] ** (N_META - 1.0 - j[None, :])
    return np.broadcast_to(z[:, :, None], (RET_HEADS, N_META, HEAD_DIM)).astype(np.float32)


def _rms_norm(x, g):
    return x * lax.rsqrt(jnp.mean(x * x, axis=-1, keepdims=True) + EPS) * g


def _silu(x):
    return x * (1.0 / (1.0 + jnp.exp(-x)))


def _rotary(t, cos, sin):
    return t * cos + pltpu.roll(t, HALF, axis=1) * sin


def _project(hn_bf16, win_ref, sec):
    return jnp.dot(hn_bf16, win_ref[:, sec * SEC:(sec + 1) * SEC], preferred_element_type=jnp.float32)


def _head(t, h):
    return t[:, h * HEAD_DIM:(h + 1) * HEAD_DIM]


def _zero_tile_after(stored_f32):
    bits = pltpu.bitcast(stored_f32, jnp.uint32)
    rows, cols = bits.shape
    acc = bits[0:V7X_F32_SUBLANES]
    for r in range(V7X_F32_SUBLANES, rows, V7X_F32_SUBLANES):
        acc = acc | bits[r:r + V7X_F32_SUBLANES]
    word = acc[:, 0:V7X_LANES]
    for c in range(V7X_LANES, cols, V7X_LANES):
        word = word | acc[:, c:c + V7X_LANES]
    return pltpu.bitcast((word >> 16) >> 16, jnp.bfloat16)


def _after(lhs_bf16, zero_tile):
    r, c = V7X_BF16_SUBLANES, V7X_LANES
    top = jnp.concatenate([lhs_bf16[0:r, 0:c] + zero_tile, lhs_bf16[0:r, c:]], axis=1)
    return jnp.concatenate([top, lhs_bf16[r:]], axis=0)


def _weight_chunks(*weights):
    chunks = []
    for w_hbm, w_bf16_ref in weights:
        n_rows, n_cols = w_hbm.shape
        for c0 in range(0, n_cols, D_MODEL):
            for r0 in range(0, n_rows, STAGE_ROWS):
                chunks.append((w_hbm, w_bf16_ref, r0, c0))
    return chunks


def _stage_rows(idx):
    slot = idx % STAGE_SLOTS
    return pl.ds(slot * STAGE_ROWS, STAGE_ROWS)


def _stage_copy(chunks, idx, stage_ref, sem_ref):
    w_hbm, _, r0, c0 = chunks[idx]
    return pltpu.make_async_copy(w_hbm.at[pl.ds(r0, STAGE_ROWS), pl.ds(c0, D_MODEL)],
                                 stage_ref.at[_stage_rows(idx), :], sem_ref.at[idx % STAGE_SLOTS])


def _load_weights_as_bf16(chunks, stage_ref, sem_ref):
    for i in range(min(STAGE_SLOTS, len(chunks))):
        _stage_copy(chunks, i, stage_ref, sem_ref).start()
    for i, (_, w_bf16_ref, r0, c0) in enumerate(chunks):
        _stage_copy(chunks, i, stage_ref, sem_ref).wait()
        w_bf16_ref[r0:r0 + STAGE_ROWS, c0:c0 + D_MODEL] = stage_ref[_stage_rows(i), :].astype(jnp.bfloat16)
        if i + STAGE_SLOTS < len(chunks):
            _stage_copy(chunks, i + STAGE_SLOTS, stage_ref, sem_ref).start()


def _mixer_kernel(x_ref, meta_ref, g1_hbm, win_hbm, convw_hbm, retg_hbm, wout_hbm, gf_hbm,
                  cos_ref, sin_ref, mrope_ref, decay_ref, xi_ref, zeta_ref, mzeta_ref,
                  out_ref,
                  win_ref, wout_ref, ystage_ref, stage_sem, par_ref, par_sem,
                  state_ref, state0_ref, u_ref, hist0_ref, q_ref, qx_ref, k_ref, kz_ref, v_ref,
                  p_ref, sbf_ref, mixed_ref,
                  *, tokens, chunk, chunk_decay, blocks_per_seq, n_blocks):
    s = pl.program_id(0)
    bf16 = jnp.bfloat16
    f32 = jnp.float32
    y_ref = ystage_ref.at[0:tokens, :]
    g1_ref = par_ref.at[PAR_G1:PAR_G1 + 1, :]
    gf_ref = par_ref.at[PAR_GF:PAR_GF + 1, :]
    retg_ref = par_ref.at[PAR_RETG:PAR_RETG + 1, 0:D_RET]
    convw_ref = par_ref.at[PAR_CONVW:PAR_CONVW + CONV_WIDTH, 0:D_CONV]
    par_copies = [pltpu.make_async_copy(src, dst, par_sem.at[i]) for i, (src, dst) in enumerate(
        ((g1_hbm, g1_ref), (gf_hbm, gf_ref), (retg_hbm, retg_ref), (convw_hbm, convw_ref)))]

    def _finish_rows(rows):
        out_ref[0, rows, :] = _rms_norm(y_ref[rows, :], gf_ref[...])

    def _finish_previous_block():
        _finish_rows(slice(0, tokens))

    @pl.when(s == 0)
    def _first_step():
        for copy in par_copies:
            copy.start()
        _load_weights_as_bf16(_weight_chunks((win_hbm, win_ref), (wout_hbm, wout_ref)), ystage_ref, stage_sem)
        y_ref[...] = jnp.zeros((tokens, D_MODEL), f32)
        for copy in par_copies:
            copy.wait()
        hm = _rms_norm(meta_ref[...], g1_ref[...]).astype(bf16)
        u_m = _project(hm, win_ref, CC) * _project(hm, win_ref, CX)
        hist0_ref[:, 0:D_CONV] = u_m[N_META - HIST_ROWS:, :]
        k_m = _project(hm, win_ref, K)
        v_m = _project(hm, win_ref, V).astype(bf16)
        for h in range(RET_HEADS):
            kr = _rotary(_head(k_m, h), mrope_ref[0:N_META, :], mrope_ref[N_META:, :]) * mzeta_ref[h]
            state0_ref[h] = lax.dot_general(kr.astype(bf16), _head(v_m, h), (((0,), (0,)), ((), ())),
                                            preferred_element_type=f32)

    @pl.when(jnp.logical_and(s % blocks_per_seq == 0, s < n_blocks))
    def _start_sequence():
        state_ref[...] = state0_ref[...]
        u_ref[0:HIST_ROWS, :] = hist0_ref[:, 0:D_CONV]

    @pl.when(s < n_blocks)
    def _mix_block():
        group = tokens // FINISH_GROUPS
        finished = []
        for g in range(FINISH_GROUPS):
            rows = slice(g * group, (g + 1) * group)
            _finish_rows(rows)
            finished.append(_zero_tile_after(out_ref[0, rows, :]))
        x = x_ref[0]
        hn = _rms_norm(x, g1_ref[...]).astype(bf16)

        n_chunks = tokens // chunk

        cos = cos_ref[...]
        sin = sin_ref[...]
        q = _project(hn, win_ref, Q)
        k = _project(hn, win_ref, K)
        for h in range(RET_HEADS):
            cols = slice(h * HEAD_DIM, (h + 1) * HEAD_DIM)
            qr = _rotary(_head(q, h), cos, sin)
            kr = _rotary(_head(k, h), cos, sin)
            q_ref[:, cols] = qr.astype(bf16)
            k_ref[:, cols] = kr.astype(bf16)
            for c in range(n_chunks):
                rows = slice(c * chunk, (c + 1) * chunk)
                qx_ref[rows, cols] = (qr[rows] * xi_ref[h]).astype(bf16)
                kz_ref[rows, cols] = (kr[rows] * zeta_ref[h]).astype(bf16)
        v_ref[...] = _project(_after(hn, finished[0]), win_ref, V).astype(bf16)

        upd = {}
        for c in range(n_chunks):
            rows = slice(c * chunk, (c + 1) * chunk)
            for h in range(RET_HEADS):
                cols = slice(h * HEAD_DIM, (h + 1) * HEAD_DIM)
                scores = lax.dot_general(q_ref[rows, cols], k_ref[rows, cols], (((1,), (1,)), ((), ())),
                                         preferred_element_type=f32)
                p_ref[c * RET_HEADS + h] = (scores * decay_ref[h]).astype(bf16)
                upd[c, h] = lax.dot_general(kz_ref[rows, cols], v_ref[rows, cols], (((0,), (0,)), ((), ())),
                                            preferred_element_type=f32)
        for h in range(RET_HEADS):
            state = state_ref[h]
            for c in range(n_chunks):
                sbf_ref[c * RET_HEADS + h] = state.astype(bf16)
                state = chunk_decay[h] * state + upd[c, h]
            state_ref[h] = state

        u = _project(_after(hn, finished[1]), win_ref, CC) * _project(_after(hn, finished[2]), win_ref, CX)
        u_ref[HIST_ROWS:HIST_ROWS + tokens, :] = u
        u1 = u_ref[HIST_ROWS - 1:HIST_ROWS - 1 + tokens, :]
        u2 = u_ref[HIST_ROWS - 2:HIST_ROWS - 2 + tokens, :]
        conv = convw_ref[0:1, :] * u2 + convw_ref[1:2, :] * u1 + convw_ref[2:3, :] * u
        conv_out = _project(_after(hn, finished[3]), win_ref, CB) * conv * _silu(_project(hn, win_ref, CG))
        mixed_ref[:, 0:D_CONV] = conv_out.astype(bf16)
        u_ref[0:HIST_ROWS, :] = u_ref[tokens:tokens + HIST_ROWS, :]

        gate = _silu(_project(hn, win_ref, RG))
        for c in range(n_chunks):
            rows = slice(c * chunk, (c + 1) * chunk)
            for h in range(RET_HEADS):
                cols = slice(h * HEAD_DIM, (h + 1) * HEAD_DIM)
                hc = c * RET_HEADS + h
                lhs = jnp.concatenate([p_ref[hc], qx_ref[rows, cols]], axis=1)
                rhs = jnp.concatenate([v_ref[rows, cols], sbf_ref[hc]], axis=0)
                o = jnp.dot(lhs, rhs, preferred_element_type=f32)
                mu = jnp.mean(o, axis=-1, keepdims=True)
                d = o - mu
                var = jnp.mean(d * d, axis=-1, keepdims=True)
                y = d * lax.rsqrt(var + EPS) * retg_ref[:, cols]
                mixed_ref[rows, D_CONV + h * HEAD_DIM:D_CONV + (h + 1) * HEAD_DIM] = (
                    y * gate[rows, cols]).astype(bf16)

        y_ref[...] = jnp.dot(mixed_ref[...], wout_ref[...], preferred_element_type=f32) + x

    @pl.when(s == n_blocks)
    def _drain():
        _finish_previous_block()


def kernel(x, meta, norm1_g, w_in, conv_w, ret_norm_g, w_out, final_g):
    bsz, seq, d_model = x.shape
    assert d_model == D_MODEL and meta.shape == (N_META, D_MODEL)
    tokens, chunk = TOKENS_PER_STEP, RET_CHUNK
    assert seq % tokens == 0 and tokens % chunk == 0 and tokens <= STAGE_SLOTS * STAGE_ROWS

    cos, sin = _rope_tables(N_META + seq)
    decay, xi, zeta = _decay_tables(chunk)
    chunk_decay = tuple(float(v) for v in _gammas() ** chunk)
    n_hc = (tokens // chunk) * RET_HEADS

    blocks_per_seq = seq // tokens
    n_blocks = bsz * blocks_per_seq

    def mixed_block(s):
        s = jnp.minimum(s, n_blocks - 1)
        return s // blocks_per_seq, s % blocks_per_seq

    def finished_block(s):
        s = jnp.maximum(s - 1, 0)
        return s // blocks_per_seq, s % blocks_per_seq

    const = lambda *shape: pl.BlockSpec(shape, lambda s: (0,) * len(shape))
    rope_spec = pl.BlockSpec((tokens, HEAD_DIM), lambda s: (mixed_block(s)[1], 0))
    body = functools.partial(_mixer_kernel, tokens=tokens, chunk=chunk, chunk_decay=chunk_decay,
                             blocks_per_seq=blocks_per_seq, n_blocks=n_blocks)
    return pl.pallas_call(
        body,
        grid=(n_blocks + 1,),
        in_specs=[
            pl.BlockSpec((1, tokens, D_MODEL), lambda s: (*mixed_block(s), 0)),
            const(N_META, D_MODEL),
            pl.BlockSpec(memory_space=pl.ANY),
            pl.BlockSpec(memory_space=pl.ANY),
            pl.BlockSpec(memory_space=pl.ANY),
            pl.BlockSpec(memory_space=pl.ANY),
            pl.BlockSpec(memory_space=pl.ANY),
            pl.BlockSpec(memory_space=pl.ANY),
            rope_spec,
            rope_spec,
            const(2 * N_META, HEAD_DIM),
            const(RET_HEADS, chunk, chunk),
            const(RET_HEADS, chunk, HEAD_DIM),
            const(RET_HEADS, chunk, HEAD_DIM),
            const(RET_HEADS, N_META, HEAD_DIM),
        ],
        out_specs=pl.BlockSpec((1, tokens, D_MODEL), lambda s: (*finished_block(s), 0)),
        out_shape=jax.ShapeDtypeStruct((bsz, seq, D_MODEL), x.dtype),
        scratch_shapes=[
            pltpu.VMEM((D_MODEL, 8 * SEC), jnp.bfloat16),
            pltpu.VMEM((D_MODEL, D_MODEL), jnp.bfloat16),
            pltpu.VMEM((STAGE_SLOTS * STAGE_ROWS, D_MODEL), jnp.float32),
            pltpu.SemaphoreType.DMA((STAGE_SLOTS,)),
            pltpu.VMEM((PAR_ROWS, D_MODEL), jnp.float32),
            pltpu.SemaphoreType.DMA((N_PAR_COPIES,)),
            pltpu.VMEM((RET_HEADS, HEAD_DIM, HEAD_DIM), jnp.float32),
            pltpu.VMEM((RET_HEADS, HEAD_DIM, HEAD_DIM), jnp.float32),
            pltpu.VMEM((HIST_ROWS + tokens, D_CONV), jnp.float32),
            pltpu.VMEM((HIST_ROWS, D_CONV + 256), jnp.float32),
            pltpu.VMEM((tokens, D_RET), jnp.bfloat16),
            pltpu.VMEM((tokens, D_RET), jnp.bfloat16),
            pltpu.VMEM((tokens, D_RET), jnp.bfloat16),
            pltpu.VMEM((tokens, D_RET), jnp.bfloat16),
            pltpu.VMEM((tokens, D_RET), jnp.bfloat16),
            pltpu.VMEM((n_hc, chunk, chunk), jnp.bfloat16),
            pltpu.VMEM((n_hc, HEAD_DIM, HEAD_DIM), jnp.bfloat16),
            pltpu.VMEM((tokens, D_MODEL), jnp.bfloat16),
        ],
        compiler_params=pltpu.CompilerParams(
            dimension_semantics=("arbitrary",),
            vmem_limit_bytes=VMEM_LIMIT_BYTES),
        name="hymba_mixer",
    )(x, meta, norm1_g.reshape(1, -1), w_in, conv_w,
      ret_norm_g.reshape(1, -1), w_out, final_g.reshape(1, -1),
      jnp.asarray(cos[N_META:]), jnp.asarray(sin[N_META:]),
      jnp.asarray(np.concatenate([cos[:N_META], sin[:N_META]], axis=0)),
      jnp.asarray(decay), jnp.asarray(xi), jnp.asarray(zeta), jnp.asarray(_meta_zeta()))
```

```python
import functools

import numpy as np
import jax
import jax.numpy as jnp
from jax import lax
from jax.experimental import pallas as pl
from jax.experimental.pallas import tpu as pltpu

D_MODEL = 1024
N_META = 16
D_CONV = 512
D_RET = 512
RET_HEADS = 4
HEAD_DIM = 128
HALF = HEAD_DIM // 2
CONV_WIDTH = 3
ROPE_BASE = 10000.0
EPS = 1e-6

SEC = 512
CX, CB, CC, CG, Q, K, V, RG = range(8)
HALF_SEC = SEC // 2
CONV_IN_LO, CONV_IN_HI, CONV_GATE_LO, CONV_GATE_HI = range(4)
WIN_HALF_BLOCK_ORDER = (2 * CC, 2 * CX, 2 * CC + 1, 2 * CX + 1, 2 * CB, 2 * CG, 2 * CB + 1, 2 * CG + 1,
                        2 * Q, 2 * Q + 1, 2 * K, 2 * K + 1, 2 * V, 2 * V + 1, 2 * RG, 2 * RG + 1)

TOKENS_PER_STEP = 1024
RET_CHUNK = 128
HIST_ROWS = 8
STAGE_ROWS = 256
STAGE_SLOTS = 8
PAR_G1, PAR_GF, PAR_RETG, PAR_CONVW, PAR_ROWS = 0, 1, 2, 3, 8
N_PAR_COPIES = 4
FINISH_GROUPS = 4
V7X_LANES = 128
V7X_F32_SUBLANES = 8
V7X_BF16_SUBLANES = 16
V7X_VMEM_BYTES = 64 * 1024 * 1024
VMEM_UNCLAIMED_BYTES = 6 * 1024 * 1024
VMEM_LIMIT_BYTES = V7X_VMEM_BYTES - VMEM_UNCLAIMED_BYTES


def _gammas():
    return 1.0 - 2.0 ** (-5.0 - np.arange(RET_HEADS, dtype=np.float64))


def _rope_tables(n_pos):
    freqs = 1.0 / (ROPE_BASE ** (np.arange(HALF, dtype=np.float64) / HALF))
    ang = np.arange(n_pos, dtype=np.float64)[:, None] * freqs[None, :]
    cos = np.concatenate([np.cos(ang), np.cos(ang)], axis=1)
    sin = np.concatenate([-np.sin(ang), np.sin(ang)], axis=1)
    return cos.astype(np.float32), sin.astype(np.float32)


def _decay_tables(chunk):
    g = _gammas()
    idx = np.arange(chunk, dtype=np.float64)
    diff = idx[:, None] - idx[None, :]
    scale = HEAD_DIM ** -0.5
    decay = np.where(diff[None] >= 0, g[:, None, None] ** np.maximum(diff[None], 0.0), 0.0) * scale
    xi = (g[:, None] ** (idx[None, :] + 1.0)) * scale
    zeta = g[:, None] ** (chunk - 1.0 - idx[None, :])
    xi = np.broadcast_to(xi[:, :, None], (RET_HEADS, chunk, HEAD_DIM))
    zeta = np.broadcast_to(zeta[:, :, None], (RET_HEADS, chunk, HEAD_DIM))
    return decay.astype(np.float32), xi.astype(np.float32), zeta.astype(np.float32)


def _meta_zeta():
    g = _gammas()
    j = np.arange(N_META, dtype=np.float64)
    z = g[:, None] ** (N_META - 1.0 - j[None, :])
    return np.broadcast_to(z[:, :, None], (RET_HEADS, N_META, HEAD_DIM)).astype(np.float32)


def _rms_norm(x, g):
    return x * lax.rsqrt(jnp.mean(x * x, axis=-1, keepdims=True) + EPS) * g


def _silu(x):
    return x * (1.0 / (1.0 + jnp.exp(-x)))


def _rotary(t, cos, sin):
    return t * cos + pltpu.roll(t, HALF, axis=1) * sin


def _project(hn_bf16, win_ref, sec):
    return jnp.dot(hn_bf16, win_ref[:, sec * SEC:(sec + 1) * SEC], preferred_element_type=jnp.float32)


def _project_pair(hn_bf16, win_ref, block):
    both = _project(hn_bf16, win_ref, block)
    return both[:, 0:HALF_SEC], both[:, HALF_SEC:]


def _head(t, h):
    return t[:, h * HEAD_DIM:(h + 1) * HEAD_DIM]


def _zero_tile_after(stored_f32):
    bits = pltpu.bitcast(stored_f32, jnp.uint32)
    rows, cols = bits.shape
    acc = bits[0:V7X_F32_SUBLANES]
    for r in range(V7X_F32_SUBLANES, rows, V7X_F32_SUBLANES):
        acc = acc | bits[r:r + V7X_F32_SUBLANES]
    word = acc[:, 0:V7X_LANES]
    for c in range(V7X_LANES, cols, V7X_LANES):
        word = word | acc[:, c:c + V7X_LANES]
    return pltpu.bitcast((word >> 16) >> 16, jnp.bfloat16)


def _after(lhs_bf16, zero_tile):
    r, c = V7X_BF16_SUBLANES, V7X_LANES
    top = jnp.concatenate([lhs_bf16[0:r, 0:c] + zero_tile, lhs_bf16[0:r, c:]], axis=1)
    return jnp.concatenate([top, lhs_bf16[r:]], axis=0)


def _weight_chunks(*weights):
    chunks = []
    for w_hbm, w_bf16_ref, order in weights:
        n_rows, n_cols = w_hbm.shape
        for c0 in range(0, n_cols, D_MODEL):
            for r0 in range(0, n_rows, STAGE_ROWS):
                chunks.append((w_hbm, w_bf16_ref, order, r0, c0))
    return chunks


def _stage_rows(idx):
    slot = idx % STAGE_SLOTS
    return pl.ds(slot * STAGE_ROWS, STAGE_ROWS)


def _stage_copy(chunks, idx, stage_ref, sem_ref):
    w_hbm, _, _, r0, c0 = chunks[idx]
    return pltpu.make_async_copy(w_hbm.at[pl.ds(r0, STAGE_ROWS), pl.ds(c0, D_MODEL)],
                                 stage_ref.at[_stage_rows(idx), :], sem_ref.at[idx % STAGE_SLOTS])


def _load_weights_as_bf16(chunks, stage_ref, sem_ref):
    for i in range(min(STAGE_SLOTS, len(chunks))):
        _stage_copy(chunks, i, stage_ref, sem_ref).start()
    for i, (_, w_bf16_ref, order, r0, c0) in enumerate(chunks):
        _stage_copy(chunks, i, stage_ref, sem_ref).wait()
        for j in range(D_MODEL // HALF_SEC):
            src_block = c0 // HALF_SEC + j
            dst = (src_block if order is None else order.index(src_block)) * HALF_SEC
            w_bf16_ref[r0:r0 + STAGE_ROWS, dst:dst + HALF_SEC] = (
                stage_ref[_stage_rows(i), j * HALF_SEC:(j + 1) * HALF_SEC].astype(jnp.bfloat16))
        if i + STAGE_SLOTS < len(chunks):
            _stage_copy(chunks, i + STAGE_SLOTS, stage_ref, sem_ref).start()


def _mixer_kernel(x_ref, meta_ref, g1_hbm, win_hbm, convw_hbm, retg_hbm, wout_hbm, gf_hbm,
                  cos_ref, sin_ref, mrope_ref, decay_ref, xi_ref, zeta_ref, mzeta_ref,
                  out_ref,
                  win_ref, wout_ref, ystage_ref, stage_sem, par_ref, par_sem,
                  state_ref, state0_ref, u_ref, hist0_ref, q_ref, qx_ref, k_ref, kz_ref, v_ref,
                  p_ref, sbf_ref, mixed_ref,
                  *, tokens, chunk, chunk_decay, blocks_per_seq, n_blocks):
    s = pl.program_id(0)
    bf16 = jnp.bfloat16
    f32 = jnp.float32
    y_ref = ystage_ref.at[0:tokens, :]
    g1_ref = par_ref.at[PAR_G1:PAR_G1 + 1, :]
    gf_ref = par_ref.at[PAR_GF:PAR_GF + 1, :]
    retg_ref = par_ref.at[PAR_RETG:PAR_RETG + 1, 0:D_RET]
    convw_ref = par_ref.at[PAR_CONVW:PAR_CONVW + CONV_WIDTH, 0:D_CONV]
    par_copies = [pltpu.make_async_copy(src, dst, par_sem.at[i]) for i, (src, dst) in enumerate(
        ((g1_hbm, g1_ref), (gf_hbm, gf_ref), (retg_hbm, retg_ref), (convw_hbm, convw_ref)))]

    def _finish_rows(rows):
        out_ref[0, rows, :] = _rms_norm(y_ref[rows, :], gf_ref[...])

    def _finish_previous_block():
        _finish_rows(slice(0, tokens))

    @pl.when(s == 0)
    def _first_step():
        for copy in par_copies:
            copy.start()
        _load_weights_as_bf16(_weight_chunks((win_hbm, win_ref, WIN_HALF_BLOCK_ORDER), (wout_hbm, wout_ref, None)), ystage_ref, stage_sem)
        y_ref[...] = jnp.zeros((tokens, D_MODEL), f32)
        for copy in par_copies:
            copy.wait()
        hm = _rms_norm(meta_ref[...], g1_ref[...]).astype(bf16)
        cc_lo, cx_lo = _project_pair(hm, win_ref, CONV_IN_LO)
        cc_hi, cx_hi = _project_pair(hm, win_ref, CONV_IN_HI)
        u_m = jnp.concatenate([cc_lo * cx_lo, cc_hi * cx_hi], axis=1)
        hist0_ref[...] = u_m[N_META - HIST_ROWS:, :]
        k_m = _project(hm, win_ref, K)
        v_m = _project(hm, win_ref, V).astype(bf16)
        for h in range(RET_HEADS):
            kr = _rotary(_head(k_m, h), mrope_ref[0:N_META, :], mrope_ref[N_META:, :]) * mzeta_ref[h]
            state0_ref[h] = lax.dot_general(kr.astype(bf16), _head(v_m, h), (((0,), (0,)), ((), ())),
                                            preferred_element_type=f32)

    @pl.when(jnp.logical_and(s % blocks_per_seq == 0, s < n_blocks))
    def _start_sequence():
        state_ref[...] = state0_ref[...]
        u_ref[0:HIST_ROWS, :] = hist0_ref[...]

    @pl.when(s < n_blocks)
    def _mix_block():
        group = tokens // FINISH_GROUPS
        finished = []
        for g in range(FINISH_GROUPS):
            rows = slice(g * group, (g + 1) * group)
            _finish_rows(rows)
            finished.append(_zero_tile_after(out_ref[0, rows, :]))
        x = x_ref[0]
        hn = _rms_norm(x, g1_ref[...]).astype(bf16)

        n_chunks = tokens // chunk

        cos = cos_ref[...]
        sin = sin_ref[...]
        q = _project(hn, win_ref, Q)
        k = _project(hn, win_ref, K)
        for h in range(RET_HEADS):
            cols = slice(h * HEAD_DIM, (h + 1) * HEAD_DIM)
            qr = _rotary(_head(q, h), cos, sin)
            kr = _rotary(_head(k, h), cos, sin)
            q_ref[:, cols] = qr.astype(bf16)
            k_ref[:, cols] = kr.astype(bf16)
            for c in range(n_chunks):
                rows = slice(c * chunk, (c + 1) * chunk)
                qx_ref[rows, cols] = (qr[rows] * xi_ref[h]).astype(bf16)
                kz_ref[rows, cols] = (kr[rows] * zeta_ref[h]).astype(bf16)
        v_ref[...] = _project(_after(hn, finished[0]), win_ref, V).astype(bf16)

        upd = {}
        for c in range(n_chunks):
            rows = slice(c * chunk, (c + 1) * chunk)
            for h in range(RET_HEADS):
                cols = slice(h * HEAD_DIM, (h + 1) * HEAD_DIM)
                scores = lax.dot_general(q_ref[rows, cols], k_ref[rows, cols], (((1,), (1,)), ((), ())),
                                         preferred_element_type=f32)
                p_ref[c * RET_HEADS + h] = (scores * decay_ref[h]).astype(bf16)
                upd[c, h] = lax.dot_general(kz_ref[rows, cols], v_ref[rows, cols], (((0,), (0,)), ((), ())),
                                            preferred_element_type=f32)
        for h in range(RET_HEADS):
            state = state_ref[h]
            for c in range(n_chunks):
                sbf_ref[c * RET_HEADS + h] = state.astype(bf16)
                state = chunk_decay[h] * state + upd[c, h]
            state_ref[h] = state

        cc_lo, cx_lo = _project_pair(_after(hn, finished[1]), win_ref, CONV_IN_LO)
        cc_hi, cx_hi = _project_pair(_after(hn, finished[2]), win_ref, CONV_IN_HI)
        u = jnp.concatenate([cc_lo * cx_lo, cc_hi * cx_hi], axis=1)
        u_ref[HIST_ROWS:HIST_ROWS + tokens, :] = u
        u1 = u_ref[HIST_ROWS - 1:HIST_ROWS - 1 + tokens, :]
        u2 = u_ref[HIST_ROWS - 2:HIST_ROWS - 2 + tokens, :]
        conv = convw_ref[0:1, :] * u2 + convw_ref[1:2, :] * u1 + convw_ref[2:3, :] * u
        cb_lo, cg_lo = _project_pair(_after(hn, finished[3]), win_ref, CONV_GATE_LO)
        cb_hi, cg_hi = _project_pair(hn, win_ref, CONV_GATE_HI)
        conv_gate = jnp.concatenate([cb_lo * _silu(cg_lo), cb_hi * _silu(cg_hi)], axis=1)
        mixed_ref[:, 0:D_CONV] = (conv * conv_gate).astype(bf16)
        u_ref[0:HIST_ROWS, :] = u_ref[tokens:tokens + HIST_ROWS, :]

        gate = _silu(_project(hn, win_ref, RG))
        for c in range(n_chunks):
            rows = slice(c * chunk, (c + 1) * chunk)
            for h in range(RET_HEADS):
                cols = slice(h * HEAD_DIM, (h + 1) * HEAD_DIM)
                hc = c * RET_HEADS + h
                lhs = jnp.concatenate([p_ref[hc], qx_ref[rows, cols]], axis=1)
                rhs = jnp.concatenate([v_ref[rows, cols], sbf_ref[hc]], axis=0)
                o = jnp.dot(lhs, rhs, preferred_element_type=f32)
                mu = jnp.mean(o, axis=-1, keepdims=True)
                d = o - mu
                var = jnp.mean(d * d, axis=-1, keepdims=True)
                y = d * lax.rsqrt(var + EPS) * retg_ref[:, cols]
                mixed_ref[rows, D_CONV + h * HEAD_DIM:D_CONV + (h + 1) * HEAD_DIM] = (
                    y * gate[rows, cols]).astype(bf16)

        y_ref[...] = jnp.dot(mixed_ref[...], wout_ref[...], preferred_element_type=f32) + x

    @pl.when(s == n_blocks)
    def _drain():
        _finish_previous_block()


def kernel(x, meta, norm1_g, w_in, conv_w, ret_norm_g, w_out, final_g):
    bsz, seq, d_model = x.shape
    assert d_model == D_MODEL and meta.shape == (N_META, D_MODEL)
    tokens, chunk = TOKENS_PER_STEP, RET_CHUNK
    assert seq % tokens == 0 and tokens % chunk == 0 and tokens <= STAGE_SLOTS * STAGE_ROWS

    cos, sin = _rope_tables(N_META + seq)
    decay, xi, zeta = _decay_tables(chunk)
    chunk_decay = tuple(float(v) for v in _gammas() ** chunk)
    n_hc = (tokens // chunk) * RET_HEADS

    blocks_per_seq = seq // tokens
    n_blocks = bsz * blocks_per_seq

    def mixed_block(s):
        s = jnp.minimum(s, n_blocks - 1)
        return s // blocks_per_seq, s % blocks_per_seq

    def finished_block(s):
        s = jnp.maximum(s - 1, 0)
        return s // blocks_per_seq, s % blocks_per_seq

    const = lambda *shape: pl.BlockSpec(shape, lambda s: (0,) * len(shape))
    rope_spec = pl.BlockSpec((tokens, HEAD_DIM), lambda s: (mixed_block(s)[1], 0))
    body = functools.partial(_mixer_kernel, tokens=tokens, chunk=chunk, chunk_decay=chunk_decay,
                             blocks_per_seq=blocks_per_seq, n_blocks=n_blocks)
    return pl.pallas_call(
        body,
        grid=(n_blocks + 1,),
        in_specs=[
            pl.BlockSpec((1, tokens, D_MODEL), lambda s: (*mixed_block(s), 0)),
            const(N_META, D_MODEL),
            pl.BlockSpec(memory_space=pl.ANY),
            pl.BlockSpec(memory_space=pl.ANY),
            pl.BlockSpec(memory_space=pl.ANY),
            pl.BlockSpec(memory_space=pl.ANY),
            pl.BlockSpec(memory_space=pl.ANY),
            pl.BlockSpec(memory_space=pl.ANY),
            rope_spec,
            rope_spec,
            const(2 * N_META, HEAD_DIM),
            const(RET_HEADS, chunk, chunk),
            const(RET_HEADS, chunk, HEAD_DIM),
            const(RET_HEADS, chunk, HEAD_DIM),
            const(RET_HEADS, N_META, HEAD_DIM),
        ],
        out_specs=pl.BlockSpec((1, tokens, D_MODEL), lambda s: (*finished_block(s), 0)),
        out_shape=jax.ShapeDtypeStruct((bsz, seq, D_MODEL), x.dtype),
        scratch_shapes=[
            pltpu.VMEM((D_MODEL, 8 * SEC), jnp.bfloat16),
            pltpu.VMEM((D_MODEL, D_MODEL), jnp.bfloat16),
            pltpu.VMEM((STAGE_SLOTS * STAGE_ROWS, D_MODEL), jnp.float32),
            pltpu.SemaphoreType.DMA((STAGE_SLOTS,)),
            pltpu.VMEM((PAR_ROWS, D_MODEL), jnp.float32),
            pltpu.SemaphoreType.DMA((N_PAR_COPIES,)),
            pltpu.VMEM((RET_HEADS, HEAD_DIM, HEAD_DIM), jnp.float32),
            pltpu.VMEM((RET_HEADS, HEAD_DIM, HEAD_DIM), jnp.float32),
            pltpu.VMEM((HIST_ROWS + tokens, D_CONV), jnp.float32),
            pltpu.VMEM((HIST_ROWS, D_CONV), jnp.float32),
            pltpu.VMEM((tokens, D_RET), jnp.bfloat16),
            pltpu.VMEM((tokens, D_RET), jnp.bfloat16),
            pltpu.VMEM((tokens, D_RET), jnp.bfloat16),
            pltpu.VMEM((tokens, D_RET), jnp.bfloat16),
            pltpu.VMEM((tokens, D_RET), jnp.bfloat16),
            pltpu.VMEM((n_hc, chunk, chunk), jnp.bfloat16),
            pltpu.VMEM((n_hc, HEAD_DIM, HEAD_DIM), jnp.bfloat16),
            pltpu.VMEM((tokens, D_MODEL), jnp.bfloat16),
        ],
        compiler_params=pltpu.CompilerParams(
            dimension_semantics=("arbitrary",),
            vmem_limit_bytes=VMEM_LIMIT_BYTES),
        name="hymba_mixer",
    )(x, meta, norm1_g.reshape(1, -1), w_in, conv_w,
      ret_norm_g.reshape(1, -1), w_out, final_g.reshape(1, -1),
      jnp.asarray(cos[N_META:]), jnp.asarray(sin[N_META:]),
      jnp.asarray(np.concatenate([cos[:N_META], sin[:N_META]], axis=0)),
      jnp.asarray(decay), jnp.asarray(xi), jnp.asarray(zeta), jnp.asarray(_meta_zeta()))
```

```python
import functools

import numpy as np
import jax
import jax.numpy as jnp
from jax import lax
from jax.experimental import pallas as pl
from jax.experimental.pallas import tpu as pltpu

D_MODEL = 1024
N_META = 16
D_CONV = 512
D_RET = 512
RET_HEADS = 4
HEAD_DIM = 128
HALF = HEAD_DIM // 2
CONV_WIDTH = 3
ROPE_BASE = 10000.0
EPS = 1e-6

SEC = 512
CX, CB, CC, CG, Q, K, V, RG = range(8)

TOKENS_PER_STEP = 1024
RET_CHUNK = 128
HIST_ROWS = 8
STAGE_ROWS = 256
STAGE_SLOTS = 8
PAR_G1, PAR_GF, PAR_RETG, PAR_CONVW, PAR_ROWS = 0, 1, 2, 3, 8
N_PAR_COPIES = 4
FINISH_GROUPS = 4
V7X_LANES = 128
V7X_F32_SUBLANES = 8
V7X_BF16_SUBLANES = 16
V7X_VMEM_BYTES = 64 * 1024 * 1024
VMEM_UNCLAIMED_BYTES = 6 * 1024 * 1024
VMEM_LIMIT_BYTES = V7X_VMEM_BYTES - VMEM_UNCLAIMED_BYTES


def _gammas():
    return 1.0 - 2.0 ** (-5.0 - np.arange(RET_HEADS, dtype=np.float64))


def _rope_tables(n_pos):
    freqs = 1.0 / (ROPE_BASE ** (np.arange(HALF, dtype=np.float64) / HALF))
    ang = np.arange(n_pos, dtype=np.float64)[:, None] * freqs[None, :]
    cos = np.concatenate([np.cos(ang), np.cos(ang)], axis=1)
    sin = np.concatenate([-np.sin(ang), np.sin(ang)], axis=1)
    return cos.astype(np.float32), sin.astype(np.float32)


def _decay_tables(chunk):
    g = _gammas()
    idx = np.arange(chunk, dtype=np.float64)
    diff = idx[:, None] - idx[None, :]
    scale = HEAD_DIM ** -0.5
    decay = np.where(diff[None] >= 0, g[:, None, None] ** np.maximum(diff[None], 0.0), 0.0) * scale
    xi = (g[:, None] ** (idx[None, :] + 1.0)) * scale
    zeta = g[:, None] ** (chunk - 1.0 - idx[None, :])
    xi = np.broadcast_to(xi[:, :, None], (RET_HEADS, chunk, HEAD_DIM))
    zeta = np.broadcast_to(zeta[:, :, None], (RET_HEADS, chunk, HEAD_DIM))
    return decay.astype(np.float32), xi.astype(np.float32), zeta.astype(np.float32)


def _meta_zeta():
    g = _gammas()
    j = np.arange(N_META, dtype=np.float64)
    z = g[:, None] ** (N_META - 1.0 - j[None, :])
    return np.broadcast_to(z[:, :, None], (RET_HEADS, N_META, HEAD_DIM)).astype(np.float32)


def _rms_norm(x, g):
    return x * lax.rsqrt(jnp.mean(x * x, axis=-1, keepdims=True) + EPS) * g


def _silu(x):
    return x * (1.0 / (1.0 + jnp.exp(-x)))


def _rotary(t, cos, sin):
    return t * cos + pltpu.roll(t, HALF, axis=1) * sin


def _project(hn_bf16, win_ref, sec):
    return jnp.dot(hn_bf16, win_ref[:, sec * SEC:(sec + 1) * SEC], preferred_element_type=jnp.float32)


def _head(t, h):
    return t[:, h * HEAD_DIM:(h + 1) * HEAD_DIM]


def _zero_tile_after(stored_f32):
    bits = pltpu.bitcast(stored_f32, jnp.uint32)
    rows, cols = bits.shape
    acc = bits[0:V7X_F32_SUBLANES]
    for r in range(V7X_F32_SUBLANES, rows, V7X_F32_SUBLANES):
        acc = acc | bits[r:r + V7X_F32_SUBLANES]
    word = acc[:, 0:V7X_LANES]
    for c in range(V7X_LANES, cols, V7X_LANES):
        word = word | acc[:, c:c + V7X_LANES]
    return pltpu.bitcast((word >> 16) >> 16, jnp.bfloat16)


def _after(lhs_bf16, zero_tile):
    r, c = V7X_BF16_SUBLANES, V7X_LANES
    top = jnp.concatenate([lhs_bf16[0:r, 0:c] + zero_tile, lhs_bf16[0:r, c:]], axis=1)
    return jnp.concatenate([top, lhs_bf16[r:]], axis=0)


def _weight_chunks(*weights):
    chunks = []
    for w_hbm, w_bf16_ref in weights:
        n_rows, n_cols = w_hbm.shape
        for c0 in range(0, n_cols, D_MODEL):
            for r0 in range(0, n_rows, STAGE_ROWS):
                chunks.append((w_hbm, w_bf16_ref, r0, c0))
    return chunks


def _stage_rows(idx):
    slot = idx % STAGE_SLOTS
    return pl.ds(slot * STAGE_ROWS, STAGE_ROWS)


def _stage_copy(chunks, idx, stage_ref, sem_ref):
    w_hbm, _, r0, c0 = chunks[idx]
    return pltpu.make_async_copy(w_hbm.at[pl.ds(r0, STAGE_ROWS), pl.ds(c0, D_MODEL)],
                                 stage_ref.at[_stage_rows(idx), :], sem_ref.at[idx % STAGE_SLOTS])


def _load_weights_as_bf16(chunks, stage_ref, sem_ref):
    for i in range(min(STAGE_SLOTS, len(chunks))):
        _stage_copy(chunks, i, stage_ref, sem_ref).start()
    for i, (_, w_bf16_ref, r0, c0) in enumerate(chunks):
        _stage_copy(chunks, i, stage_ref, sem_ref).wait()
        w_bf16_ref[r0:r0 + STAGE_ROWS, c0:c0 + D_MODEL] = stage_ref[_stage_rows(i), :].astype(jnp.bfloat16)
        if i + STAGE_SLOTS < len(chunks):
            _stage_copy(chunks, i + STAGE_SLOTS, stage_ref, sem_ref).start()


def _mixer_kernel(x_hbm, meta_ref, g1_hbm, win_hbm, convw_hbm, retg_hbm, wout_hbm, gf_hbm,
                  cos_hbm, sin_hbm, mrope_ref, decay_ref, xi_ref, zeta_ref, mzeta_ref,
                  out_hbm,
                  win_ref, wout_ref, ystage_ref, stage_sem, par_ref, par_sem,
                  xbuf, cosbuf, sinbuf, obuf, in_sem, out_sem,
                  state_ref, state0_ref, u_ref, hist0_ref, q_ref, qx_ref, k_ref, kz_ref, v_ref,
                  p_ref, sbf_ref, mixed_ref,
                  *, tokens, chunk, chunk_decay, blocks_per_seq, n_blocks):
    bf16 = jnp.bfloat16
    f32 = jnp.float32
    y_ref = ystage_ref.at[0:tokens, :]
    g1_ref = par_ref.at[PAR_G1:PAR_G1 + 1, :]
    gf_ref = par_ref.at[PAR_GF:PAR_GF + 1, :]
    retg_ref = par_ref.at[PAR_RETG:PAR_RETG + 1, 0:D_RET]
    convw_ref = par_ref.at[PAR_CONVW:PAR_CONVW + CONV_WIDTH, 0:D_CONV]
    par_copies = [pltpu.make_async_copy(src, dst, par_sem.at[i]) for i, (src, dst) in enumerate(
        ((g1_hbm, g1_ref), (gf_hbm, gf_ref), (retg_hbm, retg_ref), (convw_hbm, convw_ref)))]

    def _block_rows(blk):
        return blk // blocks_per_seq, pl.ds((blk % blocks_per_seq) * tokens, tokens)

    def _in_copies(blk, slot):
        b, rows = _block_rows(blk)
        return (pltpu.make_async_copy(x_hbm.at[b, rows, :], xbuf.at[slot], in_sem.at[slot, 0]),
                pltpu.make_async_copy(cos_hbm.at[rows, :], cosbuf.at[slot], in_sem.at[slot, 1]),
                pltpu.make_async_copy(sin_hbm.at[rows, :], sinbuf.at[slot], in_sem.at[slot, 2]))

    def _out_copy(blk, slot):
        b, rows = _block_rows(blk)
        return pltpu.make_async_copy(obuf.at[slot], out_hbm.at[b, rows, :], out_sem.at[slot])

    def _finish_rows(out_ref, rows):
        out_ref[rows, :] = _rms_norm(y_ref[rows, :], gf_ref[...])

    def _mix_block(x_ref, cos_ref, sin_ref, out_ref):
        group = tokens // FINISH_GROUPS
        finished = []
        for g in range(FINISH_GROUPS):
            rows = slice(g * group, (g + 1) * group)
            _finish_rows(out_ref, rows)
            finished.append(_zero_tile_after(out_ref[rows, :]))
        x = x_ref[...]
        hn = _rms_norm(x, g1_ref[...]).astype(bf16)

        n_chunks = tokens // chunk

        cos = cos_ref[...]
        sin = sin_ref[...]
        q = _project(hn, win_ref, Q)
        k = _project(hn, win_ref, K)
        for h in range(RET_HEADS):
            cols = slice(h * HEAD_DIM, (h + 1) * HEAD_DIM)
            qr = _rotary(_head(q, h), cos, sin)
            kr = _rotary(_head(k, h), cos, sin)
            q_ref[:, cols] = qr.astype(bf16)
            k_ref[:, cols] = kr.astype(bf16)
            for c in range(n_chunks):
                rows = slice(c * chunk, (c + 1) * chunk)
                qx_ref[rows, cols] = (qr[rows] * xi_ref[h]).astype(bf16)
                kz_ref[rows, cols] = (kr[rows] * zeta_ref[h]).astype(bf16)
        v_ref[...] = _project(_after(hn, finished[0]), win_ref, V).astype(bf16)

        upd = {}
        for c in range(n_chunks):
            rows = slice(c * chunk, (c + 1) * chunk)
            for h in range(RET_HEADS):
                cols = slice(h * HEAD_DIM, (h + 1) * HEAD_DIM)
                scores = lax.dot_general(q_ref[rows, cols], k_ref[rows, cols], (((1,), (1,)), ((), ())),
                                         preferred_element_type=f32)
                p_ref[c * RET_HEADS + h] = (scores * decay_ref[h]).astype(bf16)
                upd[c, h] = lax.dot_general(kz_ref[rows, cols], v_ref[rows, cols], (((0,), (0,)), ((), ())),
                                            preferred_element_type=f32)
        for h in range(RET_HEADS):
            state = state_ref[h]
            for c in range(n_chunks):
                sbf_ref[c * RET_HEADS + h] = state.astype(bf16)
                state = chunk_decay[h] * state + upd[c, h]
            state_ref[h] = state

        u = _project(_after(hn, finished[1]), win_ref, CC) * _project(_after(hn, finished[2]), win_ref, CX)
        u_ref[HIST_ROWS:HIST_ROWS + tokens, :] = u
        u1 = u_ref[HIST_ROWS - 1:HIST_ROWS - 1 + tokens, :]
        u2 = u_ref[HIST_ROWS - 2:HIST_ROWS - 2 + tokens, :]
        conv = convw_ref[0:1, :] * u2 + convw_ref[1:2, :] * u1 + convw_ref[2:3, :] * u
        conv_out = _project(_after(hn, finished[3]), win_ref, CB) * conv * _silu(_project(hn, win_ref, CG))
        mixed_ref[:, 0:D_CONV] = conv_out.astype(bf16)
        u_ref[0:HIST_ROWS, :] = u_ref[tokens:tokens + HIST_ROWS, :]

        gate = _silu(_project(hn, win_ref, RG))
        for c in range(n_chunks):
            rows = slice(c * chunk, (c + 1) * chunk)
            for h in range(RET_HEADS):
                cols = slice(h * HEAD_DIM, (h + 1) * HEAD_DIM)
                hc = c * RET_HEADS + h
                lhs = jnp.concatenate([p_ref[hc], qx_ref[rows, cols]], axis=1)
                rhs = jnp.concatenate([v_ref[rows, cols], sbf_ref[hc]], axis=0)
                o = jnp.dot(lhs, rhs, preferred_element_type=f32)
                mu = jnp.mean(o, axis=-1, keepdims=True)
                d = o - mu
                var = jnp.mean(d * d, axis=-1, keepdims=True)
                y = d * lax.rsqrt(var + EPS) * retg_ref[:, cols]
                mixed_ref[rows, D_CONV + h * HEAD_DIM:D_CONV + (h + 1) * HEAD_DIM] = (
                    y * gate[rows, cols]).astype(bf16)

        y_ref[...] = jnp.dot(mixed_ref[...], wout_ref[...], preferred_element_type=f32) + x

    for copy in _in_copies(0, 0):
        copy.start()
    for copy in par_copies:
        copy.start()
    _load_weights_as_bf16(_weight_chunks((win_hbm, win_ref), (wout_hbm, wout_ref)), ystage_ref, stage_sem)
    y_ref[...] = jnp.zeros((tokens, D_MODEL), f32)
    for copy in par_copies:
        copy.wait()
    hm = _rms_norm(meta_ref[...], g1_ref[...]).astype(bf16)
    u_m = _project(hm, win_ref, CC) * _project(hm, win_ref, CX)
    hist0_ref[...] = u_m[N_META - HIST_ROWS:, :]
    k_m = _project(hm, win_ref, K)
    v_m = _project(hm, win_ref, V).astype(bf16)
    for h in range(RET_HEADS):
        kr = _rotary(_head(k_m, h), mrope_ref[0:N_META, :], mrope_ref[N_META:, :]) * mzeta_ref[h]
        state0_ref[h] = lax.dot_general(kr.astype(bf16), _head(v_m, h), (((0,), (0,)), ((), ())),
                                        preferred_element_type=f32)

    def _step(s, carry):
        slot = s % 2
        prev_slot = 1 - slot
        for copy in _in_copies(s, slot):
            copy.wait()

        @pl.when(s + 1 < n_blocks)
        def _fetch_next():
            for copy in _in_copies(s + 1, prev_slot):
                copy.start()

        @pl.when(s >= 3)
        def _free_out_slot():
            _out_copy(s - 3, prev_slot).wait()

        @pl.when(s % blocks_per_seq == 0)
        def _start_sequence():
            state_ref[...] = state0_ref[...]
            u_ref[0:HIST_ROWS, :] = hist0_ref[...]

        _mix_block(xbuf.at[slot], cosbuf.at[slot], sinbuf.at[slot], obuf.at[prev_slot])

        @pl.when(s >= 1)
        def _write_back_previous():
            _out_copy(s - 1, prev_slot).start()

        return carry

    lax.fori_loop(0, n_blocks, _step, 0)

    last = n_blocks - 1
    last_slot = last % 2
    if n_blocks >= 3:
        _out_copy(last - 2, last_slot).wait()
    _finish_rows(obuf.at[last_slot], slice(0, tokens))
    _out_copy(last, last_slot).start()
    if n_blocks >= 2:
        _out_copy(last - 1, 1 - last_slot).wait()
    _out_copy(last, last_slot).wait()


def kernel(x, meta, norm1_g, w_in, conv_w, ret_norm_g, w_out, final_g):
    bsz, seq, d_model = x.shape
    assert d_model == D_MODEL and meta.shape == (N_META, D_MODEL)
    tokens, chunk = TOKENS_PER_STEP, RET_CHUNK
    assert seq % tokens == 0 and tokens % chunk == 0 and tokens <= STAGE_SLOTS * STAGE_ROWS

    cos, sin = _rope_tables(N_META + seq)
    decay, xi, zeta = _decay_tables(chunk)
    chunk_decay = tuple(float(v) for v in _gammas() ** chunk)
    n_hc = (tokens // chunk) * RET_HEADS

    blocks_per_seq = seq // tokens
    n_blocks = bsz * blocks_per_seq

    in_vmem = pl.BlockSpec(memory_space=pltpu.VMEM)
    in_hbm = pl.BlockSpec(memory_space=pl.ANY)
    body = functools.partial(_mixer_kernel, tokens=tokens, chunk=chunk, chunk_decay=chunk_decay,
                             blocks_per_seq=blocks_per_seq, n_blocks=n_blocks)
    return pl.pallas_call(
        body,
        in_specs=[
            in_hbm,
            in_vmem,
            in_hbm,
            in_hbm,
            in_hbm,
            in_hbm,
            in_hbm,
            in_hbm,
            in_hbm,
            in_hbm,
            in_vmem,
            in_vmem,
            in_vmem,
            in_vmem,
            in_vmem,
        ],
        out_specs=pl.BlockSpec(memory_space=pl.ANY),
        out_shape=jax.ShapeDtypeStruct((bsz, seq, D_MODEL), x.dtype),
        scratch_shapes=[
            pltpu.VMEM((D_MODEL, 8 * SEC), jnp.bfloat16),
            pltpu.VMEM((D_MODEL, D_MODEL), jnp.bfloat16),
            pltpu.VMEM((STAGE_SLOTS * STAGE_ROWS, D_MODEL), jnp.float32),
            pltpu.SemaphoreType.DMA((STAGE_SLOTS,)),
            pltpu.VMEM((PAR_ROWS, D_MODEL), jnp.float32),
            pltpu.SemaphoreType.DMA((N_PAR_COPIES,)),
            pltpu.VMEM((2, tokens, D_MODEL), jnp.float32),
            pltpu.VMEM((2, tokens, HEAD_DIM), jnp.float32),
            pltpu.VMEM((2, tokens, HEAD_DIM), jnp.float32),
            pltpu.VMEM((2, tokens, D_MODEL), jnp.float32),
            pltpu.SemaphoreType.DMA((2, 3)),
            pltpu.SemaphoreType.DMA((2,)),
            pltpu.VMEM((RET_HEADS, HEAD_DIM, HEAD_DIM), jnp.float32),
            pltpu.VMEM((RET_HEADS, HEAD_DIM, HEAD_DIM), jnp.float32),
            pltpu.VMEM((HIST_ROWS + tokens, D_CONV), jnp.float32),
            pltpu.VMEM((HIST_ROWS, D_CONV), jnp.float32),
            pltpu.VMEM((tokens, D_RET), jnp.bfloat16),
            pltpu.VMEM((tokens, D_RET), jnp.bfloat16),
            pltpu.VMEM((tokens, D_RET), jnp.bfloat16),
            pltpu.VMEM((tokens, D_RET), jnp.bfloat16),
            pltpu.VMEM((tokens, D_RET), jnp.bfloat16),
            pltpu.VMEM((n_hc, chunk, chunk), jnp.bfloat16),
            pltpu.VMEM((n_hc, HEAD_DIM, HEAD_DIM), jnp.bfloat16),
            pltpu.VMEM((tokens, D_MODEL), jnp.bfloat16),
        ],
        compiler_params=pltpu.CompilerParams(
            vmem_limit_bytes=VMEM_LIMIT_BYTES),
        name="hymba_mixer",
    )(x, meta, norm1_g.reshape(1, -1), w_in, conv_w,
      ret_norm_g.reshape(1, -1), w_out, final_g.reshape(1, -1),
      jnp.asarray(cos[N_META:]), jnp.asarray(sin[N_META:]),
      jnp.asarray(np.concatenate([cos[:N_META], sin[:N_META]], axis=0)),
      jnp.asarray(decay), jnp.asarray(xi), jnp.asarray(zeta), jnp.asarray(_meta_zeta()))
```

```python
import functools

import numpy as np
import jax
import jax.numpy as jnp
from jax import lax
from jax.experimental import pallas as pl
from jax.experimental.pallas import tpu as pltpu

D_MODEL = 1024
N_META = 16
D_CONV = 512
D_RET = 512
RET_HEADS = 4
HEAD_DIM = 128
HALF = HEAD_DIM // 2
CONV_WIDTH = 3
ROPE_BASE = 10000.0
EPS = 1e-6

SEC = 512
CX, CB, CC, CG, Q, K, V, RG = range(8)

TOKENS_PER_STEP = 1024
RET_CHUNK = 128
HIST_ROWS = 8
STAGE_ROWS = 256
STAGE_SLOTS = 8
PAR_G1, PAR_GF, PAR_RETG, PAR_CONVW, PAR_ROWS = 0, 1, 2, 3, 8
N_PAR_COPIES = 4
FINISH_GROUPS = 4
V7X_LANES = 128
V7X_F32_SUBLANES = 8
V7X_BF16_SUBLANES = 16
V7X_VMEM_BYTES = 64 * 1024 * 1024
VMEM_UNCLAIMED_BYTES = 6 * 1024 * 1024
VMEM_LIMIT_BYTES = V7X_VMEM_BYTES - VMEM_UNCLAIMED_BYTES


def _gammas():
    return 1.0 - 2.0 ** (-5.0 - np.arange(RET_HEADS, dtype=np.float64))


def _rope_tables(n_pos):
    freqs = 1.0 / (ROPE_BASE ** (np.arange(HALF, dtype=np.float64) / HALF))
    ang = np.arange(n_pos, dtype=np.float64)[:, None] * freqs[None, :]
    cos = np.concatenate([np.cos(ang), np.cos(ang)], axis=1)
    sin = np.concatenate([-np.sin(ang), np.sin(ang)], axis=1)
    return cos.astype(np.float32), sin.astype(np.float32)


def _decay_tables(chunk):
    g = _gammas()
    idx = np.arange(chunk, dtype=np.float64)
    diff = idx[:, None] - idx[None, :]
    scale = HEAD_DIM ** -0.5
    decay = np.where(diff[None] >= 0, g[:, None, None] ** np.maximum(diff[None], 0.0), 0.0) * scale
    xi = (g[:, None] ** (idx[None, :] + 1.0)) * scale
    zeta = g[:, None] ** (chunk - 1.0 - idx[None, :])
    xi = np.broadcast_to(xi[:, :, None], (RET_HEADS, chunk, HEAD_DIM))
    zeta = np.broadcast_to(zeta[:, :, None], (RET_HEADS, chunk, HEAD_DIM))
    return decay.astype(np.float32), xi.astype(np.float32), zeta.astype(np.float32)


def _meta_zeta():
    g = _gammas()
    j = np.arange(N_META, dtype=np.float64)
    z = g[:, None] ** (N_META - 1.0 - j[None, :])
    return np.broadcast_to(z[:, :, None], (RET_HEADS, N_META, HEAD_DIM)).astype(np.float32)


def _rms_norm(x, g):
    return x * lax.rsqrt(jnp.mean(x * x, axis=-1, keepdims=True) + EPS) * g


def _silu(x):
    return x * (1.0 / (1.0 + jnp.exp(-x)))


def _rotary(t, cos, sin):
    return t * cos + pltpu.roll(t, HALF, axis=1) * sin


def _project(hn_bf16, win_ref, sec):
    return jnp.dot(hn_bf16, win_ref[:, sec * SEC:(sec + 1) * SEC], preferred_element_type=jnp.float32)


def _head(t, h):
    return t[:, h * HEAD_DIM:(h + 1) * HEAD_DIM]


def _zero_tile_after(stored_f32):
    bits = pltpu.bitcast(stored_f32, jnp.uint32)
    rows, cols = bits.shape
    acc = bits[0:V7X_F32_SUBLANES]
    for r in range(V7X_F32_SUBLANES, rows, V7X_F32_SUBLANES):
        acc = acc | bits[r:r + V7X_F32_SUBLANES]
    word = acc[:, 0:V7X_LANES]
    for c in range(V7X_LANES, cols, V7X_LANES):
        word = word | acc[:, c:c + V7X_LANES]
    return pltpu.bitcast((word >> 16) >> 16, jnp.bfloat16)


def _after(lhs_bf16, zero_tile):
    r, c = V7X_BF16_SUBLANES, V7X_LANES
    top = jnp.concatenate([lhs_bf16[0:r, 0:c] + zero_tile, lhs_bf16[0:r, c:]], axis=1)
    return jnp.concatenate([top, lhs_bf16[r:]], axis=0)


def _weight_chunks(*weights):
    chunks = []
    for w_hbm, w_bf16_ref in weights:
        n_rows, n_cols = w_hbm.shape
        for c0 in range(0, n_cols, D_MODEL):
            for r0 in range(0, n_rows, STAGE_ROWS):
                chunks.append((w_hbm, w_bf16_ref, r0, c0))
    return chunks


def _stage_rows(idx):
    slot = idx % STAGE_SLOTS
    return pl.ds(slot * STAGE_ROWS, STAGE_ROWS)


def _stage_copy(chunks, idx, stage_ref, sem_ref):
    w_hbm, _, r0, c0 = chunks[idx]
    return pltpu.make_async_copy(w_hbm.at[pl.ds(r0, STAGE_ROWS), pl.ds(c0, D_MODEL)],
                                 stage_ref.at[_stage_rows(idx), :], sem_ref.at[idx % STAGE_SLOTS])


def _load_weights_as_bf16(chunks, first, last, stage_ref, sem_ref):
    if first == 0:
        for i in range(min(STAGE_SLOTS, len(chunks))):
            _stage_copy(chunks, i, stage_ref, sem_ref).start()
    for i in range(first, last):
        _, w_bf16_ref, r0, c0 = chunks[i]
        _stage_copy(chunks, i, stage_ref, sem_ref).wait()
        w_bf16_ref[r0:r0 + STAGE_ROWS, c0:c0 + D_MODEL] = stage_ref[_stage_rows(i), :].astype(jnp.bfloat16)
        if i + STAGE_SLOTS < len(chunks):
            _stage_copy(chunks, i + STAGE_SLOTS, stage_ref, sem_ref).start()


def _mixer_kernel(x_hbm, meta_ref, g1_hbm, win_hbm, convw_hbm, retg_hbm, wout_hbm, gf_hbm,
                  cos_hbm, sin_hbm, mrope_ref, decay_ref, xi_ref, zeta_ref, mzeta_ref,
                  out_hbm,
                  win_ref, wout_ref, ystage_ref, stage_sem, par_ref, par_sem,
                  xbuf, cosbuf, sinbuf, obuf, in_sem, out_sem, tail_sem,
                  state_ref, state0_ref, u_ref, hist0_ref, q_ref, qx_ref, k_ref, kz_ref, v_ref,
                  p_ref, sbf_ref, mixed_ref,
                  *, tokens, chunk, chunk_decay, blocks_per_seq, n_blocks):
    bf16 = jnp.bfloat16
    f32 = jnp.float32
    y_ref = ystage_ref.at[0:tokens, :]
    g1_ref = par_ref.at[PAR_G1:PAR_G1 + 1, :]
    gf_ref = par_ref.at[PAR_GF:PAR_GF + 1, :]
    retg_ref = par_ref.at[PAR_RETG:PAR_RETG + 1, 0:D_RET]
    convw_ref = par_ref.at[PAR_CONVW:PAR_CONVW + CONV_WIDTH, 0:D_CONV]
    par_copies = [pltpu.make_async_copy(src, dst, par_sem.at[i]) for i, (src, dst) in enumerate(
        ((g1_hbm, g1_ref), (gf_hbm, gf_ref), (retg_hbm, retg_ref), (convw_hbm, convw_ref)))]

    def _block_rows(blk):
        return blk // blocks_per_seq, pl.ds((blk % blocks_per_seq) * tokens, tokens)

    def _in_copies(blk, slot):
        b, rows = _block_rows(blk)
        return (pltpu.make_async_copy(x_hbm.at[b, rows, :], xbuf.at[slot], in_sem.at[slot, 0]),
                pltpu.make_async_copy(cos_hbm.at[rows, :], cosbuf.at[slot], in_sem.at[slot, 1]),
                pltpu.make_async_copy(sin_hbm.at[rows, :], sinbuf.at[slot], in_sem.at[slot, 2]))

    def _out_copy(blk, slot):
        b, rows = _block_rows(blk)
        return pltpu.make_async_copy(obuf.at[slot], out_hbm.at[b, rows, :], out_sem.at[slot])

    def _finish_rows(out_ref, rows):
        out_ref[rows, :] = _rms_norm(y_ref[rows, :], gf_ref[...])

    def _mix_block(x_ref, cos_ref, sin_ref, out_ref):
        group = tokens // FINISH_GROUPS
        finished = []
        for g in range(FINISH_GROUPS):
            rows = slice(g * group, (g + 1) * group)
            _finish_rows(out_ref, rows)
            finished.append(_zero_tile_after(out_ref[rows, :]))
        x = x_ref[...]
        hn = _rms_norm(x, g1_ref[...]).astype(bf16)

        n_chunks = tokens // chunk

        cos = cos_ref[...]
        sin = sin_ref[...]
        q = _project(hn, win_ref, Q)
        k = _project(hn, win_ref, K)
        for h in range(RET_HEADS):
            cols = slice(h * HEAD_DIM, (h + 1) * HEAD_DIM)
            qr = _rotary(_head(q, h), cos, sin)
            kr = _rotary(_head(k, h), cos, sin)
            q_ref[:, cols] = qr.astype(bf16)
            k_ref[:, cols] = kr.astype(bf16)
            for c in range(n_chunks):
                rows = slice(c * chunk, (c + 1) * chunk)
                qx_ref[rows, cols] = (qr[rows] * xi_ref[h]).astype(bf16)
                kz_ref[rows, cols] = (kr[rows] * zeta_ref[h]).astype(bf16)
        v_ref[...] = _project(_after(hn, finished[0]), win_ref, V).astype(bf16)

        upd = {}
        for c in range(n_chunks):
            rows = slice(c * chunk, (c + 1) * chunk)
            for h in range(RET_HEADS):
                cols = slice(h * HEAD_DIM, (h + 1) * HEAD_DIM)
                scores = lax.dot_general(q_ref[rows, cols], k_ref[rows, cols], (((1,), (1,)), ((), ())),
                                         preferred_element_type=f32)
                p_ref[c * RET_HEADS + h] = (scores * decay_ref[h]).astype(bf16)
                upd[c, h] = lax.dot_general(kz_ref[rows, cols], v_ref[rows, cols], (((0,), (0,)), ((), ())),
                                            preferred_element_type=f32)
        for h in range(RET_HEADS):
            state = state_ref[h]
            for c in range(n_chunks):
                sbf_ref[c * RET_HEADS + h] = state.astype(bf16)
                state = chunk_decay[h] * state + upd[c, h]
            state_ref[h] = state

        u = _project(_after(hn, finished[1]), win_ref, CC) * _project(_after(hn, finished[2]), win_ref, CX)
        u_ref[HIST_ROWS:HIST_ROWS + tokens, :] = u
        u1 = u_ref[HIST_ROWS - 1:HIST_ROWS - 1 + tokens, :]
        u2 = u_ref[HIST_ROWS - 2:HIST_ROWS - 2 + tokens, :]
        conv = convw_ref[0:1, :] * u2 + convw_ref[1:2, :] * u1 + convw_ref[2:3, :] * u
        conv_out = _project(_after(hn, finished[3]), win_ref, CB) * conv * _silu(_project(hn, win_ref, CG))
        mixed_ref[:, 0:D_CONV] = conv_out.astype(bf16)
        u_ref[0:HIST_ROWS, :] = u_ref[tokens:tokens + HIST_ROWS, :]

        gate = _silu(_project(hn, win_ref, RG))
        for c in range(n_chunks):
            rows = slice(c * chunk, (c + 1) * chunk)
            for h in range(RET_HEADS):
                cols = slice(h * HEAD_DIM, (h + 1) * HEAD_DIM)
                hc = c * RET_HEADS + h
                lhs = jnp.concatenate([p_ref[hc], qx_ref[rows, cols]], axis=1)
                rhs = jnp.concatenate([v_ref[rows, cols], sbf_ref[hc]], axis=0)
                o = jnp.dot(lhs, rhs, preferred_element_type=f32)
                mu = jnp.mean(o, axis=-1, keepdims=True)
                d = o - mu
                var = jnp.mean(d * d, axis=-1, keepdims=True)
                y = d * lax.rsqrt(var + EPS) * retg_ref[:, cols]
                mixed_ref[rows, D_CONV + h * HEAD_DIM:D_CONV + (h + 1) * HEAD_DIM] = (
                    y * gate[rows, cols]).astype(bf16)

        y_ref[...] = jnp.dot(mixed_ref[...], wout_ref[...], preferred_element_type=f32) + x

    for copy in _in_copies(0, 0):
        copy.start()
    for copy in par_copies:
        copy.start()
    chunks = _weight_chunks((win_hbm, win_ref), (wout_hbm, wout_ref))
    n_win_chunks = len(_weight_chunks((win_hbm, win_ref)))
    _load_weights_as_bf16(chunks, 0, n_win_chunks, ystage_ref, stage_sem)
    for copy in par_copies:
        copy.wait()
    hm = _rms_norm(meta_ref[...], g1_ref[...]).astype(bf16)
    u_m = _project(hm, win_ref, CC) * _project(hm, win_ref, CX)
    hist0_ref[...] = u_m[N_META - HIST_ROWS:, :]
    k_m = _project(hm, win_ref, K)
    v_m = _project(hm, win_ref, V).astype(bf16)
    for h in range(RET_HEADS):
        kr = _rotary(_head(k_m, h), mrope_ref[0:N_META, :], mrope_ref[N_META:, :]) * mzeta_ref[h]
        state0_ref[h] = lax.dot_general(kr.astype(bf16), _head(v_m, h), (((0,), (0,)), ((), ())),
                                        preferred_element_type=f32)
    _load_weights_as_bf16(chunks, n_win_chunks, len(chunks), ystage_ref, stage_sem)
    y_ref[...] = jnp.zeros((tokens, D_MODEL), f32)

    def _step(s, carry):
        slot = s % 2
        prev_slot = 1 - slot
        for copy in _in_copies(s, slot):
            copy.wait()

        @pl.when(s + 1 < n_blocks)
        def _fetch_next():
            for copy in _in_copies(s + 1, prev_slot):
                copy.start()

        @pl.when(s >= 3)
        def _free_out_slot():
            _out_copy(s - 3, prev_slot).wait()

        @pl.when(s % blocks_per_seq == 0)
        def _start_sequence():
            state_ref[...] = state0_ref[...]
            u_ref[0:HIST_ROWS, :] = hist0_ref[...]

        _mix_block(xbuf.at[slot], cosbuf.at[slot], sinbuf.at[slot], obuf.at[prev_slot])

        @pl.when(s >= 1)
        def _write_back_previous():
            _out_copy(s - 1, prev_slot).start()

        return carry

    lax.fori_loop(0, n_blocks, _step, 0)

    last = n_blocks - 1
    last_slot = last % 2
    if n_blocks >= 3:
        _out_copy(last - 2, last_slot).wait()
    group = tokens // FINISH_GROUPS
    b_last, rows_last = last // blocks_per_seq, (last % blocks_per_seq) * tokens
    tail_copies = []
    for g in range(FINISH_GROUPS):
        rows = slice(g * group, (g + 1) * group)
        _finish_rows(obuf.at[last_slot], rows)
        tail_copies.append(pltpu.make_async_copy(
            obuf.at[last_slot, rows, :], out_hbm.at[b_last, pl.ds(rows_last + g * group, group), :], tail_sem.at[g]))
        tail_copies[-1].start()
    if n_blocks >= 2:
        _out_copy(last - 1, 1 - last_slot).wait()
    for copy in tail_copies:
        copy.wait()


def kernel(x, meta, norm1_g, w_in, conv_w, ret_norm_g, w_out, final_g):
    bsz, seq, d_model = x.shape
    assert d_model == D_MODEL and meta.shape == (N_META, D_MODEL)
    tokens, chunk = TOKENS_PER_STEP, RET_CHUNK
    assert seq % tokens == 0 and tokens % chunk == 0 and tokens <= STAGE_SLOTS * STAGE_ROWS

    cos, sin = _rope_tables(N_META + seq)
    decay, xi, zeta = _decay_tables(chunk)
    chunk_decay = tuple(float(v) for v in _gammas() ** chunk)
    n_hc = (tokens // chunk) * RET_HEADS

    blocks_per_seq = seq // tokens
    n_blocks = bsz * blocks_per_seq

    in_vmem = pl.BlockSpec(memory_space=pltpu.VMEM)
    in_hbm = pl.BlockSpec(memory_space=pl.ANY)
    body = functools.partial(_mixer_kernel, tokens=tokens, chunk=chunk, chunk_decay=chunk_decay,
                             blocks_per_seq=blocks_per_seq, n_blocks=n_blocks)
    return pl.pallas_call(
        body,
        in_specs=[
            in_hbm,
            in_vmem,
            in_hbm,
            in_hbm,
            in_hbm,
            in_hbm,
            in_hbm,
            in_hbm,
            in_hbm,
            in_hbm,
            in_vmem,
            in_vmem,
            in_vmem,
            in_vmem,
            in_vmem,
        ],
        out_specs=pl.BlockSpec(memory_space=pl.ANY),
        out_shape=jax.ShapeDtypeStruct((bsz, seq, D_MODEL), x.dtype),
        scratch_shapes=[
            pltpu.VMEM((D_MODEL, 8 * SEC), jnp.bfloat16),
            pltpu.VMEM((D_MODEL, D_MODEL), jnp.bfloat16),
            pltpu.VMEM((STAGE_SLOTS * STAGE_ROWS, D_MODEL), jnp.float32),
            pltpu.SemaphoreType.DMA((STAGE_SLOTS,)),
            pltpu.VMEM((PAR_ROWS, D_MODEL), jnp.float32),
            pltpu.SemaphoreType.DMA((N_PAR_COPIES,)),
            pltpu.VMEM((2, tokens, D_MODEL), jnp.float32),
            pltpu.VMEM((2, tokens, HEAD_DIM), jnp.float32),
            pltpu.VMEM((2, tokens, HEAD_DIM), jnp.float32),
            pltpu.VMEM((2, tokens, D_MODEL), jnp.float32),
            pltpu.SemaphoreType.DMA((2, 3)),
            pltpu.SemaphoreType.DMA((2,)),
            pltpu.SemaphoreType.DMA((FINISH_GROUPS,)),
            pltpu.VMEM((RET_HEADS, HEAD_DIM, HEAD_DIM), jnp.float32),
            pltpu.VMEM((RET_HEADS, HEAD_DIM, HEAD_DIM), jnp.float32),
            pltpu.VMEM((HIST_ROWS + tokens, D_CONV), jnp.float32),
            pltpu.VMEM((HIST_ROWS, D_CONV), jnp.float32),
            pltpu.VMEM((tokens, D_RET), jnp.bfloat16),
            pltpu.VMEM((tokens, D_RET), jnp.bfloat16),
            pltpu.VMEM((tokens, D_RET), jnp.bfloat16),
            pltpu.VMEM((tokens, D_RET), jnp.bfloat16),
            pltpu.VMEM((tokens, D_RET), jnp.bfloat16),
            pltpu.VMEM((n_hc, chunk, chunk), jnp.bfloat16),
            pltpu.VMEM((n_hc, HEAD_DIM, HEAD_DIM), jnp.bfloat16),
            pltpu.VMEM((tokens, D_MODEL), jnp.bfloat16),
        ],
        compiler_params=pltpu.CompilerParams(
            vmem_limit_bytes=VMEM_LIMIT_BYTES),
        name="hymba_mixer",
    )(x, meta, norm1_g.reshape(1, -1), w_in, conv_w,
      ret_norm_g.reshape(1, -1), w_out, final_g.reshape(1, -1),
      jnp.asarray(cos[N_META:]), jnp.asarray(sin[N_META:]),
      jnp.asarray(np.concatenate([cos[:N_META], sin[:N_META]], axis=0)),
      jnp.asarray(decay), jnp.asarray(xi), jnp.asarray(zeta), jnp.asarray(_meta_zeta()))
```

```python
import functools

import numpy as np
import jax
import jax.numpy as jnp
from jax import lax
from jax.experimental import pallas as pl
from jax.experimental.pallas import tpu as pltpu

D_MODEL = 1024
N_META = 16
D_CONV = 512
D_RET = 512
RET_HEADS = 4
HEAD_DIM = 128
HALF = HEAD_DIM // 2
CONV_WIDTH = 3
ROPE_BASE = 10000.0
EPS = 1e-6

SEC = 512
CX, CB, CC, CG, Q, K, V, RG = range(8)

TOKENS_PER_STEP = 1024
RET_CHUNK = 128
HIST_ROWS = 8
STAGE_ROWS = 256
STAGE_SLOTS = 12
WIN_PIECE_ORDER = (Q // 2, V // 2, CC // 2, CX // 2)
EARLY_PIECES = 2 * (D_MODEL // STAGE_ROWS)
PAR_G1, PAR_GF, PAR_RETG, PAR_CONVW, PAR_ROWS = 0, 1, 2, 3, 8
N_PAR_COPIES = 4
FINISH_GROUPS = 4
V7X_LANES = 128
V7X_F32_SUBLANES = 8
V7X_BF16_SUBLANES = 16
V7X_VMEM_BYTES = 64 * 1024 * 1024
VMEM_UNCLAIMED_BYTES = 2 * 1024 * 1024
VMEM_LIMIT_BYTES = V7X_VMEM_BYTES - VMEM_UNCLAIMED_BYTES


def _gammas():
    return 1.0 - 2.0 ** (-5.0 - np.arange(RET_HEADS, dtype=np.float64))


def _rope_tables(n_pos):
    freqs = 1.0 / (ROPE_BASE ** (np.arange(HALF, dtype=np.float64) / HALF))
    ang = np.arange(n_pos, dtype=np.float64)[:, None] * freqs[None, :]
    cos = np.concatenate([np.cos(ang), np.cos(ang)], axis=1)
    sin = np.concatenate([-np.sin(ang), np.sin(ang)], axis=1)
    return cos.astype(np.float32), sin.astype(np.float32)


def _decay_tables(chunk):
    g = _gammas()
    idx = np.arange(chunk, dtype=np.float64)
    diff = idx[:, None] - idx[None, :]
    scale = HEAD_DIM ** -0.5
    decay = np.where(diff[None] >= 0, g[:, None, None] ** np.maximum(diff[None], 0.0), 0.0) * scale
    xi = (g[:, None] ** (idx[None, :] + 1.0)) * scale
    zeta = g[:, None] ** (chunk - 1.0 - idx[None, :])
    xi = np.broadcast_to(xi[:, :, None], (RET_HEADS, chunk, HEAD_DIM))
    zeta = np.broadcast_to(zeta[:, :, None], (RET_HEADS, chunk, HEAD_DIM))
    return decay.astype(np.float32), xi.astype(np.float32), zeta.astype(np.float32)


def _meta_zeta():
    g = _gammas()
    j = np.arange(N_META, dtype=np.float64)
    z = g[:, None] ** (N_META - 1.0 - j[None, :])
    return np.broadcast_to(z[:, :, None], (RET_HEADS, N_META, HEAD_DIM)).astype(np.float32)


def _rms_norm(x, g):
    return x * lax.rsqrt(jnp.mean(x * x, axis=-1, keepdims=True) + EPS) * g


def _silu(x):
    return x * (1.0 / (1.0 + jnp.exp(-x)))


def _rotary(t, cos, sin):
    return t * cos + pltpu.roll(t, HALF, axis=1) * sin


def _project(hn_bf16, win_ref, sec):
    return jnp.dot(hn_bf16, win_ref[:, sec * SEC:(sec + 1) * SEC], preferred_element_type=jnp.float32)


def _head(t, h):
    return t[:, h * HEAD_DIM:(h + 1) * HEAD_DIM]


def _zero_tile_after(stored_f32):
    bits = pltpu.bitcast(stored_f32, jnp.uint32)
    rows, cols = bits.shape
    acc = bits[0:V7X_F32_SUBLANES]
    for r in range(V7X_F32_SUBLANES, rows, V7X_F32_SUBLANES):
        acc = acc | bits[r:r + V7X_F32_SUBLANES]
    word = acc[:, 0:V7X_LANES]
    for c in range(V7X_LANES, cols, V7X_LANES):
        word = word | acc[:, c:c + V7X_LANES]
    return pltpu.bitcast((word >> 16) >> 16, jnp.bfloat16)


def _after(lhs_bf16, zero_tile):
    r, c = V7X_BF16_SUBLANES, V7X_LANES
    top = jnp.concatenate([lhs_bf16[0:r, 0:c] + zero_tile, lhs_bf16[0:r, c:]], axis=1)
    return jnp.concatenate([top, lhs_bf16[r:]], axis=0)


def _weight_chunks(*weights):
    chunks = []
    for w_hbm, w_bf16_ref, piece_order in weights:
        n_rows, n_cols = w_hbm.shape
        assert sorted(piece_order) == list(range(n_cols // D_MODEL))
        for piece in piece_order:
            for r0 in range(0, n_rows, STAGE_ROWS):
                chunks.append((w_hbm, w_bf16_ref, r0, piece * D_MODEL))
    return chunks


def _stage_copy(chunks, idx, stage_slots, sem_ref):
    w_hbm, _, r0, c0 = chunks[idx]
    return pltpu.make_async_copy(w_hbm.at[pl.ds(r0, STAGE_ROWS), pl.ds(c0, D_MODEL)],
                                 stage_slots[idx % STAGE_SLOTS], sem_ref.at[idx % STAGE_SLOTS])


def _load_weights_as_bf16(chunks, first, last, stage_slots, sem_ref):
    if first == 0:
        for i in range(min(STAGE_SLOTS, len(chunks))):
            _stage_copy(chunks, i, stage_slots, sem_ref).start()
    for i in range(first, last):
        _, w_bf16_ref, r0, c0 = chunks[i]
        _stage_copy(chunks, i, stage_slots, sem_ref).wait()
        w_bf16_ref[r0:r0 + STAGE_ROWS, c0:c0 + D_MODEL] = stage_slots[i % STAGE_SLOTS][...].astype(jnp.bfloat16)
        if i + STAGE_SLOTS < len(chunks):
            _stage_copy(chunks, i + STAGE_SLOTS, stage_slots, sem_ref).start()


def _mixer_kernel(x_hbm, meta_ref, g1_hbm, win_hbm, convw_hbm, retg_hbm, wout_hbm, gf_hbm,
                  cos_hbm, sin_hbm, mrope_ref, decay_ref, xi_ref, zeta_ref, mzeta_ref,
                  out_hbm,
                  win_ref, wout_ref, y_ref, stage_sem, par_ref, par_sem,
                  xbuf, cosbuf, sinbuf, obuf, in_sem, out_sem, tail_sem,
                  state_ref, state0_ref, u_ref, hist0_ref, q_ref, qx_ref, k_ref, kz_ref, v_ref,
                  p_ref, sbf_ref, mixed_ref,
                  *, tokens, chunk, chunk_decay, blocks_per_seq, n_blocks):
    bf16 = jnp.bfloat16
    f32 = jnp.float32
    row_pieces = range(0, tokens, STAGE_ROWS)
    stage_slots = ([y_ref.at[r:r + STAGE_ROWS, :] for r in row_pieces]
                   + [obuf.at[slot, r:r + STAGE_ROWS, :] for slot in range(2) for r in row_pieces])
    assert len(stage_slots) == STAGE_SLOTS
    g1_ref = par_ref.at[PAR_G1:PAR_G1 + 1, :]
    gf_ref = par_ref.at[PAR_GF:PAR_GF + 1, :]
    retg_ref = par_ref.at[PAR_RETG:PAR_RETG + 1, 0:D_RET]
    convw_ref = par_ref.at[PAR_CONVW:PAR_CONVW + CONV_WIDTH, 0:D_CONV]
    par_copies = [pltpu.make_async_copy(src, dst, par_sem.at[i]) for i, (src, dst) in enumerate(
        ((g1_hbm, g1_ref), (gf_hbm, gf_ref), (retg_hbm, retg_ref), (convw_hbm, convw_ref)))]

    def _block_rows(blk):
        return blk // blocks_per_seq, pl.ds((blk % blocks_per_seq) * tokens, tokens)

    def _in_copies(blk, slot):
        b, rows = _block_rows(blk)
        return (pltpu.make_async_copy(x_hbm.at[b, rows, :], xbuf.at[slot], in_sem.at[slot, 0]),
                pltpu.make_async_copy(cos_hbm.at[rows, :], cosbuf.at[slot], in_sem.at[slot, 1]),
                pltpu.make_async_copy(sin_hbm.at[rows, :], sinbuf.at[slot], in_sem.at[slot, 2]))

    def _out_copy(blk, slot):
        b, rows = _block_rows(blk)
        return pltpu.make_async_copy(obuf.at[slot], out_hbm.at[b, rows, :], out_sem.at[slot])

    def _finish_rows(out_ref, rows):
        out_ref[rows, :] = _rms_norm(y_ref[rows, :], gf_ref[...])

    def _mix_block(x_ref, cos_ref, sin_ref, out_ref, before_conv=None):
        finished = []
        if out_ref is not None:
            group = tokens // FINISH_GROUPS
            for g in range(FINISH_GROUPS):
                rows = slice(g * group, (g + 1) * group)
                _finish_rows(out_ref, rows)
                finished.append(_zero_tile_after(out_ref[rows, :]))

        def _after_finished(lhs, g):
            return _after(lhs, finished[g]) if finished else lhs

        x = x_ref[...]
        hn = _rms_norm(x, g1_ref[...]).astype(bf16)

        n_chunks = tokens // chunk

        cos = cos_ref[...]
        sin = sin_ref[...]
        q = _project(hn, win_ref, Q)
        k = _project(hn, win_ref, K)
        for h in range(RET_HEADS):
            cols = slice(h * HEAD_DIM, (h + 1) * HEAD_DIM)
            qr = _rotary(_head(q, h), cos, sin)
            kr = _rotary(_head(k, h), cos, sin)
            q_ref[:, cols] = qr.astype(bf16)
            k_ref[:, cols] = kr.astype(bf16)
            for c in range(n_chunks):
                rows = slice(c * chunk, (c + 1) * chunk)
                qx_ref[rows, cols] = (qr[rows] * xi_ref[h]).astype(bf16)
                kz_ref[rows, cols] = (kr[rows] * zeta_ref[h]).astype(bf16)
        v_ref[...] = _project(_after_finished(hn, 0), win_ref, V).astype(bf16)

        upd = {}
        for c in range(n_chunks):
            rows = slice(c * chunk, (c + 1) * chunk)
            for h in range(RET_HEADS):
                cols = slice(h * HEAD_DIM, (h + 1) * HEAD_DIM)
                scores = lax.dot_general(q_ref[rows, cols], k_ref[rows, cols], (((1,), (1,)), ((), ())),
                                         preferred_element_type=f32)
                p_ref[c * RET_HEADS + h] = (scores * decay_ref[h]).astype(bf16)
                upd[c, h] = lax.dot_general(kz_ref[rows, cols], v_ref[rows, cols], (((0,), (0,)), ((), ())),
                                            preferred_element_type=f32)
        for h in range(RET_HEADS):
            state = state_ref[h]
            for c in range(n_chunks):
                sbf_ref[c * RET_HEADS + h] = state.astype(bf16)
                state = chunk_decay[h] * state + upd[c, h]
            state_ref[h] = state

        if before_conv is not None:
            before_conv()

        u = _project(_after_finished(hn, 1), win_ref, CC) * _project(_after_finished(hn, 2), win_ref, CX)
        u_ref[HIST_ROWS:HIST_ROWS + tokens, :] = u
        u1 = u_ref[HIST_ROWS - 1:HIST_ROWS - 1 + tokens, :]
        u2 = u_ref[HIST_ROWS - 2:HIST_ROWS - 2 + tokens, :]
        conv = convw_ref[0:1, :] * u2 + convw_ref[1:2, :] * u1 + convw_ref[2:3, :] * u
        conv_out = _project(_after_finished(hn, 3), win_ref, CB) * conv * _silu(_project(hn, win_ref, CG))
        mixed_ref[:, 0:D_CONV] = conv_out.astype(bf16)
        u_ref[0:HIST_ROWS, :] = u_ref[tokens:tokens + HIST_ROWS, :]

        gate = _silu(_project(hn, win_ref, RG))
        for c in range(n_chunks):
            rows = slice(c * chunk, (c + 1) * chunk)
            for h in range(RET_HEADS):
                cols = slice(h * HEAD_DIM, (h + 1) * HEAD_DIM)
                hc = c * RET_HEADS + h
                lhs = jnp.concatenate([p_ref[hc], qx_ref[rows, cols]], axis=1)
                rhs = jnp.concatenate([v_ref[rows, cols], sbf_ref[hc]], axis=0)
                o = jnp.dot(lhs, rhs, preferred_element_type=f32)
                mu = jnp.mean(o, axis=-1, keepdims=True)
                d = o - mu
                var = jnp.mean(d * d, axis=-1, keepdims=True)
                y = d * lax.rsqrt(var + EPS) * retg_ref[:, cols]
                mixed_ref[rows, D_CONV + h * HEAD_DIM:D_CONV + (h + 1) * HEAD_DIM] = (
                    y * gate[rows, cols]).astype(bf16)

        y_ref[...] = jnp.dot(mixed_ref[...], wout_ref[...], preferred_element_type=f32) + x

    for copy in _in_copies(0, 0):
        copy.start()
    for copy in par_copies:
        copy.start()
    chunks = _weight_chunks((win_hbm, win_ref, WIN_PIECE_ORDER), (wout_hbm, wout_ref, (0,)))
    _load_weights_as_bf16(chunks, 0, EARLY_PIECES, stage_slots, stage_sem)
    for copy in par_copies:
        copy.wait()
    hm = _rms_norm(meta_ref[...], g1_ref[...]).astype(bf16)
    k_m = _project(hm, win_ref, K)
    v_m = _project(hm, win_ref, V).astype(bf16)
    for h in range(RET_HEADS):
        kr = _rotary(_head(k_m, h), mrope_ref[0:N_META, :], mrope_ref[N_META:, :]) * mzeta_ref[h]
        state0_ref[h] = lax.dot_general(kr.astype(bf16), _head(v_m, h), (((0,), (0,)), ((), ())),
                                        preferred_element_type=f32)

    def _rest_of_weights():
        _load_weights_as_bf16(chunks, EARLY_PIECES, len(chunks), stage_slots, stage_sem)
        u_m = _project(hm, win_ref, CC) * _project(hm, win_ref, CX)
        hist0_ref[...] = u_m[N_META - HIST_ROWS:, :]
        u_ref[0:HIST_ROWS, :] = hist0_ref[...]

    for copy in _in_copies(0, 0):
        copy.wait()
    if n_blocks > 1:
        for copy in _in_copies(1, 1):
            copy.start()
    state_ref[...] = state0_ref[...]
    _mix_block(xbuf.at[0], cosbuf.at[0], sinbuf.at[0], None, before_conv=_rest_of_weights)

    def _step(s, carry):
        slot = s % 2
        prev_slot = 1 - slot
        for copy in _in_copies(s, slot):
            copy.wait()

        @pl.when(s + 1 < n_blocks)
        def _fetch_next():
            for copy in _in_copies(s + 1, prev_slot):
                copy.start()

        @pl.when(s >= 3)
        def _free_out_slot():
            _out_copy(s - 3, prev_slot).wait()

        @pl.when(s % blocks_per_seq == 0)
        def _start_sequence():
            state_ref[...] = state0_ref[...]
            u_ref[0:HIST_ROWS, :] = hist0_ref[...]

        _mix_block(xbuf.at[slot], cosbuf.at[slot], sinbuf.at[slot], obuf.at[prev_slot])
        _out_copy(s - 1, prev_slot).start()
        return carry

    lax.fori_loop(1, n_blocks, _step, 0)

    last = n_blocks - 1
    last_slot = last % 2
    if n_blocks >= 3:
        _out_copy(last - 2, last_slot).wait()
    group = tokens // FINISH_GROUPS
    b_last, rows_last = last // blocks_per_seq, (last % blocks_per_seq) * tokens
    tail_copies = []
    for g in range(FINISH_GROUPS):
        rows = slice(g * group, (g + 1) * group)
        _finish_rows(obuf.at[last_slot], rows)
        tail_copies.append(pltpu.make_async_copy(
            obuf.at[last_slot, rows, :], out_hbm.at[b_last, pl.ds(rows_last + g * group, group), :], tail_sem.at[g]))
        tail_copies[-1].start()
    if n_blocks >= 2:
        _out_copy(last - 1, 1 - last_slot).wait()
    for copy in tail_copies:
        copy.wait()


def kernel(x, meta, norm1_g, w_in, conv_w, ret_norm_g, w_out, final_g):
    bsz, seq, d_model = x.shape
    assert d_model == D_MODEL and meta.shape == (N_META, D_MODEL)
    tokens, chunk = TOKENS_PER_STEP, RET_CHUNK
    assert seq % tokens == 0 and tokens % chunk == 0 and 3 * tokens == STAGE_SLOTS * STAGE_ROWS

    cos, sin = _rope_tables(N_META + seq)
    decay, xi, zeta = _decay_tables(chunk)
    chunk_decay = tuple(float(v) for v in _gammas() ** chunk)
    n_hc = (tokens // chunk) * RET_HEADS

    blocks_per_seq = seq // tokens
    n_blocks = bsz * blocks_per_seq

    in_vmem = pl.BlockSpec(memory_space=pltpu.VMEM)
    in_hbm = pl.BlockSpec(memory_space=pl.ANY)
    body = functools.partial(_mixer_kernel, tokens=tokens, chunk=chunk, chunk_decay=chunk_decay,
                             blocks_per_seq=blocks_per_seq, n_blocks=n_blocks)
    return pl.pallas_call(
        body,
        in_specs=[
            in_hbm,
            in_vmem,
            in_hbm,
            in_hbm,
            in_hbm,
            in_hbm,
            in_hbm,
            in_hbm,
            in_hbm,
            in_hbm,
            in_vmem,
            in_vmem,
            in_vmem,
            in_vmem,
            in_vmem,
        ],
        out_specs=pl.BlockSpec(memory_space=pl.ANY),
        out_shape=jax.ShapeDtypeStruct((bsz, seq, D_MODEL), x.dtype),
        scratch_shapes=[
            pltpu.VMEM((D_MODEL, 8 * SEC), jnp.bfloat16),
            pltpu.VMEM((D_MODEL, D_MODEL), jnp.bfloat16),
            pltpu.VMEM((tokens, D_MODEL), jnp.float32),
            pltpu.SemaphoreType.DMA((STAGE_SLOTS,)),
            pltpu.VMEM((PAR_ROWS, D_MODEL), jnp.float32),
            pltpu.SemaphoreType.DMA((N_PAR_COPIES,)),
            pltpu.VMEM((2, tokens, D_MODEL), jnp.float32),
            pltpu.VMEM((2, tokens, HEAD_DIM), jnp.float32),
            pltpu.VMEM((2, tokens, HEAD_DIM), jnp.float32),
            pltpu.VMEM((2, tokens, D_MODEL), jnp.float32),
            pltpu.SemaphoreType.DMA((2, 3)),
            pltpu.SemaphoreType.DMA((2,)),
            pltpu.SemaphoreType.DMA((FINISH_GROUPS,)),
            pltpu.VMEM((RET_HEADS, HEAD_DIM, HEAD_DIM), jnp.float32),
            pltpu.VMEM((RET_HEADS, HEAD_DIM, HEAD_DIM), jnp.float32),
            pltpu.VMEM((HIST_ROWS + tokens, D_CONV), jnp.float32),
            pltpu.VMEM((HIST_ROWS, D_CONV), jnp.float32),
            pltpu.VMEM((tokens, D_RET), jnp.bfloat16),
            pltpu.VMEM((tokens, D_RET), jnp.bfloat16),
            pltpu.VMEM((tokens, D_RET), jnp.bfloat16),
            pltpu.VMEM((tokens, D_RET), jnp.bfloat16),
            pltpu.VMEM((tokens, D_RET), jnp.bfloat16),
            pltpu.VMEM((n_hc, chunk, chunk), jnp.bfloat16),
            pltpu.VMEM((n_hc, HEAD_DIM, HEAD_DIM), jnp.bfloat16),
            pltpu.VMEM((tokens, D_MODEL), jnp.bfloat16),
        ],
        compiler_params=pltpu.CompilerParams(
            vmem_limit_bytes=VMEM_LIMIT_BYTES),
        name="hymba_mixer",
    )(x, meta, norm1_g.reshape(1, -1), w_in, conv_w,
      ret_norm_g.reshape(1, -1), w_out, final_g.reshape(1, -1),
      jnp.asarray(cos[N_META:]), jnp.asarray(sin[N_META:]),
      jnp.asarray(np.concatenate([cos[:N_META], sin[:N_META]], axis=0)),
      jnp.asarray(decay), jnp.asarray(xi), jnp.asarray(zeta), jnp.asarray(_meta_zeta()))
```

```python
import functools

import numpy as np
import jax
import jax.numpy as jnp
from jax import lax
from jax.experimental import pallas as pl
from jax.experimental.pallas import tpu as pltpu

D_MODEL = 1024
N_META = 16
D_CONV = 512
D_RET = 512
RET_HEADS = 4
HEAD_DIM = 128
HALF = HEAD_DIM // 2
CONV_WIDTH = 3
ROPE_BASE = 10000.0
EPS = 1e-6

SEC = 512
CX, CB, CC, CG, Q, K, V, RG = range(8)

TOKENS_PER_STEP = 1024
RET_CHUNK = 128
HIST_ROWS = 8
STAGE_ROWS = 256
STAGE_SLOTS = 12
WIN_PIECE_ORDER = (Q // 2, V // 2, CC // 2, CX // 2)
EARLY_PIECES = 2 * (D_MODEL // STAGE_ROWS)
PAR_G1, PAR_GF, PAR_RETG, PAR_CONVW, PAR_ROWS = 0, 1, 2, 3, 8
N_PAR_COPIES = 4
FINISH_GROUPS = 4
V7X_LANES = 128
V7X_F32_SUBLANES = 8
V7X_BF16_SUBLANES = 16
V7X_VMEM_BYTES = 64 * 1024 * 1024
VMEM_UNCLAIMED_BYTES = 2 * 1024 * 1024
VMEM_LIMIT_BYTES = V7X_VMEM_BYTES - VMEM_UNCLAIMED_BYTES


def _gammas():
    return 1.0 - 2.0 ** (-5.0 - np.arange(RET_HEADS, dtype=np.float64))


def _rope_tables(n_pos):
    freqs = 1.0 / (ROPE_BASE ** (np.arange(HALF, dtype=np.float64) / HALF))
    ang = np.arange(n_pos, dtype=np.float64)[:, None] * freqs[None, :]
    cos = np.concatenate([np.cos(ang), np.cos(ang)], axis=1)
    sin = np.concatenate([-np.sin(ang), np.sin(ang)], axis=1)
    return cos.astype(np.float32), sin.astype(np.float32)


def _decay_tables(chunk):
    g = _gammas()
    idx = np.arange(chunk, dtype=np.float64)
    diff = idx[:, None] - idx[None, :]
    scale = HEAD_DIM ** -0.5
    decay = np.where(diff[None] >= 0, g[:, None, None] ** np.maximum(diff[None], 0.0), 0.0) * scale
    xi = (g[:, None] ** (idx[None, :] + 1.0)) * scale
    zeta = g[:, None] ** (chunk - 1.0 - idx[None, :])
    xi = np.broadcast_to(xi[:, :, None], (RET_HEADS, chunk, HEAD_DIM))
    zeta = np.broadcast_to(zeta[:, :, None], (RET_HEADS, chunk, HEAD_DIM))
    return decay.astype(np.float32), xi.astype(np.float32), zeta.astype(np.float32)


def _meta_zeta():
    g = _gammas()
    j = np.arange(N_META, dtype=np.float64)
    z = g[:, None] ** (N_META - 1.0 - j[None, :])
    return np.broadcast_to(z[:, :, None], (RET_HEADS, N_META, HEAD_DIM)).astype(np.float32)


def _rms_norm(x, g):
    return x * lax.rsqrt(jnp.mean(x * x, axis=-1, keepdims=True) + EPS) * g


def _silu(x):
    return x * (1.0 / (1.0 + jnp.exp(-x)))


def _rotary(t, cos, sin):
    return t * cos + pltpu.roll(t, HALF, axis=1) * sin


def _project(hn_bf16, win_ref, sec):
    return jnp.dot(hn_bf16, win_ref[:, sec * SEC:(sec + 1) * SEC], preferred_element_type=jnp.float32)


def _head(t, h):
    return t[:, h * HEAD_DIM:(h + 1) * HEAD_DIM]


def _zero_tile_after(stored_f32):
    bits = pltpu.bitcast(stored_f32, jnp.uint32)
    rows, cols = bits.shape
    acc = bits[0:V7X_F32_SUBLANES]
    for r in range(V7X_F32_SUBLANES, rows, V7X_F32_SUBLANES):
        acc = acc | bits[r:r + V7X_F32_SUBLANES]
    word = acc[:, 0:V7X_LANES]
    for c in range(V7X_LANES, cols, V7X_LANES):
        word = word | acc[:, c:c + V7X_LANES]
    return pltpu.bitcast((word >> 16) >> 16, jnp.bfloat16)


def _after(lhs_bf16, zero_tile):
    r, c = V7X_BF16_SUBLANES, V7X_LANES
    top = jnp.concatenate([lhs_bf16[0:r, 0:c] + zero_tile, lhs_bf16[0:r, c:]], axis=1)
    return jnp.concatenate([top, lhs_bf16[r:]], axis=0)


def _weight_chunks(*weights):
    chunks = []
    for w_hbm, w_bf16_ref, piece_order in weights:
        n_rows, n_cols = w_hbm.shape
        assert sorted(piece_order) == list(range(n_cols // D_MODEL))
        for piece in piece_order:
            for r0 in range(0, n_rows, STAGE_ROWS):
                chunks.append((w_hbm, w_bf16_ref, r0, piece * D_MODEL))
    return chunks


def _stage_copy(chunks, idx, stage_slots, sem_ref):
    w_hbm, _, r0, c0 = chunks[idx]
    return pltpu.make_async_copy(w_hbm.at[pl.ds(r0, STAGE_ROWS), pl.ds(c0, D_MODEL)],
                                 stage_slots[idx % STAGE_SLOTS], sem_ref.at[idx % STAGE_SLOTS])


def _fill_stage_slots(chunks, first, last, stage_slots, sem_ref):
    assert last <= STAGE_SLOTS
    for i in range(first, min(last, len(chunks))):
        _stage_copy(chunks, i, stage_slots, sem_ref).start()


def _load_weights_as_bf16(chunks, first, last, stage_slots, sem_ref):
    for i in range(first, last):
        _, w_bf16_ref, r0, c0 = chunks[i]
        _stage_copy(chunks, i, stage_slots, sem_ref).wait()
        w_bf16_ref[r0:r0 + STAGE_ROWS, c0:c0 + D_MODEL] = stage_slots[i % STAGE_SLOTS][...].astype(jnp.bfloat16)
        if i + STAGE_SLOTS < len(chunks):
            _stage_copy(chunks, i + STAGE_SLOTS, stage_slots, sem_ref).start()


def _mixer_kernel(x_hbm, meta_ref, g1_hbm, win_hbm, convw_hbm, retg_hbm, wout_hbm, gf_hbm,
                  cos_hbm, sin_hbm, mrope_ref, decay_ref, xi_ref, zeta_ref, mzeta_ref,
                  out_hbm,
                  win_ref, wout_ref, y_ref, stage_sem, par_ref, par_sem,
                  xbuf, cosbuf, sinbuf, obuf, in_sem, out_sem, tail_sem,
                  state_ref, state0_ref, u_ref, hist0_ref, q_ref, qx_ref, k_ref, kz_ref, v_ref,
                  p_ref, sbf_ref, mixed_ref,
                  *, tokens, chunk, chunk_decay, blocks_per_seq, n_blocks):
    bf16 = jnp.bfloat16
    f32 = jnp.float32
    row_pieces = range(0, tokens, STAGE_ROWS)
    stage_slots = ([y_ref.at[r:r + STAGE_ROWS, :] for r in row_pieces]
                   + [obuf.at[slot, r:r + STAGE_ROWS, :] for slot in range(2) for r in row_pieces])
    assert len(stage_slots) == STAGE_SLOTS
    g1_ref = par_ref.at[PAR_G1:PAR_G1 + 1, :]
    gf_ref = par_ref.at[PAR_GF:PAR_GF + 1, :]
    retg_ref = par_ref.at[PAR_RETG:PAR_RETG + 1, 0:D_RET]
    convw_ref = par_ref.at[PAR_CONVW:PAR_CONVW + CONV_WIDTH, 0:D_CONV]
    par_copies = [pltpu.make_async_copy(src, dst, par_sem.at[i]) for i, (src, dst) in enumerate(
        ((g1_hbm, g1_ref), (gf_hbm, gf_ref), (retg_hbm, retg_ref), (convw_hbm, convw_ref)))]

    def _block_rows(blk):
        return blk // blocks_per_seq, pl.ds((blk % blocks_per_seq) * tokens, tokens)

    def _in_copies(blk, slot):
        b, rows = _block_rows(blk)
        return (pltpu.make_async_copy(x_hbm.at[b, rows, :], xbuf.at[slot], in_sem.at[slot, 0]),
                pltpu.make_async_copy(cos_hbm.at[rows, :], cosbuf.at[slot], in_sem.at[slot, 1]),
                pltpu.make_async_copy(sin_hbm.at[rows, :], sinbuf.at[slot], in_sem.at[slot, 2]))

    def _out_copy(blk, slot):
        b, rows = _block_rows(blk)
        return pltpu.make_async_copy(obuf.at[slot], out_hbm.at[b, rows, :], out_sem.at[slot])

    def _finish_rows(out_ref, rows):
        out_ref[rows, :] = _rms_norm(y_ref[rows, :], gf_ref[...])

    def _mix_block(x_ref, cos_ref, sin_ref, out_ref, fetch_weights=None):
        before_qk, before_v, before_conv = fetch_weights or (lambda: None,) * 3
        finished = []
        if out_ref is not None:
            group = tokens // FINISH_GROUPS
            for g in range(FINISH_GROUPS):
                rows = slice(g * group, (g + 1) * group)
                _finish_rows(out_ref, rows)
                finished.append(_zero_tile_after(out_ref[rows, :]))

        def _after_finished(lhs, g):
            return _after(lhs, finished[g]) if finished else lhs

        x = x_ref[...]
        hn = _rms_norm(x, g1_ref[...]).astype(bf16)

        n_chunks = tokens // chunk

        cos = cos_ref[...]
        sin = sin_ref[...]
        before_qk()
        q = _project(hn, win_ref, Q)
        k = _project(hn, win_ref, K)
        before_v()
        for h in range(RET_HEADS):
            cols = slice(h * HEAD_DIM, (h + 1) * HEAD_DIM)
            qr = _rotary(_head(q, h), cos, sin)
            kr = _rotary(_head(k, h), cos, sin)
            q_ref[:, cols] = qr.astype(bf16)
            k_ref[:, cols] = kr.astype(bf16)
            for c in range(n_chunks):
                rows = slice(c * chunk, (c + 1) * chunk)
                qx_ref[rows, cols] = (qr[rows] * xi_ref[h]).astype(bf16)
                kz_ref[rows, cols] = (kr[rows] * zeta_ref[h]).astype(bf16)
        v_ref[...] = _project(_after_finished(hn, 0), win_ref, V).astype(bf16)

        upd = {}
        for c in range(n_chunks):
            rows = slice(c * chunk, (c + 1) * chunk)
            for h in range(RET_HEADS):
                cols = slice(h * HEAD_DIM, (h + 1) * HEAD_DIM)
                scores = lax.dot_general(q_ref[rows, cols], k_ref[rows, cols], (((1,), (1,)), ((), ())),
                                         preferred_element_type=f32)
                p_ref[c * RET_HEADS + h] = (scores * decay_ref[h]).astype(bf16)
                upd[c, h] = lax.dot_general(kz_ref[rows, cols], v_ref[rows, cols], (((0,), (0,)), ((), ())),
                                            preferred_element_type=f32)
        for h in range(RET_HEADS):
            state = state_ref[h]
            for c in range(n_chunks):
                sbf_ref[c * RET_HEADS + h] = state.astype(bf16)
                state = chunk_decay[h] * state + upd[c, h]
            state_ref[h] = state

        before_conv()

        u = _project(_after_finished(hn, 1), win_ref, CC) * _project(_after_finished(hn, 2), win_ref, CX)
        u_ref[HIST_ROWS:HIST_ROWS + tokens, :] = u
        u1 = u_ref[HIST_ROWS - 1:HIST_ROWS - 1 + tokens, :]
        u2 = u_ref[HIST_ROWS - 2:HIST_ROWS - 2 + tokens, :]
        conv = convw_ref[0:1, :] * u2 + convw_ref[1:2, :] * u1 + convw_ref[2:3, :] * u
        conv_out = _project(_after_finished(hn, 3), win_ref, CB) * conv * _silu(_project(hn, win_ref, CG))
        mixed_ref[:, 0:D_CONV] = conv_out.astype(bf16)
        u_ref[0:HIST_ROWS, :] = u_ref[tokens:tokens + HIST_ROWS, :]

        gate = _silu(_project(hn, win_ref, RG))
        for c in range(n_chunks):
            rows = slice(c * chunk, (c + 1) * chunk)
            for h in range(RET_HEADS):
                cols = slice(h * HEAD_DIM, (h + 1) * HEAD_DIM)
                hc = c * RET_HEADS + h
                lhs = jnp.concatenate([p_ref[hc], qx_ref[rows, cols]], axis=1)
                rhs = jnp.concatenate([v_ref[rows, cols], sbf_ref[hc]], axis=0)
                o = jnp.dot(lhs, rhs, preferred_element_type=f32)
                mu = jnp.mean(o, axis=-1, keepdims=True)
                d = o - mu
                var = jnp.mean(d * d, axis=-1, keepdims=True)
                y = d * lax.rsqrt(var + EPS) * retg_ref[:, cols]
                mixed_ref[rows, D_CONV + h * HEAD_DIM:D_CONV + (h + 1) * HEAD_DIM] = (
                    y * gate[rows, cols]).astype(bf16)

        y_ref[...] = jnp.dot(mixed_ref[...], wout_ref[...], preferred_element_type=f32) + x

    for copy in par_copies:
        copy.start()
    for copy in _in_copies(0, 0):
        copy.start()
    chunks = _weight_chunks((win_hbm, win_ref, WIN_PIECE_ORDER), (wout_hbm, wout_ref, (0,)))
    _fill_stage_slots(chunks, 0, STAGE_SLOTS, stage_slots, stage_sem)
    for copy in par_copies:
        copy.wait()
    for copy in _in_copies(0, 0):
        copy.wait()

    hm = _rms_norm(meta_ref[...], g1_ref[...]).astype(bf16)
    column_piece = D_MODEL // STAGE_ROWS

    def _weights_for_qk():
        _load_weights_as_bf16(chunks, 0, column_piece, stage_slots, stage_sem)

    def _weights_for_v():
        _load_weights_as_bf16(chunks, column_piece, EARLY_PIECES, stage_slots, stage_sem)
        k_m = _project(hm, win_ref, K)
        v_m = _project(hm, win_ref, V).astype(bf16)
        for h in range(RET_HEADS):
            kr = _rotary(_head(k_m, h), mrope_ref[0:N_META, :], mrope_ref[N_META:, :]) * mzeta_ref[h]
            state0_ref[h] = lax.dot_general(kr.astype(bf16), _head(v_m, h), (((0,), (0,)), ((), ())),
                                            preferred_element_type=f32)
        state_ref[...] = state0_ref[...]

    def _rest_of_weights():
        _load_weights_as_bf16(chunks, EARLY_PIECES, len(chunks), stage_slots, stage_sem)
        u_m = _project(hm, win_ref, CC) * _project(hm, win_ref, CX)
        hist0_ref[...] = u_m[N_META - HIST_ROWS:, :]
        u_ref[0:HIST_ROWS, :] = hist0_ref[...]
        if n_blocks > 1:
            for copy in _in_copies(1, 1):
                copy.start()

    _mix_block(xbuf.at[0], cosbuf.at[0], sinbuf.at[0], None,
               fetch_weights=(_weights_for_qk, _weights_for_v, _rest_of_weights))

    def _step(s, carry):
        slot = s % 2
        prev_slot = 1 - slot
        for copy in _in_copies(s, slot):
            copy.wait()

        @pl.when(s + 1 < n_blocks)
        def _fetch_next():
            for copy in _in_copies(s + 1, prev_slot):
                copy.start()

        @pl.when(s >= 3)
        def _free_out_slot():
            _out_copy(s - 3, prev_slot).wait()

        @pl.when(s % blocks_per_seq == 0)
        def _start_sequence():
            state_ref[...] = state0_ref[...]
            u_ref[0:HIST_ROWS, :] = hist0_ref[...]

        _mix_block(xbuf.at[slot], cosbuf.at[slot], sinbuf.at[slot], obuf.at[prev_slot])
        _out_copy(s - 1, prev_slot).start()
        return carry

    lax.fori_loop(1, n_blocks, _step, 0)

    last = n_blocks - 1
    last_slot = last % 2
    if n_blocks >= 3:
        _out_copy(last - 2, last_slot).wait()
    group = tokens // FINISH_GROUPS
    b_last, rows_last = last // blocks_per_seq, (last % blocks_per_seq) * tokens
    tail_copies = []
    for g in range(FINISH_GROUPS):
        rows = slice(g * group, (g + 1) * group)
        _finish_rows(obuf.at[last_slot], rows)
        tail_copies.append(pltpu.make_async_copy(
            obuf.at[last_slot, rows, :], out_hbm.at[b_last, pl.ds(rows_last + g * group, group), :], tail_sem.at[g]))
        tail_copies[-1].start()
    if n_blocks >= 2:
        _out_copy(last - 1, 1 - last_slot).wait()
    for copy in tail_copies:
        copy.wait()


def kernel(x, meta, norm1_g, w_in, conv_w, ret_norm_g, w_out, final_g):
    bsz, seq, d_model = x.shape
    assert d_model == D_MODEL and meta.shape == (N_META, D_MODEL)
    tokens, chunk = TOKENS_PER_STEP, RET_CHUNK
    assert seq % tokens == 0 and tokens % chunk == 0 and 3 * tokens == STAGE_SLOTS * STAGE_ROWS

    cos, sin = _rope_tables(N_META + seq)
    decay, xi, zeta = _decay_tables(chunk)
    chunk_decay = tuple(float(v) for v in _gammas() ** chunk)
    n_hc = (tokens // chunk) * RET_HEADS

    blocks_per_seq = seq // tokens
    n_blocks = bsz * blocks_per_seq

    in_vmem = pl.BlockSpec(memory_space=pltpu.VMEM)
    in_hbm = pl.BlockSpec(memory_space=pl.ANY)
    body = functools.partial(_mixer_kernel, tokens=tokens, chunk=chunk, chunk_decay=chunk_decay,
                             blocks_per_seq=blocks_per_seq, n_blocks=n_blocks)
    return pl.pallas_call(
        body,
        in_specs=[
            in_hbm,
            in_vmem,
            in_hbm,
            in_hbm,
            in_hbm,
            in_hbm,
            in_hbm,
            in_hbm,
            in_hbm,
            in_hbm,
            in_vmem,
            in_vmem,
            in_vmem,
            in_vmem,
            in_vmem,
        ],
        out_specs=pl.BlockSpec(memory_space=pl.ANY),
        out_shape=jax.ShapeDtypeStruct((bsz, seq, D_MODEL), x.dtype),
        scratch_shapes=[
            pltpu.VMEM((D_MODEL, 8 * SEC), jnp.bfloat16),
            pltpu.VMEM((D_MODEL, D_MODEL), jnp.bfloat16),
            pltpu.VMEM((tokens, D_MODEL), jnp.float32),
            pltpu.SemaphoreType.DMA((STAGE_SLOTS,)),
            pltpu.VMEM((PAR_ROWS, D_MODEL), jnp.float32),
            pltpu.SemaphoreType.DMA((N_PAR_COPIES,)),
            pltpu.VMEM((2, tokens, D_MODEL), jnp.float32),
            pltpu.VMEM((2, tokens, HEAD_DIM), jnp.float32),
            pltpu.VMEM((2, tokens, HEAD_DIM), jnp.float32),
            pltpu.VMEM((2, tokens, D_MODEL), jnp.float32),
            pltpu.SemaphoreType.DMA((2, 3)),
            pltpu.SemaphoreType.DMA((2,)),
            pltpu.SemaphoreType.DMA((FINISH_GROUPS,)),
            pltpu.VMEM((RET_HEADS, HEAD_DIM, HEAD_DIM), jnp.float32),
            pltpu.VMEM((RET_HEADS, HEAD_DIM, HEAD_DIM), jnp.float32),
            pltpu.VMEM((HIST_ROWS + tokens, D_CONV), jnp.float32),
            pltpu.VMEM((HIST_ROWS, D_CONV), jnp.float32),
            pltpu.VMEM((tokens, D_RET), jnp.bfloat16),
            pltpu.VMEM((tokens, D_RET), jnp.bfloat16),
            pltpu.VMEM((tokens, D_RET), jnp.bfloat16),
            pltpu.VMEM((tokens, D_RET), jnp.bfloat16),
            pltpu.VMEM((tokens, D_RET), jnp.bfloat16),
            pltpu.VMEM((n_hc, chunk, chunk), jnp.bfloat16),
            pltpu.VMEM((n_hc, HEAD_DIM, HEAD_DIM), jnp.bfloat16),
            pltpu.VMEM((tokens, D_MODEL), jnp.bfloat16),
        ],
        compiler_params=pltpu.CompilerParams(
            vmem_limit_bytes=VMEM_LIMIT_BYTES),
        name="hymba_mixer",
    )(x, meta, norm1_g.reshape(1, -1), w_in, conv_w,
      ret_norm_g.reshape(1, -1), w_out, final_g.reshape(1, -1),
      jnp.asarray(cos[N_META:]), jnp.asarray(sin[N_META:]),
      jnp.asarray(np.concatenate([cos[:N_META], sin[:N_META]], axis=0)),
      jnp.asarray(decay), jnp.asarray(xi), jnp.asarray(zeta), jnp.asarray(_meta_zeta()))
```

```python
import functools

import numpy as np
import jax
import jax.numpy as jnp
from jax import lax
from jax.experimental import pallas as pl
from jax.experimental.pallas import tpu as pltpu

D_MODEL = 1024
N_META = 16
D_CONV = 512
D_RET = 512
RET_HEADS = 4
HEAD_DIM = 128
HALF = HEAD_DIM // 2
CONV_WIDTH = 3
ROPE_BASE = 10000.0
EPS = 1e-6

SEC = 512
CX, CB, CC, CG, Q, K, V, RG = range(8)

TOKENS_PER_STEP = 1024
RET_CHUNK = 128
HIST_ROWS = 8
STAGE_ROWS = 256
STAGE_SLOTS = 12
WIN_PIECE_ORDER = (Q // 2, V // 2, CC // 2, CX // 2)
EARLY_PIECES = 2 * (D_MODEL // STAGE_ROWS)
PAR_G1, PAR_GF, PAR_RETG, PAR_CONVW, PAR_ROWS = 0, 1, 2, 3, 8
N_PAR_COPIES = 4
FINISH_GROUPS = 4
V7X_LANES = 128
V7X_F32_SUBLANES = 8
V7X_BF16_SUBLANES = 16
V7X_VMEM_BYTES = 64 * 1024 * 1024
VMEM_UNCLAIMED_BYTES = 2 * 1024 * 1024
VMEM_LIMIT_BYTES = V7X_VMEM_BYTES - VMEM_UNCLAIMED_BYTES


def _gammas():
    return 1.0 - 2.0 ** (-5.0 - np.arange(RET_HEADS, dtype=np.float64))


def _rope_tables(n_pos):
    freqs = 1.0 / (ROPE_BASE ** (np.arange(HALF, dtype=np.float64) / HALF))
    ang = np.arange(n_pos, dtype=np.float64)[:, None] * freqs[None, :]
    cos = np.concatenate([np.cos(ang), np.cos(ang)], axis=1)
    sin = np.concatenate([-np.sin(ang), np.sin(ang)], axis=1)
    return cos.astype(np.float32), sin.astype(np.float32)


def _decay_tables(chunk):
    g = _gammas()
    idx = np.arange(chunk, dtype=np.float64)
    diff = idx[:, None] - idx[None, :]
    scale = HEAD_DIM ** -0.5
    decay = np.where(diff[None] >= 0, g[:, None, None] ** np.maximum(diff[None], 0.0), 0.0) * scale
    xi = (g[:, None] ** (idx[None, :] + 1.0)) * scale
    zeta = g[:, None] ** (chunk - 1.0 - idx[None, :])
    xi = np.broadcast_to(xi[:, :, None], (RET_HEADS, chunk, HEAD_DIM))
    zeta = np.broadcast_to(zeta[:, :, None], (RET_HEADS, chunk, HEAD_DIM))
    return decay.astype(np.float32), xi.astype(np.float32), zeta.astype(np.float32)


def _meta_zeta():
    g = _gammas()
    j = np.arange(N_META, dtype=np.float64)
    z = g[:, None] ** (N_META - 1.0 - j[None, :])
    return np.broadcast_to(z[:, :, None], (RET_HEADS, N_META, HEAD_DIM)).astype(np.float32)


def _rms_norm(x, g):
    return x * lax.rsqrt(jnp.mean(x * x, axis=-1, keepdims=True) + EPS) * g


def _silu(x):
    return x * (1.0 / (1.0 + jnp.exp(-x)))


def _rotary(t, cos, sin):
    return t * cos + pltpu.roll(t, HALF, axis=1) * sin


def _project(hn_bf16, win_ref, sec):
    return jnp.dot(hn_bf16, win_ref[:, sec * SEC:(sec + 1) * SEC], preferred_element_type=jnp.float32)


def _head(t, h):
    return t[:, h * HEAD_DIM:(h + 1) * HEAD_DIM]


def _zero_tile_after(stored_f32):
    bits = pltpu.bitcast(stored_f32, jnp.uint32)
    rows, cols = bits.shape
    acc = bits[0:V7X_F32_SUBLANES]
    for r in range(V7X_F32_SUBLANES, rows, V7X_F32_SUBLANES):
        acc = acc | bits[r:r + V7X_F32_SUBLANES]
    word = acc[:, 0:V7X_LANES]
    for c in range(V7X_LANES, cols, V7X_LANES):
        word = word | acc[:, c:c + V7X_LANES]
    return pltpu.bitcast((word >> 16) >> 16, jnp.bfloat16)


def _after(lhs_bf16, zero_tile):
    r, c = V7X_BF16_SUBLANES, V7X_LANES
    top = jnp.concatenate([lhs_bf16[0:r, 0:c] + zero_tile, lhs_bf16[0:r, c:]], axis=1)
    return jnp.concatenate([top, lhs_bf16[r:]], axis=0)


def _weight_chunks(*weights):
    chunks = []
    for w_hbm, w_bf16_ref, piece_order in weights:
        n_rows, n_cols = w_hbm.shape
        assert sorted(piece_order) == list(range(n_cols // D_MODEL))
        for piece in piece_order:
            for r0 in range(0, n_rows, STAGE_ROWS):
                chunks.append((w_hbm, w_bf16_ref, r0, piece * D_MODEL))
    return chunks


def _stage_copy(chunks, idx, stage_slots, sem_ref):
    w_hbm, _, r0, c0 = chunks[idx]
    return pltpu.make_async_copy(w_hbm.at[pl.ds(r0, STAGE_ROWS), pl.ds(c0, D_MODEL)],
                                 stage_slots[idx % STAGE_SLOTS], sem_ref.at[idx % STAGE_SLOTS])


def _convert_piece(chunks, idx, stage_slots):
    _, w_bf16_ref, r0, c0 = chunks[idx]
    w_bf16_ref[r0:r0 + STAGE_ROWS, c0:c0 + D_MODEL] = stage_slots[idx % STAGE_SLOTS][...].astype(jnp.bfloat16)


def _fill_stage_slots(chunks, first, last, stage_slots, sem_ref):
    assert last <= STAGE_SLOTS
    for i in range(first, min(last, len(chunks))):
        _stage_copy(chunks, i, stage_slots, sem_ref).start()


def _load_weights_as_bf16(chunks, first, last, stage_slots, sem_ref):
    for i in range(first, last):
        _stage_copy(chunks, i, stage_slots, sem_ref).wait()
        _convert_piece(chunks, i, stage_slots)
        if i + STAGE_SLOTS < len(chunks):
            _stage_copy(chunks, i + STAGE_SLOTS, stage_slots, sem_ref).start()


def _mixer_kernel(x_hbm, meta_ref, g1_hbm, win_hbm, convw_hbm, retg_hbm, wout_hbm, gf_hbm,
                  cos_hbm, sin_hbm, mrope_ref, decay_ref, xi_ref, zeta_ref, mzeta_ref,
                  out_hbm,
                  win_ref, wout_ref, y_ref, stage_sem, par_ref, par_sem,
                  xbuf, cosbuf, sinbuf, obuf, in_sem, out_sem, tail_sem,
                  state_ref, state0_ref, u_ref, hist0_ref, q_ref, qx_ref, k_ref, kz_ref, v_ref,
                  p_ref, sbf_ref, mixed_ref,
                  *, tokens, chunk, chunk_decay, blocks_per_seq, n_blocks):
    bf16 = jnp.bfloat16
    f32 = jnp.float32
    row_pieces = range(0, tokens, STAGE_ROWS)
    stage_slots = ([y_ref.at[r:r + STAGE_ROWS, :] for r in row_pieces]
                   + [obuf.at[slot, r:r + STAGE_ROWS, :] for slot in range(2) for r in row_pieces])
    assert len(stage_slots) == STAGE_SLOTS
    g1_ref = par_ref.at[PAR_G1:PAR_G1 + 1, :]
    gf_ref = par_ref.at[PAR_GF:PAR_GF + 1, :]
    retg_ref = par_ref.at[PAR_RETG:PAR_RETG + 1, 0:D_RET]
    convw_ref = par_ref.at[PAR_CONVW:PAR_CONVW + CONV_WIDTH, 0:D_CONV]
    par_copies = [pltpu.make_async_copy(src, dst, par_sem.at[i]) for i, (src, dst) in enumerate(
        ((g1_hbm, g1_ref), (gf_hbm, gf_ref), (retg_hbm, retg_ref), (convw_hbm, convw_ref)))]

    def _block_rows(blk):
        return blk // blocks_per_seq, pl.ds((blk % blocks_per_seq) * tokens, tokens)

    def _in_copies(blk, slot):
        b, rows = _block_rows(blk)
        return (pltpu.make_async_copy(x_hbm.at[b, rows, :], xbuf.at[slot], in_sem.at[slot, 0]),
                pltpu.make_async_copy(cos_hbm.at[rows, :], cosbuf.at[slot], in_sem.at[slot, 1]),
                pltpu.make_async_copy(sin_hbm.at[rows, :], sinbuf.at[slot], in_sem.at[slot, 2]))

    def _out_copy(blk, slot):
        b, rows = _block_rows(blk)
        return pltpu.make_async_copy(obuf.at[slot], out_hbm.at[b, rows, :], out_sem.at[slot])

    def _finish_rows(out_ref, rows):
        out_ref[rows, :] = _rms_norm(y_ref[rows, :], gf_ref[...])

    def _mix_block(x_ref, cos_ref, sin_ref, out_ref, before_conv=None):
        finished = []
        if out_ref is not None:
            group = tokens // FINISH_GROUPS
            for g in range(FINISH_GROUPS):
                rows = slice(g * group, (g + 1) * group)
                _finish_rows(out_ref, rows)
                finished.append(_zero_tile_after(out_ref[rows, :]))

        def _after_finished(lhs, g):
            return _after(lhs, finished[g]) if finished else lhs

        x = x_ref[...]
        hn = _rms_norm(x, g1_ref[...]).astype(bf16)

        n_chunks = tokens // chunk

        cos = cos_ref[...]
        sin = sin_ref[...]
        q = _project(hn, win_ref, Q)
        k = _project(hn, win_ref, K)
        for h in range(RET_HEADS):
            cols = slice(h * HEAD_DIM, (h + 1) * HEAD_DIM)
            qr = _rotary(_head(q, h), cos, sin)
            kr = _rotary(_head(k, h), cos, sin)
            q_ref[:, cols] = qr.astype(bf16)
            k_ref[:, cols] = kr.astype(bf16)
            for c in range(n_chunks):
                rows = slice(c * chunk, (c + 1) * chunk)
                qx_ref[rows, cols] = (qr[rows] * xi_ref[h]).astype(bf16)
                kz_ref[rows, cols] = (kr[rows] * zeta_ref[h]).astype(bf16)
        v_ref[...] = _project(_after_finished(hn, 0), win_ref, V).astype(bf16)

        upd = {}
        for c in range(n_chunks):
            rows = slice(c * chunk, (c + 1) * chunk)
            for h in range(RET_HEADS):
                cols = slice(h * HEAD_DIM, (h + 1) * HEAD_DIM)
                scores = lax.dot_general(q_ref[rows, cols], k_ref[rows, cols], (((1,), (1,)), ((), ())),
                                         preferred_element_type=f32)
                p_ref[c * RET_HEADS + h] = (scores * decay_ref[h]).astype(bf16)
                upd[c, h] = lax.dot_general(kz_ref[rows, cols], v_ref[rows, cols], (((0,), (0,)), ((), ())),
                                            preferred_element_type=f32)
        if before_conv is not None:
            before_conv()
        for h in range(RET_HEADS):
            state = state_ref[h]
            for c in range(n_chunks):
                sbf_ref[c * RET_HEADS + h] = state.astype(bf16)
                state = chunk_decay[h] * state + upd[c, h]
            state_ref[h] = state

        u = _project(_after_finished(hn, 1), win_ref, CC) * _project(_after_finished(hn, 2), win_ref, CX)
        u_ref[HIST_ROWS:HIST_ROWS + tokens, :] = u
        u1 = u_ref[HIST_ROWS - 1:HIST_ROWS - 1 + tokens, :]
        u2 = u_ref[HIST_ROWS - 2:HIST_ROWS - 2 + tokens, :]
        conv = convw_ref[0:1, :] * u2 + convw_ref[1:2, :] * u1 + convw_ref[2:3, :] * u
        conv_out = _project(_after_finished(hn, 3), win_ref, CB) * conv * _silu(_project(hn, win_ref, CG))
        mixed_ref[:, 0:D_CONV] = conv_out.astype(bf16)
        u_ref[0:HIST_ROWS, :] = u_ref[tokens:tokens + HIST_ROWS, :]

        gate = _silu(_project(hn, win_ref, RG))
        for c in range(n_chunks):
            rows = slice(c * chunk, (c + 1) * chunk)
            for h in range(RET_HEADS):
                cols = slice(h * HEAD_DIM, (h + 1) * HEAD_DIM)
                hc = c * RET_HEADS + h
                lhs = jnp.concatenate([p_ref[hc], qx_ref[rows, cols]], axis=1)
                rhs = jnp.concatenate([v_ref[rows, cols], sbf_ref[hc]], axis=0)
                o = jnp.dot(lhs, rhs, preferred_element_type=f32)
                mu = jnp.mean(o, axis=-1, keepdims=True)
                d = o - mu
                var = jnp.mean(d * d, axis=-1, keepdims=True)
                y = d * lax.rsqrt(var + EPS) * retg_ref[:, cols]
                mixed_ref[rows, D_CONV + h * HEAD_DIM:D_CONV + (h + 1) * HEAD_DIM] = (
                    y * gate[rows, cols]).astype(bf16)

        y_ref[...] = jnp.dot(mixed_ref[...], wout_ref[...], preferred_element_type=f32) + x

    for copy in par_copies:
        copy.start()
    chunks = _weight_chunks((win_hbm, win_ref, WIN_PIECE_ORDER), (wout_hbm, wout_ref, (0,)))
    _fill_stage_slots(chunks, 0, EARLY_PIECES, stage_slots, stage_sem)
    for copy in _in_copies(0, 0):
        copy.start()
    _fill_stage_slots(chunks, EARLY_PIECES, STAGE_SLOTS, stage_slots, stage_sem)
    _load_weights_as_bf16(chunks, 0, EARLY_PIECES, stage_slots, stage_sem)
    for copy in par_copies:
        copy.wait()
    hm = _rms_norm(meta_ref[...], g1_ref[...]).astype(bf16)

    def _rest_of_weights():
        assert len(chunks) - EARLY_PIECES <= STAGE_SLOTS
        for i in range(EARLY_PIECES, len(chunks)):
            _stage_copy(chunks, i, stage_slots, stage_sem).wait()
        if n_blocks > 1:
            for copy in _in_copies(1, 1):
                copy.start()
        k_m = _project(hm, win_ref, K)
        v_m = _project(hm, win_ref, V).astype(bf16)
        for h in range(RET_HEADS):
            kr = _rotary(_head(k_m, h), mrope_ref[0:N_META, :], mrope_ref[N_META:, :]) * mzeta_ref[h]
            state0_ref[h] = lax.dot_general(kr.astype(bf16), _head(v_m, h), (((0,), (0,)), ((), ())),
                                            preferred_element_type=f32)
        state_ref[...] = state0_ref[...]
        for i in range(EARLY_PIECES, len(chunks)):
            _convert_piece(chunks, i, stage_slots)
        u_m = _project(hm, win_ref, CC) * _project(hm, win_ref, CX)
        hist0_ref[...] = u_m[N_META - HIST_ROWS:, :]
        u_ref[0:HIST_ROWS, :] = hist0_ref[...]

    for copy in _in_copies(0, 0):
        copy.wait()
    _mix_block(xbuf.at[0], cosbuf.at[0], sinbuf.at[0], None, before_conv=_rest_of_weights)

    def _step(s, carry):
        slot = s % 2
        prev_slot = 1 - slot
        for copy in _in_copies(s, slot):
            copy.wait()

        @pl.when(s + 1 < n_blocks)
        def _fetch_next():
            for copy in _in_copies(s + 1, prev_slot):
                copy.start()

        @pl.when(s >= 3)
        def _free_out_slot():
            _out_copy(s - 3, prev_slot).wait()

        @pl.when(s % blocks_per_seq == 0)
        def _start_sequence():
            state_ref[...] = state0_ref[...]
            u_ref[0:HIST_ROWS, :] = hist0_ref[...]

        _mix_block(xbuf.at[slot], cosbuf.at[slot], sinbuf.at[slot], obuf.at[prev_slot])
        _out_copy(s - 1, prev_slot).start()
        return carry

    lax.fori_loop(1, n_blocks, _step, 0)

    last = n_blocks - 1
    last_slot = last % 2
    if n_blocks >= 3:
        _out_copy(last - 2, last_slot).wait()
    group = tokens // FINISH_GROUPS
    b_last, rows_last = last // blocks_per_seq, (last % blocks_per_seq) * tokens
    tail_copies = []
    for g in range(FINISH_GROUPS):
        rows = slice(g * group, (g + 1) * group)
        _finish_rows(obuf.at[last_slot], rows)
        tail_copies.append(pltpu.make_async_copy(
            obuf.at[last_slot, rows, :], out_hbm.at[b_last, pl.ds(rows_last + g * group, group), :], tail_sem.at[g]))
        tail_copies[-1].start()
    if n_blocks >= 2:
        _out_copy(last - 1, 1 - last_slot).wait()
    for copy in tail_copies:
        copy.wait()


def kernel(x, meta, norm1_g, w_in, conv_w, ret_norm_g, w_out, final_g):
    bsz, seq, d_model = x.shape
    assert d_model == D_MODEL and meta.shape == (N_META, D_MODEL)
    tokens, chunk = TOKENS_PER_STEP, RET_CHUNK
    assert seq % tokens == 0 and tokens % chunk == 0 and 3 * tokens == STAGE_SLOTS * STAGE_ROWS

    cos, sin = _rope_tables(N_META + seq)
    decay, xi, zeta = _decay_tables(chunk)
    chunk_decay = tuple(float(v) for v in _gammas() ** chunk)
    n_hc = (tokens // chunk) * RET_HEADS

    blocks_per_seq = seq // tokens
    n_blocks = bsz * blocks_per_seq

    in_vmem = pl.BlockSpec(memory_space=pltpu.VMEM)
    in_hbm = pl.BlockSpec(memory_space=pl.ANY)
    body = functools.partial(_mixer_kernel, tokens=tokens, chunk=chunk, chunk_decay=chunk_decay,
                             blocks_per_seq=blocks_per_seq, n_blocks=n_blocks)
    return pl.pallas_call(
        body,
        in_specs=[
            in_hbm,
            in_vmem,
            in_hbm,
            in_hbm,
            in_hbm,
            in_hbm,
            in_hbm,
            in_hbm,
            in_hbm,
            in_hbm,
            in_vmem,
            in_vmem,
            in_vmem,
            in_vmem,
            in_vmem,
        ],
        out_specs=pl.BlockSpec(memory_space=pl.ANY),
        out_shape=jax.ShapeDtypeStruct((bsz, seq, D_MODEL), x.dtype),
        scratch_shapes=[
            pltpu.VMEM((D_MODEL, 8 * SEC), jnp.bfloat16),
            pltpu.VMEM((D_MODEL, D_MODEL), jnp.bfloat16),
            pltpu.VMEM((tokens, D_MODEL), jnp.float32),
            pltpu.SemaphoreType.DMA((STAGE_SLOTS,)),
            pltpu.VMEM((PAR_ROWS, D_MODEL), jnp.float32),
            pltpu.SemaphoreType.DMA((N_PAR_COPIES,)),
            pltpu.VMEM((2, tokens, D_MODEL), jnp.float32),
            pltpu.VMEM((2, tokens, HEAD_DIM), jnp.float32),
            pltpu.VMEM((2, tokens, HEAD_DIM), jnp.float32),
            pltpu.VMEM((2, tokens, D_MODEL), jnp.float32),
            pltpu.SemaphoreType.DMA((2, 3)),
            pltpu.SemaphoreType.DMA((2,)),
            pltpu.SemaphoreType.DMA((FINISH_GROUPS,)),
            pltpu.VMEM((RET_HEADS, HEAD_DIM, HEAD_DIM), jnp.float32),
            pltpu.VMEM((RET_HEADS, HEAD_DIM, HEAD_DIM), jnp.float32),
            pltpu.VMEM((HIST_ROWS + tokens, D_CONV), jnp.float32),
            pltpu.VMEM((HIST_ROWS, D_CONV), jnp.float32),
            pltpu.VMEM((tokens, D_RET), jnp.bfloat16),
            pltpu.VMEM((tokens, D_RET), jnp.bfloat16),
            pltpu.VMEM((tokens, D_RET), jnp.bfloat16),
            pltpu.VMEM((tokens, D_RET), jnp.bfloat16),
            pltpu.VMEM((tokens, D_RET), jnp.bfloat16),
            pltpu.VMEM((n_hc, chunk, chunk), jnp.bfloat16),
            pltpu.VMEM((n_hc, HEAD_DIM, HEAD_DIM), jnp.bfloat16),
            pltpu.VMEM((tokens, D_MODEL), jnp.bfloat16),
        ],
        compiler_params=pltpu.CompilerParams(
            vmem_limit_bytes=VMEM_LIMIT_BYTES),
        name="hymba_mixer",
    )(x, meta, norm1_g.reshape(1, -1), w_in, conv_w,
      ret_norm_g.reshape(1, -1), w_out, final_g.reshape(1, -1),
      jnp.asarray(cos[N_META:]), jnp.asarray(sin[N_META:]),
      jnp.asarray(np.concatenate([cos[:N_META], sin[:N_META]], axis=0)),
      jnp.asarray(decay), jnp.asarray(xi), jnp.asarray(zeta), jnp.asarray(_meta_zeta()))
```

```python
import functools

import numpy as np
import jax
import jax.numpy as jnp
from jax import lax
from jax.experimental import pallas as pl
from jax.experimental.pallas import tpu as pltpu

D_MODEL = 1024
N_META = 16
D_CONV = 512
D_RET = 512
RET_HEADS = 4
HEAD_DIM = 128
HALF = HEAD_DIM // 2
CONV_WIDTH = 3
ROPE_BASE = 10000.0
EPS = 1e-6

SEC = 512
CX, CB, CC, CG, Q, K, V, RG = range(8)

TOKENS_PER_STEP = 1024
RET_CHUNK = 128
HIST_ROWS = 8
STAGE_ROWS = 256
STAGE_SLOTS = 12
WIN_PIECE_ORDER = (Q // 2, V // 2, CC // 2, CX // 2)
EARLY_PIECES = 2 * (D_MODEL // STAGE_ROWS)
FIRST_BLOCK_READS = TOKENS_PER_STEP // STAGE_ROWS
PAR_G1, PAR_GF, PAR_RETG, PAR_CONVW, PAR_ROWS = 0, 1, 2, 3, 8
N_PAR_COPIES = 4
FINISH_GROUPS = 4
V7X_LANES = 128
V7X_F32_SUBLANES = 8
V7X_BF16_SUBLANES = 16
V7X_VMEM_BYTES = 64 * 1024 * 1024
VMEM_UNCLAIMED_BYTES = 2 * 1024 * 1024
VMEM_LIMIT_BYTES = V7X_VMEM_BYTES - VMEM_UNCLAIMED_BYTES


def _gammas():
    return 1.0 - 2.0 ** (-5.0 - np.arange(RET_HEADS, dtype=np.float64))


def _rope_tables(n_pos):
    freqs = 1.0 / (ROPE_BASE ** (np.arange(HALF, dtype=np.float64) / HALF))
    ang = np.arange(n_pos, dtype=np.float64)[:, None] * freqs[None, :]
    cos = np.concatenate([np.cos(ang), np.cos(ang)], axis=1)
    sin = np.concatenate([-np.sin(ang), np.sin(ang)], axis=1)
    return cos.astype(np.float32), sin.astype(np.float32)


def _decay_tables(chunk):
    g = _gammas()
    idx = np.arange(chunk, dtype=np.float64)
    diff = idx[:, None] - idx[None, :]
    scale = HEAD_DIM ** -0.5
    decay = np.where(diff[None] >= 0, g[:, None, None] ** np.maximum(diff[None], 0.0), 0.0) * scale
    xi = (g[:, None] ** (idx[None, :] + 1.0)) * scale
    zeta = g[:, None] ** (chunk - 1.0 - idx[None, :])
    xi = np.broadcast_to(xi[:, :, None], (RET_HEADS, chunk, HEAD_DIM))
    zeta = np.broadcast_to(zeta[:, :, None], (RET_HEADS, chunk, HEAD_DIM))
    return decay.astype(np.float32), xi.astype(np.float32), zeta.astype(np.float32)


def _meta_zeta():
    g = _gammas()
    j = np.arange(N_META, dtype=np.float64)
    z = g[:, None] ** (N_META - 1.0 - j[None, :])
    return np.broadcast_to(z[:, :, None], (RET_HEADS, N_META, HEAD_DIM)).astype(np.float32)


def _rms_norm(x, g):
    return x * lax.rsqrt(jnp.mean(x * x, axis=-1, keepdims=True) + EPS) * g


def _silu(x):
    return x * (1.0 / (1.0 + jnp.exp(-x)))


def _rotary(t, cos, sin):
    return t * cos + pltpu.roll(t, HALF, axis=1) * sin


def _project(hn_bf16, win_ref, sec):
    return jnp.dot(hn_bf16, win_ref[:, sec * SEC:(sec + 1) * SEC], preferred_element_type=jnp.float32)


def _head(t, h):
    return t[:, h * HEAD_DIM:(h + 1) * HEAD_DIM]


def _zero_tile_after(stored_f32):
    bits = pltpu.bitcast(stored_f32, jnp.uint32)
    rows, cols = bits.shape
    acc = bits[0:V7X_F32_SUBLANES]
    for r in range(V7X_F32_SUBLANES, rows, V7X_F32_SUBLANES):
        acc = acc | bits[r:r + V7X_F32_SUBLANES]
    word = acc[:, 0:V7X_LANES]
    for c in range(V7X_LANES, cols, V7X_LANES):
        word = word | acc[:, c:c + V7X_LANES]
    return pltpu.bitcast((word >> 16) >> 16, jnp.bfloat16)


def _after(lhs_bf16, zero_tile):
    r, c = V7X_BF16_SUBLANES, V7X_LANES
    top = jnp.concatenate([lhs_bf16[0:r, 0:c] + zero_tile, lhs_bf16[0:r, c:]], axis=1)
    return jnp.concatenate([top, lhs_bf16[r:]], axis=0)


def _weight_chunks(*weights):
    chunks = []
    for w_hbm, w_bf16_ref, piece_order in weights:
        n_rows, n_cols = w_hbm.shape
        assert sorted(piece_order) == list(range(n_cols // D_MODEL))
        for piece in piece_order:
            for r0 in range(0, n_rows, STAGE_ROWS):
                chunks.append((w_hbm, w_bf16_ref, r0, piece * D_MODEL))
    return chunks


def _stage_copy(chunks, idx, stage_slots, sem_ref):
    w_hbm, _, r0, c0 = chunks[idx]
    return pltpu.make_async_copy(w_hbm.at[pl.ds(r0, STAGE_ROWS), pl.ds(c0, D_MODEL)],
                                 stage_slots[idx % STAGE_SLOTS], sem_ref.at[idx % STAGE_SLOTS])


def _convert_piece(chunks, idx, stage_slots):
    _, w_bf16_ref, r0, c0 = chunks[idx]
    w_bf16_ref[r0:r0 + STAGE_ROWS, c0:c0 + D_MODEL] = stage_slots[idx % STAGE_SLOTS][...].astype(jnp.bfloat16)


def _start_pieces(chunks, first, last, stage_slots, sem_ref):
    for i in range(first, min(last, len(chunks))):
        _stage_copy(chunks, i, stage_slots, sem_ref).start()


def _load_weights_as_bf16(chunks, first, last, stage_slots, sem_ref, ahead):
    assert ahead <= STAGE_SLOTS
    for i in range(first, last):
        _stage_copy(chunks, i, stage_slots, sem_ref).wait()
        _convert_piece(chunks, i, stage_slots)
        _start_pieces(chunks, i + ahead, i + ahead + 1, stage_slots, sem_ref)


def _mixer_kernel(x_hbm, meta_ref, g1_hbm, win_hbm, convw_hbm, retg_hbm, wout_hbm, gf_hbm,
                  cos_hbm, sin_hbm, mrope_ref, decay_ref, xi_ref, zeta_ref, mzeta_ref,
                  out_hbm,
                  win_ref, wout_ref, y_ref, stage_sem, par_ref, par_sem,
                  xbuf, cosbuf, sinbuf, obuf, in_sem, first_sem, out_sem, tail_sem,
                  state_ref, state0_ref, u_ref, hist0_ref, q_ref, qx_ref, k_ref, kz_ref, v_ref,
                  p_ref, sbf_ref, mixed_ref,
                  *, tokens, chunk, chunk_decay, blocks_per_seq, n_blocks):
    bf16 = jnp.bfloat16
    f32 = jnp.float32
    row_pieces = range(0, tokens, STAGE_ROWS)
    stage_slots = ([y_ref.at[r:r + STAGE_ROWS, :] for r in row_pieces]
                   + [obuf.at[slot, r:r + STAGE_ROWS, :] for slot in range(2) for r in row_pieces])
    assert len(stage_slots) == STAGE_SLOTS
    g1_ref = par_ref.at[PAR_G1:PAR_G1 + 1, :]
    gf_ref = par_ref.at[PAR_GF:PAR_GF + 1, :]
    retg_ref = par_ref.at[PAR_RETG:PAR_RETG + 1, 0:D_RET]
    convw_ref = par_ref.at[PAR_CONVW:PAR_CONVW + CONV_WIDTH, 0:D_CONV]
    par_copies = [pltpu.make_async_copy(src, dst, par_sem.at[i]) for i, (src, dst) in enumerate(
        ((g1_hbm, g1_ref), (gf_hbm, gf_ref), (retg_hbm, retg_ref), (convw_hbm, convw_ref)))]

    def _block_rows(blk):
        return blk // blocks_per_seq, pl.ds((blk % blocks_per_seq) * tokens, tokens)

    def _in_copies(blk, slot):
        b, rows = _block_rows(blk)
        return (pltpu.make_async_copy(x_hbm.at[b, rows, :], xbuf.at[slot], in_sem.at[slot, 0]),
                pltpu.make_async_copy(cos_hbm.at[rows, :], cosbuf.at[slot], in_sem.at[slot, 1]),
                pltpu.make_async_copy(sin_hbm.at[rows, :], sinbuf.at[slot], in_sem.at[slot, 2]))

    def _out_copy(blk, slot):
        b, rows = _block_rows(blk)
        return pltpu.make_async_copy(obuf.at[slot], out_hbm.at[b, rows, :], out_sem.at[slot])

    def _finish_rows(out_ref, rows):
        out_ref[rows, :] = _rms_norm(y_ref[rows, :], gf_ref[...])

    def _mix_block(x_ref, cos_ref, sin_ref, out_ref, before_conv=None):
        finished = []
        if out_ref is not None:
            group = tokens // FINISH_GROUPS
            for g in range(FINISH_GROUPS):
                rows = slice(g * group, (g + 1) * group)
                _finish_rows(out_ref, rows)
                finished.append(_zero_tile_after(out_ref[rows, :]))

        def _after_finished(lhs, g):
            return _after(lhs, finished[g]) if finished else lhs

        x = x_ref[...]
        hn = _rms_norm(x, g1_ref[...]).astype(bf16)

        n_chunks = tokens // chunk

        cos = cos_ref[...]
        sin = sin_ref[...]
        q = _project(hn, win_ref, Q)
        k = _project(hn, win_ref, K)
        for h in range(RET_HEADS):
            cols = slice(h * HEAD_DIM, (h + 1) * HEAD_DIM)
            qr = _rotary(_head(q, h), cos, sin)
            kr = _rotary(_head(k, h), cos, sin)
            q_ref[:, cols] = qr.astype(bf16)
            k_ref[:, cols] = kr.astype(bf16)
            for c in range(n_chunks):
                rows = slice(c * chunk, (c + 1) * chunk)
                qx_ref[rows, cols] = (qr[rows] * xi_ref[h]).astype(bf16)
                kz_ref[rows, cols] = (kr[rows] * zeta_ref[h]).astype(bf16)
        v_ref[...] = _project(_after_finished(hn, 0), win_ref, V).astype(bf16)

        upd = {}
        for c in range(n_chunks):
            rows = slice(c * chunk, (c + 1) * chunk)
            for h in range(RET_HEADS):
                cols = slice(h * HEAD_DIM, (h + 1) * HEAD_DIM)
                scores = lax.dot_general(q_ref[rows, cols], k_ref[rows, cols], (((1,), (1,)), ((), ())),
                                         preferred_element_type=f32)
                p_ref[c * RET_HEADS + h] = (scores * decay_ref[h]).astype(bf16)
                upd[c, h] = lax.dot_general(kz_ref[rows, cols], v_ref[rows, cols], (((0,), (0,)), ((), ())),
                                            preferred_element_type=f32)
        for h in range(RET_HEADS):
            state = state_ref[h]
            for c in range(n_chunks):
                sbf_ref[c * RET_HEADS + h] = state.astype(bf16)
                state = chunk_decay[h] * state + upd[c, h]
            state_ref[h] = state

        if before_conv is not None:
            before_conv()

        u = _project(_after_finished(hn, 1), win_ref, CC) * _project(_after_finished(hn, 2), win_ref, CX)
        u_ref[HIST_ROWS:HIST_ROWS + tokens, :] = u
        u1 = u_ref[HIST_ROWS - 1:HIST_ROWS - 1 + tokens, :]
        u2 = u_ref[HIST_ROWS - 2:HIST_ROWS - 2 + tokens, :]
        conv = convw_ref[0:1, :] * u2 + convw_ref[1:2, :] * u1 + convw_ref[2:3, :] * u
        conv_out = _project(_after_finished(hn, 3), win_ref, CB) * conv * _silu(_project(hn, win_ref, CG))
        mixed_ref[:, 0:D_CONV] = conv_out.astype(bf16)
        u_ref[0:HIST_ROWS, :] = u_ref[tokens:tokens + HIST_ROWS, :]

        gate = _silu(_project(hn, win_ref, RG))
        for c in range(n_chunks):
            rows = slice(c * chunk, (c + 1) * chunk)
            for h in range(RET_HEADS):
                cols = slice(h * HEAD_DIM, (h + 1) * HEAD_DIM)
                hc = c * RET_HEADS + h
                lhs = jnp.concatenate([p_ref[hc], qx_ref[rows, cols]], axis=1)
                rhs = jnp.concatenate([v_ref[rows, cols], sbf_ref[hc]], axis=0)
                o = jnp.dot(lhs, rhs, preferred_element_type=f32)
                mu = jnp.mean(o, axis=-1, keepdims=True)
                d = o - mu
                var = jnp.mean(d * d, axis=-1, keepdims=True)
                y = d * lax.rsqrt(var + EPS) * retg_ref[:, cols]
                mixed_ref[rows, D_CONV + h * HEAD_DIM:D_CONV + (h + 1) * HEAD_DIM] = (
                    y * gate[rows, cols]).astype(bf16)

        y_ref[...] = jnp.dot(mixed_ref[...], wout_ref[...], preferred_element_type=f32) + x

    for copy in par_copies:
        copy.start()
    chunks = _weight_chunks((win_hbm, win_ref, WIN_PIECE_ORDER), (wout_hbm, wout_ref, (0,)))
    _start_pieces(chunks, 0, EARLY_PIECES, stage_slots, stage_sem)
    first_rows = tokens // FIRST_BLOCK_READS
    first_block_copies = [
        pltpu.make_async_copy(x_hbm.at[0, pl.ds(i * first_rows, first_rows), :],
                              xbuf.at[0, pl.ds(i * first_rows, first_rows), :], first_sem.at[i])
        for i in range(FIRST_BLOCK_READS)] + list(_in_copies(0, 0)[1:])
    for copy in first_block_copies:
        copy.start()
    _load_weights_as_bf16(chunks, 0, EARLY_PIECES, stage_slots, stage_sem, ahead=EARLY_PIECES)
    _start_pieces(chunks, 2 * EARLY_PIECES, len(chunks), stage_slots, stage_sem)
    for copy in par_copies:
        copy.wait()
    hm = _rms_norm(meta_ref[...], g1_ref[...]).astype(bf16)
    k_m = _project(hm, win_ref, K)
    v_m = _project(hm, win_ref, V).astype(bf16)
    for h in range(RET_HEADS):
        kr = _rotary(_head(k_m, h), mrope_ref[0:N_META, :], mrope_ref[N_META:, :]) * mzeta_ref[h]
        state0_ref[h] = lax.dot_general(kr.astype(bf16), _head(v_m, h), (((0,), (0,)), ((), ())),
                                        preferred_element_type=f32)

    def _rest_of_weights():
        assert len(chunks) - EARLY_PIECES <= STAGE_SLOTS
        for i in range(EARLY_PIECES, len(chunks)):
            _stage_copy(chunks, i, stage_slots, stage_sem).wait()
        if n_blocks > 1:
            for copy in _in_copies(1, 1):
                copy.start()
        for i in range(EARLY_PIECES, len(chunks)):
            _convert_piece(chunks, i, stage_slots)
        u_m = _project(hm, win_ref, CC) * _project(hm, win_ref, CX)
        hist0_ref[...] = u_m[N_META - HIST_ROWS:, :]
        u_ref[0:HIST_ROWS, :] = hist0_ref[...]

    for copy in first_block_copies:
        copy.wait()
    state_ref[...] = state0_ref[...]
    _mix_block(xbuf.at[0], cosbuf.at[0], sinbuf.at[0], None, before_conv=_rest_of_weights)

    def _step(s, carry):
        slot = s % 2
        prev_slot = 1 - slot
        for copy in _in_copies(s, slot):
            copy.wait()

        @pl.when(s + 1 < n_blocks)
        def _fetch_next():
            for copy in _in_copies(s + 1, prev_slot):
                copy.start()

        @pl.when(s >= 3)
        def _free_out_slot():
            _out_copy(s - 3, prev_slot).wait()

        @pl.when(s % blocks_per_seq == 0)
        def _start_sequence():
            state_ref[...] = state0_ref[...]
            u_ref[0:HIST_ROWS, :] = hist0_ref[...]

        _mix_block(xbuf.at[slot], cosbuf.at[slot], sinbuf.at[slot], obuf.at[prev_slot])
        _out_copy(s - 1, prev_slot).start()
        return carry

    lax.fori_loop(1, n_blocks, _step, 0)

    last = n_blocks - 1
    last_slot = last % 2
    if n_blocks >= 3:
        _out_copy(last - 2, last_slot).wait()
    group = tokens // FINISH_GROUPS
    b_last, rows_last = last // blocks_per_seq, (last % blocks_per_seq) * tokens
    tail_copies = []
    for g in range(FINISH_GROUPS):
        rows = slice(g * group, (g + 1) * group)
        _finish_rows(obuf.at[last_slot], rows)
        tail_copies.append(pltpu.make_async_copy(
            obuf.at[last_slot, rows, :], out_hbm.at[b_last, pl.ds(rows_last + g * group, group), :], tail_sem.at[g]))
        tail_copies[-1].start()
    if n_blocks >= 2:
        _out_copy(last - 1, 1 - last_slot).wait()
    for copy in tail_copies:
        copy.wait()


def kernel(x, meta, norm1_g, w_in, conv_w, ret_norm_g, w_out, final_g):
    bsz, seq, d_model = x.shape
    assert d_model == D_MODEL and meta.shape == (N_META, D_MODEL)
    tokens, chunk = TOKENS_PER_STEP, RET_CHUNK
    assert seq % tokens == 0 and tokens % chunk == 0 and 3 * tokens == STAGE_SLOTS * STAGE_ROWS

    cos, sin = _rope_tables(N_META + seq)
    decay, xi, zeta = _decay_tables(chunk)
    chunk_decay = tuple(float(v) for v in _gammas() ** chunk)
    n_hc = (tokens // chunk) * RET_HEADS

    blocks_per_seq = seq // tokens
    n_blocks = bsz * blocks_per_seq

    in_vmem = pl.BlockSpec(memory_space=pltpu.VMEM)
    in_hbm = pl.BlockSpec(memory_space=pl.ANY)
    body = functools.partial(_mixer_kernel, tokens=tokens, chunk=chunk, chunk_decay=chunk_decay,
                             blocks_per_seq=blocks_per_seq, n_blocks=n_blocks)
    return pl.pallas_call(
        body,
        in_specs=[
            in_hbm,
            in_vmem,
            in_hbm,
            in_hbm,
            in_hbm,
            in_hbm,
            in_hbm,
            in_hbm,
            in_hbm,
            in_hbm,
            in_vmem,
            in_vmem,
            in_vmem,
            in_vmem,
            in_vmem,
        ],
        out_specs=pl.BlockSpec(memory_space=pl.ANY),
        out_shape=jax.ShapeDtypeStruct((bsz, seq, D_MODEL), x.dtype),
        scratch_shapes=[
            pltpu.VMEM((D_MODEL, 8 * SEC), jnp.bfloat16),
            pltpu.VMEM((D_MODEL, D_MODEL), jnp.bfloat16),
            pltpu.VMEM((tokens, D_MODEL), jnp.float32),
            pltpu.SemaphoreType.DMA((STAGE_SLOTS,)),
            pltpu.VMEM((PAR_ROWS, D_MODEL), jnp.float32),
            pltpu.SemaphoreType.DMA((N_PAR_COPIES,)),
            pltpu.VMEM((2, tokens, D_MODEL), jnp.float32),
            pltpu.VMEM((2, tokens, HEAD_DIM), jnp.float32),
            pltpu.VMEM((2, tokens, HEAD_DIM), jnp.float32),
            pltpu.VMEM((2, tokens, D_MODEL), jnp.float32),
            pltpu.SemaphoreType.DMA((2, 3)),
            pltpu.SemaphoreType.DMA((FIRST_BLOCK_READS,)),
            pltpu.SemaphoreType.DMA((2,)),
            pltpu.SemaphoreType.DMA((FINISH_GROUPS,)),
            pltpu.VMEM((RET_HEADS, HEAD_DIM, HEAD_DIM), jnp.float32),
            pltpu.VMEM((RET_HEADS, HEAD_DIM, HEAD_DIM), jnp.float32),
            pltpu.VMEM((HIST_ROWS + tokens, D_CONV), jnp.float32),
            pltpu.VMEM((HIST_ROWS, D_CONV), jnp.float32),
            pltpu.VMEM((tokens, D_RET), jnp.bfloat16),
            pltpu.VMEM((tokens, D_RET), jnp.bfloat16),
            pltpu.VMEM((tokens, D_RET), jnp.bfloat16),
            pltpu.VMEM((tokens, D_RET), jnp.bfloat16),
            pltpu.VMEM((tokens, D_RET), jnp.bfloat16),
            pltpu.VMEM((n_hc, chunk, chunk), jnp.bfloat16),
            pltpu.VMEM((n_hc, HEAD_DIM, HEAD_DIM), jnp.bfloat16),
            pltpu.VMEM((tokens, D_MODEL), jnp.bfloat16),
        ],
        compiler_params=pltpu.CompilerParams(
            vmem_limit_bytes=VMEM_LIMIT_BYTES),
        name="hymba_mixer",
    )(x, meta, norm1_g.reshape(1, -1), w_in, conv_w,
      ret_norm_g.reshape(1, -1), w_out, final_g.reshape(1, -1),
      jnp.asarray(cos[N_META:]), jnp.asarray(sin[N_META:]),
      jnp.asarray(np.concatenate([cos[:N_META], sin[:N_META]], axis=0)),
      jnp.asarray(decay), jnp.asarray(xi), jnp.asarray(zeta), jnp.asarray(_meta_zeta()))
```

```python
import functools

import numpy as np
import jax
import jax.numpy as jnp
from jax import lax
from jax.experimental import pallas as pl
from jax.experimental.pallas import tpu as pltpu

D_MODEL = 1024
N_META = 16
D_CONV = 512
D_RET = 512
RET_HEADS = 4
HEAD_DIM = 128
HALF = HEAD_DIM // 2
CONV_WIDTH = 3
ROPE_BASE = 10000.0
EPS = 1e-6

SEC = 512
CX, CB, CC, CG, Q, K, V, RG = range(8)

TOKENS_PER_STEP = 1024
RET_CHUNK = 128
HIST_ROWS = 8
STAGE_ROWS = 256
STAGE_SLOTS = 12
WIN_PIECE_ORDER = (Q // 2, V // 2, CC // 2, CX // 2)
EARLY_PIECES = 2 * (D_MODEL // STAGE_ROWS)
FIRST_BLOCK_READS = TOKENS_PER_STEP // STAGE_ROWS
PAR_G1, PAR_GF, PAR_RETG, PAR_CONVW, PAR_ROWS = 0, 1, 2, 3, 8
N_PAR_COPIES = 4
FINISH_GROUPS = 4
V7X_LANES = 128
V7X_F32_SUBLANES = 8
V7X_BF16_SUBLANES = 16
V7X_VMEM_BYTES = 64 * 1024 * 1024
VMEM_UNCLAIMED_BYTES = 2 * 1024 * 1024
VMEM_LIMIT_BYTES = V7X_VMEM_BYTES - VMEM_UNCLAIMED_BYTES


def _gammas():
    return 1.0 - 2.0 ** (-5.0 - np.arange(RET_HEADS, dtype=np.float64))


def _rope_tables(n_pos):
    freqs = 1.0 / (ROPE_BASE ** (np.arange(HALF, dtype=np.float64) / HALF))
    ang = np.arange(n_pos, dtype=np.float64)[:, None] * freqs[None, :]
    cos = np.concatenate([np.cos(ang), np.cos(ang)], axis=1)
    sin = np.concatenate([-np.sin(ang), np.sin(ang)], axis=1)
    return cos.astype(np.float32), sin.astype(np.float32)


def _decay_tables(chunk):
    g = _gammas()
    idx = np.arange(chunk, dtype=np.float64)
    diff = idx[:, None] - idx[None, :]
    scale = HEAD_DIM ** -0.5
    decay = np.where(diff[None] >= 0, g[:, None, None] ** np.maximum(diff[None], 0.0), 0.0) * scale
    xi = (g[:, None] ** (idx[None, :] + 1.0)) * scale
    zeta = g[:, None] ** (chunk - 1.0 - idx[None, :])
    xi = np.broadcast_to(xi[:, :, None], (RET_HEADS, chunk, HEAD_DIM))
    zeta = np.broadcast_to(zeta[:, :, None], (RET_HEADS, chunk, HEAD_DIM))
    return decay.astype(np.float32), xi.astype(np.float32), zeta.astype(np.float32)


def _meta_zeta():
    g = _gammas()
    j = np.arange(N_META, dtype=np.float64)
    z = g[:, None] ** (N_META - 1.0 - j[None, :])
    return np.broadcast_to(z[:, :, None], (RET_HEADS, N_META, HEAD_DIM)).astype(np.float32)


def _rms_norm(x, g):
    return x * lax.rsqrt(jnp.mean(x * x, axis=-1, keepdims=True) + EPS) * g


def _silu(x):
    return x * (1.0 / (1.0 + jnp.exp(-x)))


def _rotary(t, cos, sin):
    return t * cos + pltpu.roll(t, HALF, axis=1) * sin


def _project(hn_bf16, win_ref, sec):
    return jnp.dot(hn_bf16, win_ref[:, sec * SEC:(sec + 1) * SEC], preferred_element_type=jnp.float32)


def _head(t, h):
    return t[:, h * HEAD_DIM:(h + 1) * HEAD_DIM]


def _zero_tile_after(stored_f32):
    bits = pltpu.bitcast(stored_f32, jnp.uint32)
    rows, cols = bits.shape
    acc = bits[0:V7X_F32_SUBLANES]
    for r in range(V7X_F32_SUBLANES, rows, V7X_F32_SUBLANES):
        acc = acc | bits[r:r + V7X_F32_SUBLANES]
    word = acc[:, 0:V7X_LANES]
    for c in range(V7X_LANES, cols, V7X_LANES):
        word = word | acc[:, c:c + V7X_LANES]
    return pltpu.bitcast((word >> 16) >> 16, jnp.bfloat16)


def _after(lhs_bf16, zero_tile):
    r, c = V7X_BF16_SUBLANES, V7X_LANES
    top = jnp.concatenate([lhs_bf16[0:r, 0:c] + zero_tile, lhs_bf16[0:r, c:]], axis=1)
    return jnp.concatenate([top, lhs_bf16[r:]], axis=0)


def _weight_chunks(*weights):
    chunks = []
    for w_hbm, w_bf16_ref, piece_order in weights:
        n_rows, n_cols = w_hbm.shape
        assert sorted(piece_order) == list(range(n_cols // D_MODEL))
        for piece in piece_order:
            for r0 in range(0, n_rows, STAGE_ROWS):
                chunks.append((w_hbm, w_bf16_ref, r0, piece * D_MODEL))
    return chunks


def _stage_copy(chunks, idx, stage_slots, sem_ref):
    w_hbm, _, r0, c0 = chunks[idx]
    return pltpu.make_async_copy(w_hbm.at[pl.ds(r0, STAGE_ROWS), pl.ds(c0, D_MODEL)],
                                 stage_slots[idx % STAGE_SLOTS], sem_ref.at[idx % STAGE_SLOTS])


def _convert_piece(chunks, idx, stage_slots):
    _, w_bf16_ref, r0, c0 = chunks[idx]
    w_bf16_ref[r0:r0 + STAGE_ROWS, c0:c0 + D_MODEL] = stage_slots[idx % STAGE_SLOTS][...].astype(jnp.bfloat16)


def _start_pieces(chunks, first, last, stage_slots, sem_ref):
    for i in range(first, min(last, len(chunks))):
        _stage_copy(chunks, i, stage_slots, sem_ref).start()


def _load_weights_as_bf16(chunks, first, last, stage_slots, sem_ref, ahead):
    assert ahead <= STAGE_SLOTS
    for i in range(first, last):
        _stage_copy(chunks, i, stage_slots, sem_ref).wait()
        _convert_piece(chunks, i, stage_slots)
        _start_pieces(chunks, i + ahead, i + ahead + 1, stage_slots, sem_ref)


def _mixer_kernel(x_hbm, meta_ref, g1_hbm, win_hbm, convw_hbm, retg_hbm, wout_hbm, gf_hbm,
                  cos_hbm, sin_hbm, mrope_ref, decay_ref, xi_ref, zeta_ref, mzeta_ref,
                  out_hbm,
                  win_ref, wout_ref, y_ref, stage_sem, par_ref, par_sem,
                  xbuf, cosbuf, sinbuf, obuf, in_sem, first_sem, out_sem, tail_sem,
                  state_ref, state0_ref, u_ref, hist0_ref, q_ref, qx_ref, k_ref, kz_ref, v_ref,
                  p_ref, sbf_ref, mixed_ref,
                  *, tokens, chunk, chunk_decay, blocks_per_seq, n_blocks):
    bf16 = jnp.bfloat16
    f32 = jnp.float32
    row_pieces = range(0, tokens, STAGE_ROWS)
    stage_slots = ([y_ref.at[r:r + STAGE_ROWS, :] for r in row_pieces]
                   + [obuf.at[slot, r:r + STAGE_ROWS, :] for slot in range(2) for r in row_pieces])
    assert len(stage_slots) == STAGE_SLOTS
    g1_ref = par_ref.at[PAR_G1:PAR_G1 + 1, :]
    gf_ref = par_ref.at[PAR_GF:PAR_GF + 1, :]
    retg_ref = par_ref.at[PAR_RETG:PAR_RETG + 1, 0:D_RET]
    convw_ref = par_ref.at[PAR_CONVW:PAR_CONVW + CONV_WIDTH, 0:D_CONV]
    par_copies = [pltpu.make_async_copy(src, dst, par_sem.at[i]) for i, (src, dst) in enumerate(
        ((g1_hbm, g1_ref), (gf_hbm, gf_ref), (retg_hbm, retg_ref), (convw_hbm, convw_ref)))]

    def _block_rows(blk):
        return blk // blocks_per_seq, pl.ds((blk % blocks_per_seq) * tokens, tokens)

    def _in_copies(blk, slot):
        b, rows = _block_rows(blk)
        return (pltpu.make_async_copy(x_hbm.at[b, rows, :], xbuf.at[slot], in_sem.at[slot, 0]),
                pltpu.make_async_copy(cos_hbm.at[rows, :], cosbuf.at[slot], in_sem.at[slot, 1]),
                pltpu.make_async_copy(sin_hbm.at[rows, :], sinbuf.at[slot], in_sem.at[slot, 2]))

    def _out_copy(blk, slot):
        b, rows = _block_rows(blk)
        return pltpu.make_async_copy(obuf.at[slot], out_hbm.at[b, rows, :], out_sem.at[slot])

    def _finish_rows(out_ref, rows):
        out_ref[rows, :] = _rms_norm(y_ref[rows, :], gf_ref[...])

    def _mix_block(x_ref, cos_ref, sin_ref, out_ref, before_conv=None, hm=None):
        finished = []
        if out_ref is not None:
            group = tokens // FINISH_GROUPS
            for g in range(FINISH_GROUPS):
                rows = slice(g * group, (g + 1) * group)
                _finish_rows(out_ref, rows)
                finished.append(_zero_tile_after(out_ref[rows, :]))

        def _after_finished(lhs, g):
            return _after(lhs, finished[g]) if finished else lhs

        x = x_ref[...]
        hn = _rms_norm(x, g1_ref[...]).astype(bf16)

        n_chunks = tokens // chunk

        cos = cos_ref[...]
        sin = sin_ref[...]
        q = _project(hn, win_ref, Q)
        if hm is None:
            k = _project(hn, win_ref, K)
        else:
            hn_and_hm = jnp.concatenate([hn, hm], axis=0)
            k_all = _project(hn_and_hm, win_ref, K)
            k, k_m = k_all[0:tokens], k_all[tokens:]
        for h in range(RET_HEADS):
            cols = slice(h * HEAD_DIM, (h + 1) * HEAD_DIM)
            qr = _rotary(_head(q, h), cos, sin)
            kr = _rotary(_head(k, h), cos, sin)
            q_ref[:, cols] = qr.astype(bf16)
            k_ref[:, cols] = kr.astype(bf16)
            for c in range(n_chunks):
                rows = slice(c * chunk, (c + 1) * chunk)
                qx_ref[rows, cols] = (qr[rows] * xi_ref[h]).astype(bf16)
                kz_ref[rows, cols] = (kr[rows] * zeta_ref[h]).astype(bf16)
        if hm is None:
            v_ref[...] = _project(_after_finished(hn, 0), win_ref, V).astype(bf16)
        else:
            v_all = _project(hn_and_hm, win_ref, V).astype(bf16)
            v_ref[...] = v_all[0:tokens]
            v_m = v_all[tokens:]
            for h in range(RET_HEADS):
                kr = _rotary(_head(k_m, h), mrope_ref[0:N_META, :], mrope_ref[N_META:, :]) * mzeta_ref[h]
                state0_ref[h] = lax.dot_general(kr.astype(bf16), _head(v_m, h), (((0,), (0,)), ((), ())),
                                                preferred_element_type=f32)
            state_ref[...] = state0_ref[...]

        upd = {}
        for c in range(n_chunks):
            rows = slice(c * chunk, (c + 1) * chunk)
            for h in range(RET_HEADS):
                cols = slice(h * HEAD_DIM, (h + 1) * HEAD_DIM)
                scores = lax.dot_general(q_ref[rows, cols], k_ref[rows, cols], (((1,), (1,)), ((), ())),
                                         preferred_element_type=f32)
                p_ref[c * RET_HEADS + h] = (scores * decay_ref[h]).astype(bf16)
                upd[c, h] = lax.dot_general(kz_ref[rows, cols], v_ref[rows, cols], (((0,), (0,)), ((), ())),
                                            preferred_element_type=f32)
        for h in range(RET_HEADS):
            state = state_ref[h]
            for c in range(n_chunks):
                sbf_ref[c * RET_HEADS + h] = state.astype(bf16)
                state = chunk_decay[h] * state + upd[c, h]
            state_ref[h] = state

        if before_conv is not None:
            before_conv()

        if hm is None:
            u = _project(_after_finished(hn, 1), win_ref, CC) * _project(_after_finished(hn, 2), win_ref, CX)
        else:
            u_all = _project(hn_and_hm, win_ref, CC) * _project(hn_and_hm, win_ref, CX)
            u = u_all[0:tokens]
            hist0_ref[...] = u_all[tokens + N_META - HIST_ROWS:]
            u_ref[0:HIST_ROWS, :] = hist0_ref[...]
        u_ref[HIST_ROWS:HIST_ROWS + tokens, :] = u
        u1 = u_ref[HIST_ROWS - 1:HIST_ROWS - 1 + tokens, :]
        u2 = u_ref[HIST_ROWS - 2:HIST_ROWS - 2 + tokens, :]
        conv = convw_ref[0:1, :] * u2 + convw_ref[1:2, :] * u1 + convw_ref[2:3, :] * u
        conv_out = _project(_after_finished(hn, 3), win_ref, CB) * conv * _silu(_project(hn, win_ref, CG))
        mixed_ref[:, 0:D_CONV] = conv_out.astype(bf16)
        u_ref[0:HIST_ROWS, :] = u_ref[tokens:tokens + HIST_ROWS, :]

        gate = _silu(_project(hn, win_ref, RG))
        for c in range(n_chunks):
            rows = slice(c * chunk, (c + 1) * chunk)
            for h in range(RET_HEADS):
                cols = slice(h * HEAD_DIM, (h + 1) * HEAD_DIM)
                hc = c * RET_HEADS + h
                lhs = jnp.concatenate([p_ref[hc], qx_ref[rows, cols]], axis=1)
                rhs = jnp.concatenate([v_ref[rows, cols], sbf_ref[hc]], axis=0)
                o = jnp.dot(lhs, rhs, preferred_element_type=f32)
                mu = jnp.mean(o, axis=-1, keepdims=True)
                d = o - mu
                var = jnp.mean(d * d, axis=-1, keepdims=True)
                y = d * lax.rsqrt(var + EPS) * retg_ref[:, cols]
                mixed_ref[rows, D_CONV + h * HEAD_DIM:D_CONV + (h + 1) * HEAD_DIM] = (
                    y * gate[rows, cols]).astype(bf16)

        y_ref[...] = jnp.dot(mixed_ref[...], wout_ref[...], preferred_element_type=f32) + x

    for copy in par_copies:
        copy.start()
    chunks = _weight_chunks((win_hbm, win_ref, WIN_PIECE_ORDER), (wout_hbm, wout_ref, (0,)))
    _start_pieces(chunks, 0, EARLY_PIECES, stage_slots, stage_sem)
    first_rows = tokens // FIRST_BLOCK_READS
    first_block_copies = [
        pltpu.make_async_copy(x_hbm.at[0, pl.ds(i * first_rows, first_rows), :],
                              xbuf.at[0, pl.ds(i * first_rows, first_rows), :], first_sem.at[i])
        for i in range(FIRST_BLOCK_READS)] + list(_in_copies(0, 0)[1:])
    for copy in first_block_copies:
        copy.start()
    _load_weights_as_bf16(chunks, 0, EARLY_PIECES, stage_slots, stage_sem, ahead=EARLY_PIECES)
    _start_pieces(chunks, 2 * EARLY_PIECES, len(chunks), stage_slots, stage_sem)
    for copy in par_copies:
        copy.wait()
    hm = _rms_norm(meta_ref[...], g1_ref[...]).astype(bf16)

    def _rest_of_weights():
        assert len(chunks) - EARLY_PIECES <= STAGE_SLOTS
        for i in range(EARLY_PIECES, len(chunks)):
            _stage_copy(chunks, i, stage_slots, stage_sem).wait()
        if n_blocks > 1:
            for copy in _in_copies(1, 1):
                copy.start()
        for i in range(EARLY_PIECES, len(chunks)):
            _convert_piece(chunks, i, stage_slots)

    for copy in first_block_copies:
        copy.wait()
    _mix_block(xbuf.at[0], cosbuf.at[0], sinbuf.at[0], None, before_conv=_rest_of_weights, hm=hm)

    def _step(s, carry):
        slot = s % 2
        prev_slot = 1 - slot
        for copy in _in_copies(s, slot):
            copy.wait()

        @pl.when(s + 1 < n_blocks)
        def _fetch_next():
            for copy in _in_copies(s + 1, prev_slot):
                copy.start()

        @pl.when(s >= 3)
        def _free_out_slot():
            _out_copy(s - 3, prev_slot).wait()

        @pl.when(s % blocks_per_seq == 0)
        def _start_sequence():
            state_ref[...] = state0_ref[...]
            u_ref[0:HIST_ROWS, :] = hist0_ref[...]

        _mix_block(xbuf.at[slot], cosbuf.at[slot], sinbuf.at[slot], obuf.at[prev_slot])
        _out_copy(s - 1, prev_slot).start()
        return carry

    lax.fori_loop(1, n_blocks, _step, 0)

    last = n_blocks - 1
    last_slot = last % 2
    if n_blocks >= 3:
        _out_copy(last - 2, last_slot).wait()
    group = tokens // FINISH_GROUPS
    b_last, rows_last = last // blocks_per_seq, (last % blocks_per_seq) * tokens
    tail_copies = []
    for g in range(FINISH_GROUPS):
        rows = slice(g * group, (g + 1) * group)
        _finish_rows(obuf.at[last_slot], rows)
        tail_copies.append(pltpu.make_async_copy(
            obuf.at[last_slot, rows, :], out_hbm.at[b_last, pl.ds(rows_last + g * group, group), :], tail_sem.at[g]))
        tail_copies[-1].start()
    if n_blocks >= 2:
        _out_copy(last - 1, 1 - last_slot).wait()
    for copy in tail_copies:
        copy.wait()


def kernel(x, meta, norm1_g, w_in, conv_w, ret_norm_g, w_out, final_g):
    bsz, seq, d_model = x.shape
    assert d_model == D_MODEL and meta.shape == (N_META, D_MODEL)
    tokens, chunk = TOKENS_PER_STEP, RET_CHUNK
    assert seq % tokens == 0 and tokens % chunk == 0 and 3 * tokens == STAGE_SLOTS * STAGE_ROWS

    cos, sin = _rope_tables(N_META + seq)
    decay, xi, zeta = _decay_tables(chunk)
    chunk_decay = tuple(float(v) for v in _gammas() ** chunk)
    n_hc = (tokens // chunk) * RET_HEADS

    blocks_per_seq = seq // tokens
    n_blocks = bsz * blocks_per_seq

    in_vmem = pl.BlockSpec(memory_space=pltpu.VMEM)
    in_hbm = pl.BlockSpec(memory_space=pl.ANY)
    body = functools.partial(_mixer_kernel, tokens=tokens, chunk=chunk, chunk_decay=chunk_decay,
                             blocks_per_seq=blocks_per_seq, n_blocks=n_blocks)
    return pl.pallas_call(
        body,
        in_specs=[
            in_hbm,
            in_vmem,
            in_hbm,
            in_hbm,
            in_hbm,
            in_hbm,
            in_hbm,
            in_hbm,
            in_hbm,
            in_hbm,
            in_vmem,
            in_vmem,
            in_vmem,
            in_vmem,
            in_vmem,
        ],
        out_specs=pl.BlockSpec(memory_space=pl.ANY),
        out_shape=jax.ShapeDtypeStruct((bsz, seq, D_MODEL), x.dtype),
        scratch_shapes=[
            pltpu.VMEM((D_MODEL, 8 * SEC), jnp.bfloat16),
            pltpu.VMEM((D_MODEL, D_MODEL), jnp.bfloat16),
            pltpu.VMEM((tokens, D_MODEL), jnp.float32),
            pltpu.SemaphoreType.DMA((STAGE_SLOTS,)),
            pltpu.VMEM((PAR_ROWS, D_MODEL), jnp.float32),
            pltpu.SemaphoreType.DMA((N_PAR_COPIES,)),
            pltpu.VMEM((2, tokens, D_MODEL), jnp.float32),
            pltpu.VMEM((2, tokens, HEAD_DIM), jnp.float32),
            pltpu.VMEM((2, tokens, HEAD_DIM), jnp.float32),
            pltpu.VMEM((2, tokens, D_MODEL), jnp.float32),
            pltpu.SemaphoreType.DMA((2, 3)),
            pltpu.SemaphoreType.DMA((FIRST_BLOCK_READS,)),
            pltpu.SemaphoreType.DMA((2,)),
            pltpu.SemaphoreType.DMA((FINISH_GROUPS,)),
            pltpu.VMEM((RET_HEADS, HEAD_DIM, HEAD_DIM), jnp.float32),
            pltpu.VMEM((RET_HEADS, HEAD_DIM, HEAD_DIM), jnp.float32),
            pltpu.VMEM((HIST_ROWS + tokens, D_CONV), jnp.float32),
            pltpu.VMEM((HIST_ROWS, D_CONV), jnp.float32),
            pltpu.VMEM((tokens, D_RET), jnp.bfloat16),
            pltpu.VMEM((tokens, D_RET), jnp.bfloat16),
            pltpu.VMEM((tokens, D_RET), jnp.bfloat16),
            pltpu.VMEM((tokens, D_RET), jnp.bfloat16),
            pltpu.VMEM((tokens, D_RET), jnp.bfloat16),
            pltpu.VMEM((n_hc, chunk, chunk), jnp.bfloat16),
            pltpu.VMEM((n_hc, HEAD_DIM, HEAD_DIM), jnp.bfloat16),
            pltpu.VMEM((tokens, D_MODEL), jnp.bfloat16),
        ],
        compiler_params=pltpu.CompilerParams(
            vmem_limit_bytes=VMEM_LIMIT_BYTES),
        name="hymba_mixer",
    )(x, meta, norm1_g.reshape(1, -1), w_in, conv_w,
      ret_norm_g.reshape(1, -1), w_out, final_g.reshape(1, -1),
      jnp.asarray(cos[N_META:]), jnp.asarray(sin[N_META:]),
      jnp.asarray(np.concatenate([cos[:N_META], sin[:N_META]], axis=0)),
      jnp.asarray(decay), jnp.asarray(xi), jnp.asarray(zeta), jnp.asarray(_meta_zeta()))
```

```python
import functools

import numpy as np
import jax
import jax.numpy as jnp
from jax import lax
from jax.experimental import pallas as pl
from jax.experimental.pallas import tpu as pltpu

D_MODEL = 1024
N_META = 16
D_CONV = 512
D_RET = 512
RET_HEADS = 4
HEAD_DIM = 128
HALF = HEAD_DIM // 2
CONV_WIDTH = 3
ROPE_BASE = 10000.0
EPS = 1e-6

SEC = 512
CX, CB, CC, CG, Q, K, V, RG = range(8)

TOKENS_PER_STEP = 1024
RET_CHUNK = 128
HIST_ROWS = 8
STAGE_ROWS = 256
STAGE_SLOTS = 12
DMA_PRIORITIES = 2
WIN_PIECE_ORDER = (Q // 2, V // 2, CC // 2, CX // 2)
EARLY_PIECES = 2 * (D_MODEL // STAGE_ROWS)
FIRST_BLOCK_READS = TOKENS_PER_STEP // STAGE_ROWS
PAR_G1, PAR_GF, PAR_RETG, PAR_CONVW, PAR_ROWS = 0, 1, 2, 3, 8
N_PAR_COPIES = 4
FINISH_GROUPS = 4
V7X_LANES = 128
V7X_F32_SUBLANES = 8
V7X_BF16_SUBLANES = 16
V7X_VMEM_BYTES = 64 * 1024 * 1024
VMEM_UNCLAIMED_BYTES = 2 * 1024 * 1024
VMEM_LIMIT_BYTES = V7X_VMEM_BYTES - VMEM_UNCLAIMED_BYTES


def _gammas():
    return 1.0 - 2.0 ** (-5.0 - np.arange(RET_HEADS, dtype=np.float64))


def _rope_tables(n_pos):
    freqs = 1.0 / (ROPE_BASE ** (np.arange(HALF, dtype=np.float64) / HALF))
    ang = np.arange(n_pos, dtype=np.float64)[:, None] * freqs[None, :]
    cos = np.concatenate([np.cos(ang), np.cos(ang)], axis=1)
    sin = np.concatenate([-np.sin(ang), np.sin(ang)], axis=1)
    return cos.astype(np.float32), sin.astype(np.float32)


def _decay_tables(chunk):
    g = _gammas()
    idx = np.arange(chunk, dtype=np.float64)
    diff = idx[:, None] - idx[None, :]
    scale = HEAD_DIM ** -0.5
    decay = np.where(diff[None] >= 0, g[:, None, None] ** np.maximum(diff[None], 0.0), 0.0) * scale
    xi = (g[:, None] ** (idx[None, :] + 1.0)) * scale
    zeta = g[:, None] ** (chunk - 1.0 - idx[None, :])
    xi = np.broadcast_to(xi[:, :, None], (RET_HEADS, chunk, HEAD_DIM))
    zeta = np.broadcast_to(zeta[:, :, None], (RET_HEADS, chunk, HEAD_DIM))
    return decay.astype(np.float32), xi.astype(np.float32), zeta.astype(np.float32)


def _meta_zeta():
    g = _gammas()
    j = np.arange(N_META, dtype=np.float64)
    z = g[:, None] ** (N_META - 1.0 - j[None, :])
    return np.broadcast_to(z[:, :, None], (RET_HEADS, N_META, HEAD_DIM)).astype(np.float32)


def _rms_norm(x, g):
    return x * lax.rsqrt(jnp.mean(x * x, axis=-1, keepdims=True) + EPS) * g


def _silu(x):
    return x * (1.0 / (1.0 + jnp.exp(-x)))


def _rotary(t, cos, sin):
    return t * cos + pltpu.roll(t, HALF, axis=1) * sin


def _project(hn_bf16, win_ref, sec):
    return jnp.dot(hn_bf16, win_ref[:, sec * SEC:(sec + 1) * SEC], preferred_element_type=jnp.float32)


def _head(t, h):
    return t[:, h * HEAD_DIM:(h + 1) * HEAD_DIM]


def _zero_tile_after(stored_f32):
    bits = pltpu.bitcast(stored_f32, jnp.uint32)
    rows, cols = bits.shape
    acc = bits[0:V7X_F32_SUBLANES]
    for r in range(V7X_F32_SUBLANES, rows, V7X_F32_SUBLANES):
        acc = acc | bits[r:r + V7X_F32_SUBLANES]
    word = acc[:, 0:V7X_LANES]
    for c in range(V7X_LANES, cols, V7X_LANES):
        word = word | acc[:, c:c + V7X_LANES]
    return pltpu.bitcast((word >> 16) >> 16, jnp.bfloat16)


def _after(lhs_bf16, zero_tile):
    r, c = V7X_BF16_SUBLANES, V7X_LANES
    top = jnp.concatenate([lhs_bf16[0:r, 0:c] + zero_tile, lhs_bf16[0:r, c:]], axis=1)
    return jnp.concatenate([top, lhs_bf16[r:]], axis=0)


def _weight_chunks(*weights):
    chunks = []
    for w_hbm, w_bf16_ref, piece_order in weights:
        n_rows, n_cols = w_hbm.shape
        assert sorted(piece_order) == list(range(n_cols // D_MODEL))
        for piece in piece_order:
            for r0 in range(0, n_rows, STAGE_ROWS):
                chunks.append((w_hbm, w_bf16_ref, r0, piece * D_MODEL))
    return chunks


def _stage_copy(chunks, idx, stage_slots, sem_ref):
    w_hbm, _, r0, c0 = chunks[idx]
    return pltpu.make_async_copy(w_hbm.at[pl.ds(r0, STAGE_ROWS), pl.ds(c0, D_MODEL)],
                                 stage_slots[idx % STAGE_SLOTS], sem_ref.at[idx % STAGE_SLOTS])


def _convert_piece(chunks, idx, stage_slots):
    _, w_bf16_ref, r0, c0 = chunks[idx]
    w_bf16_ref[r0:r0 + STAGE_ROWS, c0:c0 + D_MODEL] = stage_slots[idx % STAGE_SLOTS][...].astype(jnp.bfloat16)


def _start_pieces(chunks, first, last, stage_slots, sem_ref):
    for i in range(first, min(last, len(chunks))):
        _stage_copy(chunks, i, stage_slots, sem_ref).start(priority=i % DMA_PRIORITIES)


def _load_weights_as_bf16(chunks, first, last, stage_slots, sem_ref, ahead):
    assert ahead <= STAGE_SLOTS
    for i in range(first, last):
        _stage_copy(chunks, i, stage_slots, sem_ref).wait()
        _convert_piece(chunks, i, stage_slots)
        _start_pieces(chunks, i + ahead, i + ahead + 1, stage_slots, sem_ref)


def _mixer_kernel(x_hbm, meta_ref, g1_hbm, win_hbm, convw_hbm, retg_hbm, wout_hbm, gf_hbm,
                  cos_hbm, sin_hbm, mrope_ref, decay_ref, xi_ref, zeta_ref, mzeta_ref,
                  out_hbm,
                  win_ref, wout_ref, y_ref, stage_sem, par_ref, par_sem,
                  xbuf, cosbuf, sinbuf, obuf, in_sem, first_sem, out_sem, tail_sem,
                  state_ref, state0_ref, u_ref, hist0_ref, q_ref, qx_ref, k_ref, kz_ref, v_ref,
                  p_ref, sbf_ref, mixed_ref,
                  *, tokens, chunk, chunk_decay, blocks_per_seq, n_blocks):
    bf16 = jnp.bfloat16
    f32 = jnp.float32
    row_pieces = range(0, tokens, STAGE_ROWS)
    stage_slots = ([y_ref.at[r:r + STAGE_ROWS, :] for r in row_pieces]
                   + [obuf.at[slot, r:r + STAGE_ROWS, :] for slot in range(2) for r in row_pieces])
    assert len(stage_slots) == STAGE_SLOTS
    g1_ref = par_ref.at[PAR_G1:PAR_G1 + 1, :]
    gf_ref = par_ref.at[PAR_GF:PAR_GF + 1, :]
    retg_ref = par_ref.at[PAR_RETG:PAR_RETG + 1, 0:D_RET]
    convw_ref = par_ref.at[PAR_CONVW:PAR_CONVW + CONV_WIDTH, 0:D_CONV]
    par_copies = [pltpu.make_async_copy(src, dst, par_sem.at[i]) for i, (src, dst) in enumerate(
        ((g1_hbm, g1_ref), (gf_hbm, gf_ref), (retg_hbm, retg_ref), (convw_hbm, convw_ref)))]

    def _block_rows(blk):
        return blk // blocks_per_seq, pl.ds((blk % blocks_per_seq) * tokens, tokens)

    def _in_copies(blk, slot):
        b, rows = _block_rows(blk)
        return (pltpu.make_async_copy(x_hbm.at[b, rows, :], xbuf.at[slot], in_sem.at[slot, 0]),
                pltpu.make_async_copy(cos_hbm.at[rows, :], cosbuf.at[slot], in_sem.at[slot, 1]),
                pltpu.make_async_copy(sin_hbm.at[rows, :], sinbuf.at[slot], in_sem.at[slot, 2]))

    def _out_copy(blk, slot):
        b, rows = _block_rows(blk)
        return pltpu.make_async_copy(obuf.at[slot], out_hbm.at[b, rows, :], out_sem.at[slot])

    def _finish_rows(out_ref, rows):
        out_ref[rows, :] = _rms_norm(y_ref[rows, :], gf_ref[...])

    def _mix_block(x_ref, cos_ref, sin_ref, out_ref, before_conv=None, hm=None):
        finished = []
        if out_ref is not None:
            group = tokens // FINISH_GROUPS
            for g in range(FINISH_GROUPS):
                rows = slice(g * group, (g + 1) * group)
                _finish_rows(out_ref, rows)
                finished.append(_zero_tile_after(out_ref[rows, :]))

        def _after_finished(lhs, g):
            return _after(lhs, finished[g]) if finished else lhs

        x = x_ref[...]
        hn = _rms_norm(x, g1_ref[...]).astype(bf16)

        n_chunks = tokens // chunk

        cos = cos_ref[...]
        sin = sin_ref[...]
        q = _project(hn, win_ref, Q)
        if hm is None:
            k = _project(hn, win_ref, K)
        else:
            hn_and_hm = jnp.concatenate([hn, hm], axis=0)
            k_all = _project(hn_and_hm, win_ref, K)
            k, k_m = k_all[0:tokens], k_all[tokens:]
        for h in range(RET_HEADS):
            cols = slice(h * HEAD_DIM, (h + 1) * HEAD_DIM)
            qr = _rotary(_head(q, h), cos, sin)
            kr = _rotary(_head(k, h), cos, sin)
            q_ref[:, cols] = qr.astype(bf16)
            k_ref[:, cols] = kr.astype(bf16)
            for c in range(n_chunks):
                rows = slice(c * chunk, (c + 1) * chunk)
                qx_ref[rows, cols] = (qr[rows] * xi_ref[h]).astype(bf16)
                kz_ref[rows, cols] = (kr[rows] * zeta_ref[h]).astype(bf16)
        if hm is None:
            v_ref[...] = _project(_after_finished(hn, 0), win_ref, V).astype(bf16)
        else:
            v_all = _project(hn_and_hm, win_ref, V).astype(bf16)
            v_ref[...] = v_all[0:tokens]
            v_m = v_all[tokens:]
            for h in range(RET_HEADS):
                kr = _rotary(_head(k_m, h), mrope_ref[0:N_META, :], mrope_ref[N_META:, :]) * mzeta_ref[h]
                state0_ref[h] = lax.dot_general(kr.astype(bf16), _head(v_m, h), (((0,), (0,)), ((), ())),
                                                preferred_element_type=f32)
            state_ref[...] = state0_ref[...]

        upd = {}
        for c in range(n_chunks):
            rows = slice(c * chunk, (c + 1) * chunk)
            for h in range(RET_HEADS):
                cols = slice(h * HEAD_DIM, (h + 1) * HEAD_DIM)
                scores = lax.dot_general(q_ref[rows, cols], k_ref[rows, cols], (((1,), (1,)), ((), ())),
                                         preferred_element_type=f32)
                p_ref[c * RET_HEADS + h] = (scores * decay_ref[h]).astype(bf16)
                upd[c, h] = lax.dot_general(kz_ref[rows, cols], v_ref[rows, cols], (((0,), (0,)), ((), ())),
                                            preferred_element_type=f32)
        for h in range(RET_HEADS):
            state = state_ref[h]
            for c in range(n_chunks):
                sbf_ref[c * RET_HEADS + h] = state.astype(bf16)
                state = chunk_decay[h] * state + upd[c, h]
            state_ref[h] = state

        if before_conv is not None:
            before_conv()

        if hm is None:
            u = _project(_after_finished(hn, 1), win_ref, CC) * _project(_after_finished(hn, 2), win_ref, CX)
        else:
            u_all = _project(hn_and_hm, win_ref, CC) * _project(hn_and_hm, win_ref, CX)
            u = u_all[0:tokens]
            hist0_ref[...] = u_all[tokens + N_META - HIST_ROWS:]
            u_ref[0:HIST_ROWS, :] = hist0_ref[...]
        u_ref[HIST_ROWS:HIST_ROWS + tokens, :] = u
        u1 = u_ref[HIST_ROWS - 1:HIST_ROWS - 1 + tokens, :]
        u2 = u_ref[HIST_ROWS - 2:HIST_ROWS - 2 + tokens, :]
        conv = convw_ref[0:1, :] * u2 + convw_ref[1:2, :] * u1 + convw_ref[2:3, :] * u
        conv_out = _project(_after_finished(hn, 3), win_ref, CB) * conv * _silu(_project(hn, win_ref, CG))
        mixed_ref[:, 0:D_CONV] = conv_out.astype(bf16)
        u_ref[0:HIST_ROWS, :] = u_ref[tokens:tokens + HIST_ROWS, :]

        gate = _silu(_project(hn, win_ref, RG))
        for c in range(n_chunks):
            rows = slice(c * chunk, (c + 1) * chunk)
            for h in range(RET_HEADS):
                cols = slice(h * HEAD_DIM, (h + 1) * HEAD_DIM)
                hc = c * RET_HEADS + h
                lhs = jnp.concatenate([p_ref[hc], qx_ref[rows, cols]], axis=1)
                rhs = jnp.concatenate([v_ref[rows, cols], sbf_ref[hc]], axis=0)
                o = jnp.dot(lhs, rhs, preferred_element_type=f32)
                mu = jnp.mean(o, axis=-1, keepdims=True)
                d = o - mu
                var = jnp.mean(d * d, axis=-1, keepdims=True)
                y = d * lax.rsqrt(var + EPS) * retg_ref[:, cols]
                mixed_ref[rows, D_CONV + h * HEAD_DIM:D_CONV + (h + 1) * HEAD_DIM] = (
                    y * gate[rows, cols]).astype(bf16)

        y_ref[...] = jnp.dot(mixed_ref[...], wout_ref[...], preferred_element_type=f32) + x

    for copy in par_copies:
        copy.start()
    chunks = _weight_chunks((win_hbm, win_ref, WIN_PIECE_ORDER), (wout_hbm, wout_ref, (0,)))
    _start_pieces(chunks, 0, EARLY_PIECES, stage_slots, stage_sem)
    first_rows = tokens // FIRST_BLOCK_READS
    first_block_copies = [
        pltpu.make_async_copy(x_hbm.at[0, pl.ds(i * first_rows, first_rows), :],
                              xbuf.at[0, pl.ds(i * first_rows, first_rows), :], first_sem.at[i])
        for i in range(FIRST_BLOCK_READS)] + list(_in_copies(0, 0)[1:])
    for i, copy in enumerate(first_block_copies):
        copy.start(priority=i % DMA_PRIORITIES)
    _load_weights_as_bf16(chunks, 0, EARLY_PIECES, stage_slots, stage_sem, ahead=EARLY_PIECES)
    _start_pieces(chunks, 2 * EARLY_PIECES, len(chunks), stage_slots, stage_sem)
    for copy in par_copies:
        copy.wait()
    hm = _rms_norm(meta_ref[...], g1_ref[...]).astype(bf16)

    def _rest_of_weights():
        assert len(chunks) - EARLY_PIECES <= STAGE_SLOTS
        for i in range(EARLY_PIECES, len(chunks)):
            _stage_copy(chunks, i, stage_slots, stage_sem).wait()
        if n_blocks > 1:
            for copy in _in_copies(1, 1):
                copy.start()
        for i in range(EARLY_PIECES, len(chunks)):
            _convert_piece(chunks, i, stage_slots)

    for copy in first_block_copies:
        copy.wait()
    _mix_block(xbuf.at[0], cosbuf.at[0], sinbuf.at[0], None, before_conv=_rest_of_weights, hm=hm)

    def _step(s, carry):
        slot = s % 2
        prev_slot = 1 - slot
        for copy in _in_copies(s, slot):
            copy.wait()

        @pl.when(s + 1 < n_blocks)
        def _fetch_next():
            for copy in _in_copies(s + 1, prev_slot):
                copy.start()

        @pl.when(s >= 3)
        def _free_out_slot():
            _out_copy(s - 3, prev_slot).wait()

        @pl.when(s % blocks_per_seq == 0)
        def _start_sequence():
            state_ref[...] = state0_ref[...]
            u_ref[0:HIST_ROWS, :] = hist0_ref[...]

        _mix_block(xbuf.at[slot], cosbuf.at[slot], sinbuf.at[slot], obuf.at[prev_slot])
        _out_copy(s - 1, prev_slot).start()
        return carry

    lax.fori_loop(1, n_blocks, _step, 0)

    last = n_blocks - 1
    last_slot = last % 2
    if n_blocks >= 3:
        _out_copy(last - 2, last_slot).wait()
    group = tokens // FINISH_GROUPS
    b_last, rows_last = last // blocks_per_seq, (last % blocks_per_seq) * tokens
    tail_copies = []
    for g in range(FINISH_GROUPS):
        rows = slice(g * group, (g + 1) * group)
        _finish_rows(obuf.at[last_slot], rows)
        tail_copies.append(pltpu.make_async_copy(
            obuf.at[last_slot, rows, :], out_hbm.at[b_last, pl.ds(rows_last + g * group, group), :], tail_sem.at[g]))
        tail_copies[-1].start()
    if n_blocks >= 2:
        _out_copy(last - 1, 1 - last_slot).wait()
    for copy in tail_copies:
        copy.wait()


def kernel(x, meta, norm1_g, w_in, conv_w, ret_norm_g, w_out, final_g):
    bsz, seq, d_model = x.shape
    assert d_model == D_MODEL and meta.shape == (N_META, D_MODEL)
    tokens, chunk = TOKENS_PER_STEP, RET_CHUNK
    assert seq % tokens == 0 and tokens % chunk == 0 and 3 * tokens == STAGE_SLOTS * STAGE_ROWS

    cos, sin = _rope_tables(N_META + seq)
    decay, xi, zeta = _decay_tables(chunk)
    chunk_decay = tuple(float(v) for v in _gammas() ** chunk)
    n_hc = (tokens // chunk) * RET_HEADS

    blocks_per_seq = seq // tokens
    n_blocks = bsz * blocks_per_seq

    in_vmem = pl.BlockSpec(memory_space=pltpu.VMEM)
    in_hbm = pl.BlockSpec(memory_space=pl.ANY)
    body = functools.partial(_mixer_kernel, tokens=tokens, chunk=chunk, chunk_decay=chunk_decay,
                             blocks_per_seq=blocks_per_seq, n_blocks=n_blocks)
    return pl.pallas_call(
        body,
        in_specs=[
            in_hbm,
            in_vmem,
            in_hbm,
            in_hbm,
            in_hbm,
            in_hbm,
            in_hbm,
            in_hbm,
            in_hbm,
            in_hbm,
            in_vmem,
            in_vmem,
            in_vmem,
            in_vmem,
            in_vmem,
        ],
        out_specs=pl.BlockSpec(memory_space=pl.ANY),
        out_shape=jax.ShapeDtypeStruct((bsz, seq, D_MODEL), x.dtype),
        scratch_shapes=[
            pltpu.VMEM((D_MODEL, 8 * SEC), jnp.bfloat16),
            pltpu.VMEM((D_MODEL, D_MODEL), jnp.bfloat16),
            pltpu.VMEM((tokens, D_MODEL), jnp.float32),
            pltpu.SemaphoreType.DMA((STAGE_SLOTS,)),
            pltpu.VMEM((PAR_ROWS, D_MODEL), jnp.float32),
            pltpu.SemaphoreType.DMA((N_PAR_COPIES,)),
            pltpu.VMEM((2, tokens, D_MODEL), jnp.float32),
            pltpu.VMEM((2, tokens, HEAD_DIM), jnp.float32),
            pltpu.VMEM((2, tokens, HEAD_DIM), jnp.float32),
            pltpu.VMEM((2, tokens, D_MODEL), jnp.float32),
            pltpu.SemaphoreType.DMA((2, 3)),
            pltpu.SemaphoreType.DMA((FIRST_BLOCK_READS,)),
            pltpu.SemaphoreType.DMA((2,)),
            pltpu.SemaphoreType.DMA((FINISH_GROUPS,)),
            pltpu.VMEM((RET_HEADS, HEAD_DIM, HEAD_DIM), jnp.float32),
            pltpu.VMEM((RET_HEADS, HEAD_DIM, HEAD_DIM), jnp.float32),
            pltpu.VMEM((HIST_ROWS + tokens, D_CONV), jnp.float32),
            pltpu.VMEM((HIST_ROWS, D_CONV), jnp.float32),
            pltpu.VMEM((tokens, D_RET), jnp.bfloat16),
            pltpu.VMEM((tokens, D_RET), jnp.bfloat16),
            pltpu.VMEM((tokens, D_RET), jnp.bfloat16),
            pltpu.VMEM((tokens, D_RET), jnp.bfloat16),
            pltpu.VMEM((tokens, D_RET), jnp.bfloat16),
            pltpu.VMEM((n_hc, chunk, chunk), jnp.bfloat16),
            pltpu.VMEM((n_hc, HEAD_DIM, HEAD_DIM), jnp.bfloat16),
            pltpu.VMEM((tokens, D_MODEL), jnp.bfloat16),
        ],
        compiler_params=pltpu.CompilerParams(
            vmem_limit_bytes=VMEM_LIMIT_BYTES),
        name="hymba_mixer",
    )(x, meta, norm1_g.reshape(1, -1), w_in, conv_w,
      ret_norm_g.reshape(1, -1), w_out, final_g.reshape(1, -1),
      jnp.asarray(cos[N_META:]), jnp.asarray(sin[N_META:]),
      jnp.asarray(np.concatenate([cos[:N_META], sin[:N_META]], axis=0)),
      jnp.asarray(decay), jnp.asarray(xi), jnp.asarray(zeta), jnp.asarray(_meta_zeta()))
```

```python
import functools

import numpy as np
import jax
import jax.numpy as jnp
from jax import lax
from jax.experimental import pallas as pl
from jax.experimental.pallas import tpu as pltpu

D_MODEL = 1024
N_META = 16
D_CONV = 512
D_RET = 512
RET_HEADS = 4
HEAD_DIM = 128
HALF = HEAD_DIM // 2
CONV_WIDTH = 3
ROPE_BASE = 10000.0
EPS = 1e-6

SEC = 512
CX, CB, CC, CG, Q, K, V, RG = range(8)

TOKENS_PER_STEP = 1024
RET_CHUNK = 128
HIST_ROWS = 8
STAGE_ROWS = 256
STAGE_SLOTS = 12
CONV_COLS = 256
WIN_PIECE_ORDER = (Q // 2, V // 2, CC // 2, CX // 2)
EARLY_PIECES = 2 * (D_MODEL // STAGE_ROWS)
FIRST_BLOCK_READS = TOKENS_PER_STEP // STAGE_ROWS
PAR_G1, PAR_GF, PAR_RETG, PAR_CONVW, PAR_ROWS = 0, 1, 2, 3, 8
N_PAR_COPIES = 4
FINISH_GROUPS = 4
V7X_LANES = 128
V7X_F32_SUBLANES = 8
V7X_BF16_SUBLANES = 16
V7X_VMEM_BYTES = 64 * 1024 * 1024
VMEM_UNCLAIMED_BYTES = 2 * 1024 * 1024
VMEM_LIMIT_BYTES = V7X_VMEM_BYTES - VMEM_UNCLAIMED_BYTES


def _gammas():
    return 1.0 - 2.0 ** (-5.0 - np.arange(RET_HEADS, dtype=np.float64))


def _rope_tables(n_pos):
    freqs = 1.0 / (ROPE_BASE ** (np.arange(HALF, dtype=np.float64) / HALF))
    ang = np.arange(n_pos, dtype=np.float64)[:, None] * freqs[None, :]
    cos = np.concatenate([np.cos(ang), np.cos(ang)], axis=1)
    sin = np.concatenate([-np.sin(ang), np.sin(ang)], axis=1)
    return cos.astype(np.float32), sin.astype(np.float32)


def _decay_tables(chunk):
    g = _gammas()
    idx = np.arange(chunk, dtype=np.float64)
    diff = idx[:, None] - idx[None, :]
    scale = HEAD_DIM ** -0.5
    decay = np.where(diff[None] >= 0, g[:, None, None] ** np.maximum(diff[None], 0.0), 0.0) * scale
    xi = (g[:, None] ** (idx[None, :] + 1.0)) * scale
    zeta = g[:, None] ** (chunk - 1.0 - idx[None, :])
    xi = np.broadcast_to(xi[:, :, None], (RET_HEADS, chunk, HEAD_DIM))
    zeta = np.broadcast_to(zeta[:, :, None], (RET_HEADS, chunk, HEAD_DIM))
    return decay.astype(np.float32), xi.astype(np.float32), zeta.astype(np.float32)


def _meta_zeta():
    g = _gammas()
    j = np.arange(N_META, dtype=np.float64)
    z = g[:, None] ** (N_META - 1.0 - j[None, :])
    return np.broadcast_to(z[:, :, None], (RET_HEADS, N_META, HEAD_DIM)).astype(np.float32)


def _rms_norm(x, g):
    return x * lax.rsqrt(jnp.mean(x * x, axis=-1, keepdims=True) + EPS) * g


def _silu(x):
    return x * (1.0 / (1.0 + jnp.exp(-x)))


def _rotary(t, cos, sin):
    return t * cos + pltpu.roll(t, HALF, axis=1) * sin


def _project(hn_bf16, win_ref, sec):
    return jnp.dot(hn_bf16, win_ref[:, sec * SEC:(sec + 1) * SEC], preferred_element_type=jnp.float32)


def _head(t, h):
    return t[:, h * HEAD_DIM:(h + 1) * HEAD_DIM]


def _zero_tile_after(stored_f32):
    bits = pltpu.bitcast(stored_f32, jnp.uint32)
    rows, cols = bits.shape
    acc = bits[0:V7X_F32_SUBLANES]
    for r in range(V7X_F32_SUBLANES, rows, V7X_F32_SUBLANES):
        acc = acc | bits[r:r + V7X_F32_SUBLANES]
    word = acc[:, 0:V7X_LANES]
    for c in range(V7X_LANES, cols, V7X_LANES):
        word = word | acc[:, c:c + V7X_LANES]
    return pltpu.bitcast((word >> 16) >> 16, jnp.bfloat16)


def _after(lhs_bf16, zero_tile):
    r, c = V7X_BF16_SUBLANES, V7X_LANES
    top = jnp.concatenate([lhs_bf16[0:r, 0:c] + zero_tile, lhs_bf16[0:r, c:]], axis=1)
    return jnp.concatenate([top, lhs_bf16[r:]], axis=0)


def _weight_chunks(*weights):
    chunks = []
    for w_hbm, w_bf16_ref, piece_order in weights:
        n_rows, n_cols = w_hbm.shape
        assert sorted(piece_order) == list(range(n_cols // D_MODEL))
        for piece in piece_order:
            for r0 in range(0, n_rows, STAGE_ROWS):
                chunks.append((w_hbm, w_bf16_ref, r0, piece * D_MODEL))
    return chunks


def _stage_copy(chunks, idx, stage_slots, sem_ref):
    w_hbm, _, r0, c0 = chunks[idx]
    return pltpu.make_async_copy(w_hbm.at[pl.ds(r0, STAGE_ROWS), pl.ds(c0, D_MODEL)],
                                 stage_slots[idx % STAGE_SLOTS], sem_ref.at[idx % STAGE_SLOTS])


def _convert_piece(chunks, idx, stage_slots):
    _, w_bf16_ref, r0, c0 = chunks[idx]
    w_bf16_ref[r0:r0 + STAGE_ROWS, c0:c0 + D_MODEL] = stage_slots[idx % STAGE_SLOTS][...].astype(jnp.bfloat16)


def _start_pieces(chunks, first, last, stage_slots, sem_ref):
    for i in range(first, min(last, len(chunks))):
        _stage_copy(chunks, i, stage_slots, sem_ref).start()


def _load_weights_as_bf16(chunks, first, last, stage_slots, sem_ref, ahead):
    assert ahead <= STAGE_SLOTS
    for i in range(first, last):
        _stage_copy(chunks, i, stage_slots, sem_ref).wait()
        _convert_piece(chunks, i, stage_slots)
        _start_pieces(chunks, i + ahead, i + ahead + 1, stage_slots, sem_ref)


def _mixer_kernel(x_hbm, meta_ref, g1_hbm, win_hbm, convw_hbm, retg_hbm, wout_hbm, gf_hbm,
                  cos_hbm, sin_hbm, mrope_ref, decay_ref, xi_ref, zeta_ref, mzeta_ref,
                  out_hbm,
                  win_ref, wout_ref, y_ref, stage_sem, par_ref, par_sem,
                  xbuf, cosbuf, sinbuf, obuf, in_sem, first_sem, out_sem, tail_sem,
                  state_ref, state0_ref, u_ref, hist0_ref, q_ref, qx_ref, k_ref, kz_ref, v_ref,
                  p_ref, sbf_ref, mixed_ref,
                  *, tokens, chunk, chunk_decay, blocks_per_seq, n_blocks):
    bf16 = jnp.bfloat16
    f32 = jnp.float32
    row_pieces = range(0, tokens, STAGE_ROWS)
    stage_slots = ([y_ref.at[r:r + STAGE_ROWS, :] for r in row_pieces]
                   + [obuf.at[slot, r:r + STAGE_ROWS, :] for slot in range(2) for r in row_pieces])
    assert len(stage_slots) == STAGE_SLOTS
    g1_ref = par_ref.at[PAR_G1:PAR_G1 + 1, :]
    gf_ref = par_ref.at[PAR_GF:PAR_GF + 1, :]
    retg_ref = par_ref.at[PAR_RETG:PAR_RETG + 1, 0:D_RET]
    convw_ref = par_ref.at[PAR_CONVW:PAR_CONVW + CONV_WIDTH, 0:D_CONV]
    par_copies = [pltpu.make_async_copy(src, dst, par_sem.at[i]) for i, (src, dst) in enumerate(
        ((g1_hbm, g1_ref), (gf_hbm, gf_ref), (retg_hbm, retg_ref), (convw_hbm, convw_ref)))]

    def _block_rows(blk):
        return blk // blocks_per_seq, pl.ds((blk % blocks_per_seq) * tokens, tokens)

    def _in_copies(blk, slot):
        b, rows = _block_rows(blk)
        return (pltpu.make_async_copy(x_hbm.at[b, rows, :], xbuf.at[slot], in_sem.at[slot, 0]),
                pltpu.make_async_copy(cos_hbm.at[rows, :], cosbuf.at[slot], in_sem.at[slot, 1]),
                pltpu.make_async_copy(sin_hbm.at[rows, :], sinbuf.at[slot], in_sem.at[slot, 2]))

    def _out_copy(blk, slot):
        b, rows = _block_rows(blk)
        return pltpu.make_async_copy(obuf.at[slot], out_hbm.at[b, rows, :], out_sem.at[slot])

    def _finish_rows(out_ref, rows):
        out_ref[rows, :] = _rms_norm(y_ref[rows, :], gf_ref[...])

    def _mix_block(x_ref, cos_ref, sin_ref, out_ref, before_conv=None, hm=None):
        finished = []
        if out_ref is not None:
            group = tokens // FINISH_GROUPS
            for g in range(FINISH_GROUPS):
                rows = slice(g * group, (g + 1) * group)
                _finish_rows(out_ref, rows)
                finished.append(_zero_tile_after(out_ref[rows, :]))

        def _after_finished(lhs, g):
            return _after(lhs, finished[g]) if finished else lhs

        x = x_ref[...]
        hn = _rms_norm(x, g1_ref[...]).astype(bf16)

        n_chunks = tokens // chunk

        cos = cos_ref[...]
        sin = sin_ref[...]
        q = _project(hn, win_ref, Q)
        if hm is None:
            k = _project(hn, win_ref, K)
        else:
            hn_and_hm = jnp.concatenate([hn, hm], axis=0)
            k_all = _project(hn_and_hm, win_ref, K)
            k, k_m = k_all[0:tokens], k_all[tokens:]
        for h in range(RET_HEADS):
            cols = slice(h * HEAD_DIM, (h + 1) * HEAD_DIM)
            qr = _rotary(_head(q, h), cos, sin)
            kr = _rotary(_head(k, h), cos, sin)
            q_ref[:, cols] = qr.astype(bf16)
            k_ref[:, cols] = kr.astype(bf16)
            for c in range(n_chunks):
                rows = slice(c * chunk, (c + 1) * chunk)
                qx_ref[rows, cols] = (qr[rows] * xi_ref[h]).astype(bf16)
                kz_ref[rows, cols] = (kr[rows] * zeta_ref[h]).astype(bf16)
        if hm is None:
            v_ref[...] = _project(_after_finished(hn, 0), win_ref, V).astype(bf16)
        else:
            v_all = _project(hn_and_hm, win_ref, V).astype(bf16)
            v_ref[...] = v_all[0:tokens]
            v_m = v_all[tokens:]
            for h in range(RET_HEADS):
                kr = _rotary(_head(k_m, h), mrope_ref[0:N_META, :], mrope_ref[N_META:, :]) * mzeta_ref[h]
                state0_ref[h] = lax.dot_general(kr.astype(bf16), _head(v_m, h), (((0,), (0,)), ((), ())),
                                                preferred_element_type=f32)
            state_ref[...] = state0_ref[...]

        upd = {}
        for c in range(n_chunks):
            rows = slice(c * chunk, (c + 1) * chunk)
            for h in range(RET_HEADS):
                cols = slice(h * HEAD_DIM, (h + 1) * HEAD_DIM)
                scores = lax.dot_general(q_ref[rows, cols], k_ref[rows, cols], (((1,), (1,)), ((), ())),
                                         preferred_element_type=f32)
                p_ref[c * RET_HEADS + h] = (scores * decay_ref[h]).astype(bf16)
                upd[c, h] = lax.dot_general(kz_ref[rows, cols], v_ref[rows, cols], (((0,), (0,)), ((), ())),
                                            preferred_element_type=f32)
        for h in range(RET_HEADS):
            state = state_ref[h]
            for c in range(n_chunks):
                sbf_ref[c * RET_HEADS + h] = state.astype(bf16)
                state = chunk_decay[h] * state + upd[c, h]
            state_ref[h] = state

        if before_conv is not None:
            before_conv()

        for half in range(D_CONV // CONV_COLS):
            cols = slice(half * CONV_COLS, (half + 1) * CONV_COLS)

            def _half(lhs, sec):
                c0 = sec * SEC + half * CONV_COLS
                return jnp.dot(lhs, win_ref[:, c0:c0 + CONV_COLS], preferred_element_type=f32)

            if hm is None:
                u = _half(_after_finished(hn, 1) if half == 0 else hn, CC) * _half(
                    _after_finished(hn, 2) if half == 0 else hn, CX)
            else:
                u_all = _half(hn_and_hm, CC) * _half(hn_and_hm, CX)
                u = u_all[0:tokens]
                hist0_ref[:, cols] = u_all[tokens + N_META - HIST_ROWS:]
                u_ref[0:HIST_ROWS, cols] = hist0_ref[:, cols]
            u_ref[HIST_ROWS:HIST_ROWS + tokens, cols] = u
            u1 = u_ref[HIST_ROWS - 1:HIST_ROWS - 1 + tokens, cols]
            u2 = u_ref[HIST_ROWS - 2:HIST_ROWS - 2 + tokens, cols]
            conv = convw_ref[0:1, cols] * u2 + convw_ref[1:2, cols] * u1 + convw_ref[2:3, cols] * u
            conv_out = (_half(_after_finished(hn, 3) if half == 1 else hn, CB) * conv
                        * _silu(_half(hn, CG)))
            mixed_ref[:, cols] = conv_out.astype(bf16)
        u_ref[0:HIST_ROWS, :] = u_ref[tokens:tokens + HIST_ROWS, :]

        gate = _silu(_project(hn, win_ref, RG))
        for c in range(n_chunks):
            rows = slice(c * chunk, (c + 1) * chunk)
            for h in range(RET_HEADS):
                cols = slice(h * HEAD_DIM, (h + 1) * HEAD_DIM)
                hc = c * RET_HEADS + h
                lhs = jnp.concatenate([p_ref[hc], qx_ref[rows, cols]], axis=1)
                rhs = jnp.concatenate([v_ref[rows, cols], sbf_ref[hc]], axis=0)
                o = jnp.dot(lhs, rhs, preferred_element_type=f32)
                mu = jnp.mean(o, axis=-1, keepdims=True)
                d = o - mu
                var = jnp.mean(d * d, axis=-1, keepdims=True)
                y = d * lax.rsqrt(var + EPS) * retg_ref[:, cols]
                mixed_ref[rows, D_CONV + h * HEAD_DIM:D_CONV + (h + 1) * HEAD_DIM] = (
                    y * gate[rows, cols]).astype(bf16)

        y_ref[...] = jnp.dot(mixed_ref[...], wout_ref[...], preferred_element_type=f32) + x

    for copy in par_copies:
        copy.start()
    chunks = _weight_chunks((win_hbm, win_ref, WIN_PIECE_ORDER), (wout_hbm, wout_ref, (0,)))
    _start_pieces(chunks, 0, EARLY_PIECES, stage_slots, stage_sem)
    first_rows = tokens // FIRST_BLOCK_READS
    first_block_copies = [
        pltpu.make_async_copy(x_hbm.at[0, pl.ds(i * first_rows, first_rows), :],
                              xbuf.at[0, pl.ds(i * first_rows, first_rows), :], first_sem.at[i])
        for i in range(FIRST_BLOCK_READS)] + list(_in_copies(0, 0)[1:])
    for copy in first_block_copies:
        copy.start()
    _load_weights_as_bf16(chunks, 0, EARLY_PIECES, stage_slots, stage_sem, ahead=EARLY_PIECES)
    _start_pieces(chunks, 2 * EARLY_PIECES, len(chunks), stage_slots, stage_sem)
    for copy in par_copies:
        copy.wait()
    hm = _rms_norm(meta_ref[...], g1_ref[...]).astype(bf16)

    def _rest_of_weights():
        assert len(chunks) - EARLY_PIECES <= STAGE_SLOTS
        for i in range(EARLY_PIECES, len(chunks)):
            _stage_copy(chunks, i, stage_slots, stage_sem).wait()
        if n_blocks > 1:
            for copy in _in_copies(1, 1):
                copy.start()
        for i in range(EARLY_PIECES, len(chunks)):
            _convert_piece(chunks, i, stage_slots)

    for copy in first_block_copies:
        copy.wait()
    _mix_block(xbuf.at[0], cosbuf.at[0], sinbuf.at[0], None, before_conv=_rest_of_weights, hm=hm)

    def _step(s, carry):
        slot = s % 2
        prev_slot = 1 - slot
        for copy in _in_copies(s, slot):
            copy.wait()

        @pl.when(s + 1 < n_blocks)
        def _fetch_next():
            for copy in _in_copies(s + 1, prev_slot):
                copy.start()

        @pl.when(s >= 3)
        def _free_out_slot():
            _out_copy(s - 3, prev_slot).wait()

        @pl.when(s % blocks_per_seq == 0)
        def _start_sequence():
            state_ref[...] = state0_ref[...]
            u_ref[0:HIST_ROWS, :] = hist0_ref[...]

        _mix_block(xbuf.at[slot], cosbuf.at[slot], sinbuf.at[slot], obuf.at[prev_slot])
        _out_copy(s - 1, prev_slot).start()
        return carry

    lax.fori_loop(1, n_blocks, _step, 0)

    last = n_blocks - 1
    last_slot = last % 2
    if n_blocks >= 3:
        _out_copy(last - 2, last_slot).wait()
    group = tokens // FINISH_GROUPS
    b_last, rows_last = last // blocks_per_seq, (last % blocks_per_seq) * tokens
    tail_copies = []
    for g in range(FINISH_GROUPS):
        rows = slice(g * group, (g + 1) * group)
        _finish_rows(obuf.at[last_slot], rows)
        tail_copies.append(pltpu.make_async_copy(
            obuf.at[last_slot, rows, :], out_hbm.at[b_last, pl.ds(rows_last + g * group, group), :], tail_sem.at[g]))
        tail_copies[-1].start()
    if n_blocks >= 2:
        _out_copy(last - 1, 1 - last_slot).wait()
    for copy in tail_copies:
        copy.wait()


def kernel(x, meta, norm1_g, w_in, conv_w, ret_norm_g, w_out, final_g):
    bsz, seq, d_model = x.shape
    assert d_model == D_MODEL and meta.shape == (N_META, D_MODEL)
    tokens, chunk = TOKENS_PER_STEP, RET_CHUNK
    assert seq % tokens == 0 and tokens % chunk == 0 and 3 * tokens == STAGE_SLOTS * STAGE_ROWS

    cos, sin = _rope_tables(N_META + seq)
    decay, xi, zeta = _decay_tables(chunk)
    chunk_decay = tuple(float(v) for v in _gammas() ** chunk)
    n_hc = (tokens // chunk) * RET_HEADS

    blocks_per_seq = seq // tokens
    n_blocks = bsz * blocks_per_seq

    in_vmem = pl.BlockSpec(memory_space=pltpu.VMEM)
    in_hbm = pl.BlockSpec(memory_space=pl.ANY)
    body = functools.partial(_mixer_kernel, tokens=tokens, chunk=chunk, chunk_decay=chunk_decay,
                             blocks_per_seq=blocks_per_seq, n_blocks=n_blocks)
    return pl.pallas_call(
        body,
        in_specs=[
            in_hbm,
            in_vmem,
            in_hbm,
            in_hbm,
            in_hbm,
            in_hbm,
            in_hbm,
            in_hbm,
            in_hbm,
            in_hbm,
            in_vmem,
            in_vmem,
            in_vmem,
            in_vmem,
            in_vmem,
        ],
        out_specs=pl.BlockSpec(memory_space=pl.ANY),
        out_shape=jax.ShapeDtypeStruct((bsz, seq, D_MODEL), x.dtype),
        scratch_shapes=[
            pltpu.VMEM((D_MODEL, 8 * SEC), jnp.bfloat16),
            pltpu.VMEM((D_MODEL, D_MODEL), jnp.bfloat16),
            pltpu.VMEM((tokens, D_MODEL), jnp.float32),
            pltpu.SemaphoreType.DMA((STAGE_SLOTS,)),
            pltpu.VMEM((PAR_ROWS, D_MODEL), jnp.float32),
            pltpu.SemaphoreType.DMA((N_PAR_COPIES,)),
            pltpu.VMEM((2, tokens, D_MODEL), jnp.float32),
            pltpu.VMEM((2, tokens, HEAD_DIM), jnp.float32),
            pltpu.VMEM((2, tokens, HEAD_DIM), jnp.float32),
            pltpu.VMEM((2, tokens, D_MODEL), jnp.float32),
            pltpu.SemaphoreType.DMA((2, 3)),
            pltpu.SemaphoreType.DMA((FIRST_BLOCK_READS,)),
            pltpu.SemaphoreType.DMA((2,)),
            pltpu.SemaphoreType.DMA((FINISH_GROUPS,)),
            pltpu.VMEM((RET_HEADS, HEAD_DIM, HEAD_DIM), jnp.float32),
            pltpu.VMEM((RET_HEADS, HEAD_DIM, HEAD_DIM), jnp.float32),
            pltpu.VMEM((HIST_ROWS + tokens, D_CONV), jnp.float32),
            pltpu.VMEM((HIST_ROWS, D_CONV), jnp.float32),
            pltpu.VMEM((tokens, D_RET), jnp.bfloat16),
            pltpu.VMEM((tokens, D_RET), jnp.bfloat16),
            pltpu.VMEM((tokens, D_RET), jnp.bfloat16),
            pltpu.VMEM((tokens, D_RET), jnp.bfloat16),
            pltpu.VMEM((tokens, D_RET), jnp.bfloat16),
            pltpu.VMEM((n_hc, chunk, chunk), jnp.bfloat16),
            pltpu.VMEM((n_hc, HEAD_DIM, HEAD_DIM), jnp.bfloat16),
            pltpu.VMEM((tokens, D_MODEL), jnp.bfloat16),
        ],
        compiler_params=pltpu.CompilerParams(
            vmem_limit_bytes=VMEM_LIMIT_BYTES),
        name="hymba_mixer",
    )(x, meta, norm1_g.reshape(1, -1), w_in, conv_w,
      ret_norm_g.reshape(1, -1), w_out, final_g.reshape(1, -1),
      jnp.asarray(cos[N_META:]), jnp.asarray(sin[N_META:]),
      jnp.asarray(np.concatenate([cos[:N_META], sin[:N_META]], axis=0)),
      jnp.asarray(decay), jnp.asarray(xi), jnp.asarray(zeta), jnp.asarray(_meta_zeta()))
```
